```python
import math
import jax, jax.numpy as jnp
from jax import lax
import numpy as np

D_MODEL = 1024
BATCH = 32
SEQ = 256
DEPTH = 1
DEC_BATCH = 8
DEC_SEQ = 1024
PAST_LEN = 512

GRID_W = 64
R_HEADS = 8
R_HEAD = 64
R_WIDTH = R_HEADS * R_HEAD
DECAY_LORA = 64
AAA_LORA = 64
GATE_LORA = 128
LNX_EPS = 64e-5
G_HEADS = 4
G_HEAD = 128
G_WIDTH = G_HEADS * G_HEAD
CONV_K = 3
CHUNK = 64
IN_SIZES = (R_WIDTH, R_WIDTH, R_WIDTH, DECAY_LORA, AAA_LORA, GATE_LORA,
            G_WIDTH, G_WIDTH, G_WIDTH, G_WIDTH, 2 * G_HEADS, 2 * G_HEADS)
IN_COLS = 3 * R_WIDTH + DECAY_LORA + AAA_LORA + GATE_LORA + 4 * G_WIDTH + 4 * G_HEADS
MIX_WIDTH = R_WIDTH + G_WIDTH
N_GROUPS = 4
EXPERTS_PER_GROUP = 8
N_EXPERTS = N_GROUPS * EXPERTS_PER_GROUP
TOP_K = 2
D_EXPERT = 512
MOE_BLOCK = 128
NORM_EPS = 1e-6

kernel_name = 'hybrid_rwkv7_gdn_hmoe_diffusion_step'


def _split_points(sizes):
    pts, acc = [], 0
    for s in sizes[:-1]:
        acc += s
        pts.append(acc)
    return pts


def rms_norm(x, g, eps=NORM_EPS):
    xf = x.astype(jnp.float32)
    y = xf * lax.rsqrt(jnp.mean(xf * xf, -1, keepdims=True) + eps)
    return (y * g.astype(jnp.float32)).astype(x.dtype)


def l2_normalize(x, eps=1e-6):
    xf = x.astype(jnp.float32)
    return xf * lax.rsqrt(jnp.sum(xf * xf, -1, keepdims=True) + eps)


def short_conv(x, w, n_rows):
    B, T, C = x.shape
    xs = x.reshape(B * n_rows, T // n_rows, C)
    pad = (CONV_K - 1) // 2
    y = lax.conv_general_dilated(xs, w[:, None, :].astype(x.dtype), (1,), [(pad, pad)],
                                 dimension_numbers=('NWC', 'WIO', 'NWC'), feature_group_count=C)
    return y.reshape(B, T, C)


def rwkv7_scan(r, w, k, v, a, b, s0, reverse):
    f32 = jnp.float32
    xs = tuple(jnp.moveaxis(t.astype(f32), 1, 0) for t in (r, w, k, v, a, b))

    def step(S, inp):
        r_t, w_t, k_t, v_t, a_t, b_t = inp
        sa = jnp.einsum('bhvk,bhk->bhv', S, a_t)
        S = S * w_t[:, :, None, :] + sa[..., None] * b_t[:, :, None, :] + v_t[..., None] * k_t[:, :, None, :]
        return S, jnp.einsum('bhvk,bhk->bhv', S, r_t)

    S, ys = lax.scan(step, s0.astype(f32), xs, reverse=reverse)
    return jnp.moveaxis(ys, 0, 1), S


def gated_delta_chunked(q, k, v, beta, g, s0):
    B, T, H, K = q.shape
    V = v.shape[-1]
    N = T // CHUNK
    f32 = jnp.float32

    def chunks(t):
        t = t.astype(f32).reshape(B, N, CHUNK, H, *t.shape[3:])
        return jnp.moveaxis(jnp.moveaxis(t, 1, 0), 3, 2)

    qc, kc, vc, bc, gc = chunks(q), chunks(k), chunks(v), chunks(beta), chunks(g)
    G = jnp.cumsum(gc, -1)
    idx = jnp.arange(CHUNK)
    incl = idx[:, None] >= idx[None, :]
    strict = idx[:, None] > idx[None, :]
    decay = jnp.exp(jnp.where(incl, G[..., :, None] - G[..., None, :], -jnp.inf))
    kb = kc * bc[..., None]
    M = jnp.where(strict, jnp.einsum('nbhik,nbhjk->nbhij', kb, kc) * decay, 0.0)
    A = M + jnp.eye(CHUNK, dtype=f32)

    def solve(rhs):
        return lax.linalg.triangular_solve(A, rhs, left_side=True, lower=True, unit_diagonal=True)

    u = solve(vc * bc[..., None])
    w = solve(kb * jnp.exp(G)[..., None])
    qk = jnp.einsum('nbhik,nbhjk->nbhij', qc, kc) * decay
    q_dec = qc * jnp.exp(G)[..., None]
    k_dec = kc * jnp.exp(G[..., -1:] - G)[..., None]
    g_last = jnp.exp(G[..., -1])

    def step(S, inp):
        q_n, k_n, u_n, w_n, qk_n, gl_n = inp
        v_new = u_n - jnp.einsum('bhck,bhkv->bhcv', w_n, S)
        o = jnp.einsum('bhck,bhkv->bhcv', q_n, S) + jnp.einsum('bhij,bhjv->bhiv', qk_n, v_new)
        S = S * gl_n[..., None, None] + jnp.einsum('bhck,bhcv->bhkv', k_n, v_new)
        return S, o

    S, o = lax.scan(step, s0.astype(f32), (q_dec, k_dec, u, w, qk, g_last))
    o = jnp.moveaxis(jnp.moveaxis(o, 2, 3), 0, 1).reshape(B, T, H, V)
    return o, S


def rwkv7_group(r, k, v, wl, al, gl, s0, lp):
    B, T, _ = r.shape
    f32 = jnp.float32
    heads = lambda t: t.reshape(B, T, R_HEADS, R_HEAD)
    gate = jax.nn.sigmoid(gl) @ lp['rwkv_g2']
    wl = jnp.tanh(wl)
    kk = l2_normalize(heads(k * lp['rwkv_k_k']))
    rh, vh = heads(r), heads(v)
    y_scan, bonus, finals = 0.0, 0.0, []
    for d in range(2):
        w_log = -jax.nn.softplus(-(lp['rwkv_w0'][d] + wl @ lp['rwkv_w2'][d]).astype(f32)) - 0.5
        decay = jnp.exp(-jnp.exp(w_log))
        a = jax.nn.sigmoid(lp['rwkv_a0'][d] + al @ lp['rwkv_a2'][d])
        kd = heads(k * (1.0 + (a - 1.0) * lp['rwkv_k_a']))
        yd, sd = rwkv7_scan(rh, heads(decay), kd, vh, -kk, kk * heads(a), s0[:, d], reverse=(d == 1))
        y_scan = y_scan + yd
        bonus = bonus + jnp.sum((rh * kd * lp['rwkv_r_k']).astype(f32), -1, keepdims=True) * vh.astype(f32)
        finals.append(sd.astype(r.dtype))
    mu = jnp.mean(y_scan, -1, keepdims=True)
    var = jnp.mean(jnp.square(y_scan - mu), -1, keepdims=True)
    yn = (y_scan - mu) * lax.rsqrt(var + LNX_EPS)
    yn = yn * lp['rwkv_lnx_g'].astype(f32).reshape(R_HEADS, R_HEAD) + lp['rwkv_lnx_b'].astype(f32).reshape(R_HEADS, R_HEAD)
    out = (yn + bonus).reshape(B, T, R_WIDTH) * gate.astype(f32)
    return out.astype(r.dtype), jnp.stack(finals, 1)


def gdn_group(q, k, v, z, beta_raw, alpha_raw, s0, lp, n_rows):
    B, T, _ = q.shape
    f32 = jnp.float32
    qkv = jax.nn.silu(short_conv(jnp.concatenate([q, k, v], -1), lp['gdn_conv'], n_rows))
    q, k, v = jnp.split(qkv, 3, axis=-1)
    hd = lambda t: t.reshape(B, T, G_HEADS, G_HEAD)
    q = l2_normalize(hd(q)) * (G_HEAD ** -0.5)
    k = l2_normalize(hd(k))
    v = hd(v)
    beta = jax.nn.sigmoid(beta_raw.astype(f32)).reshape(B, T, 2, G_HEADS)
    alpha = alpha_raw.astype(f32).reshape(B, T, 2, G_HEADS)
    flip = lambda t: jnp.flip(t, axis=1)
    o, finals = 0.0, []
    for d in range(2):
        g = -jnp.exp(lp['gdn_a_log'][d].astype(f32)) * jax.nn.softplus(alpha[:, :, d] + lp['gdn_dt_bias'][d].astype(f32))
        if d == 0:
            od, sd = gated_delta_chunked(q, k, v, beta[:, :, 0], g, s0[:, 0])
        else:
            od, sd = gated_delta_chunked(flip(q), flip(k), flip(v), flip(beta[:, :, 1]), flip(g), s0[:, 1])
            od = flip(od)
        o = o + od
        finals.append(sd.astype(z.dtype))
    o = rms_norm(o, lp['gdn_norm_g']) * jax.nn.silu(hd(z).astype(f32))
    return o.reshape(B, T, G_WIDTH).astype(z.dtype), jnp.stack(finals, 1)


def token_mixers(h, lp, s_rwkv0, s_gdn0, n_rows):
    proj = h @ lp['w_in']
    (r, k, v, wl, al, gl, q2, k2, v2, z2, beta_raw, alpha_raw) = jnp.split(proj, _split_points(IN_SIZES), axis=-1)
    o_r, s_r = rwkv7_group(r, k, v, wl, al, gl, s_rwkv0, lp)
    o_g, s_g = gdn_group(q2, k2, v2, z2, beta_raw, alpha_raw, s_gdn0, lp, n_rows)
    out = jnp.concatenate([o_r, o_g], -1) @ lp['w_out']
    return out, s_r, s_g


def hier_moe(h, lp):
    B, T, D = h.shape
    NT = B * T
    f32 = jnp.float32
    x = h.reshape(NT, D)
    grp_p = jax.nn.softmax((x @ lp['router_group_w']).astype(f32) + lp['router_group_b'].astype(f32), -1)
    p_grp, grp = lax.top_k(grp_p, 1)
    e_logits = ((x @ lp['router_expert_w']).astype(f32) + lp['router_expert_b'].astype(f32)).reshape(NT, N_GROUPS, EXPERTS_PER_GROUP)
    in_grp = jnp.take_along_axis(e_logits, grp[:, :, None], axis=1)[:, 0]
    p_in, idx_in = lax.top_k(jax.nn.softmax(in_grp, -1), TOP_K)
    gate = p_grp * p_in / jnp.sum(p_in, -1, keepdims=True)
    expert = grp * EXPERTS_PER_GROUP + idx_in
    A = NT * TOP_K
    flat_e = expert.reshape(A)
    flat_tok = jnp.repeat(jnp.arange(NT, dtype=jnp.int32), TOP_K)
    flat_w = gate.reshape(A)
    order = jnp.argsort(flat_e)
    se = flat_e[order]
    counts = jax.ops.segment_sum(jnp.ones((A,), jnp.int32), flat_e, num_segments=N_EXPERTS)
    padded = (counts + MOE_BLOCK - 1) // MOE_BLOCK * MOE_BLOCK
    ends = jnp.cumsum(counts)
    pends = jnp.cumsum(padded)
    rank = jnp.arange(A, dtype=jnp.int32) - (ends - counts)[se]
    dest = (pends - padded)[se] + rank
    nb = -(-A // MOE_BLOCK) + N_EXPERTS
    P = nb * MOE_BLOCK
    buf_tok = jnp.full((P,), NT, jnp.int32).at[dest].set(flat_tok[order])
    buf_w = jnp.zeros((P,), f32).at[dest].set(flat_w[order])
    blk_e = jnp.minimum(jnp.searchsorted(pends, jnp.arange(nb, dtype=jnp.int32) * MOE_BLOCK, side='right'), N_EXPERTS - 1)
    xb = jnp.concatenate([x, jnp.zeros((1, D), x.dtype)], 0)[buf_tok].reshape(nb, MOE_BLOCK, D)
    wg, wu, wd = lp['expert_gate'], lp['expert_up'], lp['expert_down']

    def expert_block(args):
        xi, e = args
        return (jax.nn.silu(xi @ wg[e]) * (xi @ wu[e])) @ wd[e]

    yb = lax.map(expert_block, (xb, blk_e)).reshape(P, D)
    out = jnp.zeros((NT + 1, D), f32).at[buf_tok].add(yb.astype(f32) * buf_w[:, None])
    return out[:NT].reshape(B, T, D).astype(h.dtype)


def block(x, cond, lp, s_rwkv0, s_gdn0, n_rows):
    mod = jax.nn.silu(cond) @ lp['ada_w'] + lp['ada_b']
    sh1, sc1, g1, sh2, sc2, g2 = jnp.split(mod[:, None, :], 6, axis=-1)
    h = rms_norm(x, lp['norm1_g']) * (1.0 + sc1) + sh1
    mix, s_r, s_g = token_mixers(h, lp, s_rwkv0, s_gdn0, n_rows)
    x = x + g1 * mix
    h = rms_norm(x, lp['norm2_g']) * (1.0 + sc2) + sh2
    x = x + g2 * hier_moe(h, lp)
    return x, s_r, s_g


def setup_inputs(seed: int = 0) -> dict:
    key = jax.random.key(seed)
    kit = iter(jax.random.split(key, 48))

    def nrm(shape, scale):
        return jax.random.normal(next(kit), shape, jnp.float32) * scale

    D, L = D_MODEL, DEPTH
    dt = jnp.exp(jax.random.uniform(next(kit), (L, 2, G_HEADS), jnp.float32, math.log(1e-3), math.log(1e-1)))
    a_log = jnp.log(jax.random.uniform(next(kit), (L, 2, G_HEADS), jnp.float32, 1.0, 16.0))
    return {
        'x_prompt': nrm((BATCH, SEQ, D), 1.0),
        'x_sample': nrm((DEC_BATCH, DEC_SEQ, D), 1.0),
        'state_rwkv': nrm((DEC_BATCH, L, 2, R_HEADS, R_HEAD, R_HEAD), 0.5),
        'state_gdn': nrm((DEC_BATCH, L, 2, G_HEADS, G_HEAD, G_HEAD), 0.1),
        'c': nrm((DEC_BATCH, D), 1.0),
        'c_ctx': nrm((D,), 1.0),
        'ada_w': nrm((L, D, 6 * D), D ** -0.5),
        'ada_b': nrm((L, 6 * D), 0.01),
        'norm1_g': 1.0 + nrm((L, D), 0.02),
        'norm2_g': 1.0 + nrm((L, D), 0.02),
        'w_in': nrm((L, D, IN_COLS), D ** -0.5),
        'w_out': nrm((L, MIX_WIDTH, D), MIX_WIDTH ** -0.5),
        'rwkv_w0': nrm((L, 2, R_WIDTH), 1.0),
        'rwkv_w2': nrm((L, 2, DECAY_LORA, R_WIDTH), 0.5 * DECAY_LORA ** -0.5),
        'rwkv_a0': nrm((L, 2, R_WIDTH), 0.1),
        'rwkv_a2': nrm((L, 2, AAA_LORA, R_WIDTH), AAA_LORA ** -0.5),
        'rwkv_g2': nrm((L, GATE_LORA, R_WIDTH), GATE_LORA ** -0.5),
        'rwkv_k_k': 0.85 + nrm((L, R_WIDTH), 0.05),
        'rwkv_k_a': 1.0 + nrm((L, R_WIDTH), 0.05),
        'rwkv_r_k': nrm((L, R_HEADS, R_HEAD), 0.1),
        'rwkv_lnx_g': 1.0 + nrm((L, R_WIDTH), 0.02),
        'rwkv_lnx_b': nrm((L, R_WIDTH), 0.01),
        'gdn_conv': nrm((L, CONV_K, 3 * G_WIDTH), CONV_K ** -0.5),
        'gdn_a_log': a_log,
        'gdn_dt_bias': dt + jnp.log(-jnp.expm1(-dt)),
        'gdn_norm_g': 1.0 + nrm((L, G_HEAD), 0.02),
        'router_group_w': nrm((L, D, N_GROUPS), D ** -0.5),
        'router_group_b': nrm((L, N_GROUPS), 0.01),
        'router_expert_w': nrm((L, D, N_EXPERTS), D ** -0.5),
        'router_expert_b': nrm((L, N_EXPERTS), 0.01),
        'expert_gate': nrm((L, N_EXPERTS, D, D_EXPERT), D ** -0.5),
        'expert_up': nrm((L, N_EXPERTS, D, D_EXPERT), D ** -0.5),
        'expert_down': nrm((L, N_EXPERTS, D_EXPERT, D), D_EXPERT ** -0.5),
        'final_norm_g': 1.0 + nrm((D,), 0.02),
    }


def reference(x_prompt, x_sample, state_rwkv, state_gdn, c, c_ctx, ada_w, ada_b, norm1_g, norm2_g,
              w_in, w_out, rwkv_w0, rwkv_w2, rwkv_a0, rwkv_a2, rwkv_g2, rwkv_k_k, rwkv_k_a, rwkv_r_k,
              rwkv_lnx_g, rwkv_lnx_b, gdn_conv, gdn_a_log, gdn_dt_bias, gdn_norm_g,
              router_group_w, router_group_b, router_expert_w, router_expert_b,
              expert_gate, expert_up, expert_down, final_norm_g):
    layers = [dict(ada_w=ada_w[l], ada_b=ada_b[l], norm1_g=norm1_g[l], norm2_g=norm2_g[l],
                   w_in=w_in[l], w_out=w_out[l], rwkv_w0=rwkv_w0[l], rwkv_w2=rwkv_w2[l],
                   rwkv_a0=rwkv_a0[l], rwkv_a2=rwkv_a2[l], rwkv_g2=rwkv_g2[l], rwkv_k_k=rwkv_k_k[l],
                   rwkv_k_a=rwkv_k_a[l], rwkv_r_k=rwkv_r_k[l], rwkv_lnx_g=rwkv_lnx_g[l],
                   rwkv_lnx_b=rwkv_lnx_b[l], gdn_conv=gdn_conv[l], gdn_a_log=gdn_a_log[l],
                   gdn_dt_bias=gdn_dt_bias[l], gdn_norm_g=gdn_norm_g[l],
                   router_group_w=router_group_w[l], router_group_b=router_group_b[l],
                   router_expert_w=router_expert_w[l], router_expert_b=router_expert_b[l],
                   expert_gate=expert_gate[l], expert_up=expert_up[l], expert_down=expert_down[l])
              for l in range(DEPTH)]

    xc = x_prompt
    bp = xc.shape[0]
    new_r, new_g = [], []
    for l in range(DEPTH):
        zr = jnp.zeros((bp, 2, R_HEADS, R_HEAD, R_HEAD), xc.dtype)
        zg = jnp.zeros((bp, 2, G_HEADS, G_HEAD, G_HEAD), xc.dtype)
        xc, s_r, s_g = block(xc, c_ctx[None, :], layers[l], zr, zg, 1)
        new_r.append(s_r)
        new_g.append(s_g)
    new_state_rwkv = jnp.stack(new_r, 1)
    new_state_gdn = jnp.stack(new_g, 1)
    y_prompt = rms_norm(xc, final_norm_g)

    n_rows = x_sample.shape[1] // GRID_W
    xl = x_sample
    for l in range(DEPTH):
        xl, _, _ = block(xl, c, layers[l], state_rwkv[:, l], state_gdn[:, l], n_rows)
    y_sample = rms_norm(xl, final_norm_g)
    return (y_prompt, y_sample, new_state_rwkv, new_state_gdn)
```

```python
import functools
import math

import jax
import jax.numpy as jnp
from jax import lax
from jax.experimental import pallas as pl
from jax.experimental.pallas import tpu as pltpu

F32 = jnp.float32
BF16 = jnp.bfloat16

D_MODEL = 1024
R_HEADS, R_HEAD = 8, 64
R_WIDTH = R_HEADS * R_HEAD
G_HEADS, G_HEAD = 4, 128
G_WIDTH = G_HEADS * G_HEAD
LNX_EPS = 64e-5
NORM_EPS = 1e-6
N_GROUPS, EXPERTS_PER_GROUP = 4, 8
N_EXPERTS = N_GROUPS * EXPERTS_PER_GROUP
D_EXPERT = 512
GRID_W = 64

CHUNK = 64
TOK_TILE = 256
MOE_ROWS = 256
LANES = 128
RW = 3 * R_WIDTH + 256
GW_PAD = 4 * G_WIDTH + LANES
VMEM_LIMIT = 56 * 1024 * 1024


def _cparams(sem):
    return pltpu.CompilerParams(dimension_semantics=sem, vmem_limit_bytes=VMEM_LIMIT)


def _sigmoid(x):
    return 1.0 / (1.0 + jnp.exp(-x))


def _silu(x):
    return x * _sigmoid(x)


def _softplus(x):
    return jnp.maximum(x, 0.0) + jnp.log(1.0 + jnp.exp(-jnp.abs(x)))


def _dg(a, b, ca, cb):
    return lax.dot_general(a, b, (((ca,), (cb,)), ((), ())), preferred_element_type=F32)


def _split2(x):
    hi = x.astype(BF16)
    lo = (x - hi.astype(F32)).astype(BF16)
    return hi, lo


def _split3(x):
    h1 = x.astype(BF16)
    r1 = x - h1.astype(F32)
    h2 = r1.astype(BF16)
    h3 = (r1 - h2.astype(F32)).astype(BF16)
    return h1, h2, h3


def _mm(a, b, ca=1, cb=0, hp=False):
    if not hp:
        return _dg(a.astype(BF16), b.astype(BF16), ca, cb)
    ah, al = _split2(a)
    bh, bl = _split2(b)
    return _dg(ah, bh, ca, cb) + (_dg(ah, bl, ca, cb) + _dg(al, bh, ca, cb))


def _mm_exact_lhs(m_bf, x, ca=1, cb=0):
    h1, h2, h3 = _split3(x)
    return _dg(m_bf, h1, ca, cb) + (_dg(m_bf, h2, ca, cb) + _dg(m_bf, h3, ca, cb))


def _mm_exact_rhs(x, m_bf):
    h1, h2, h3 = _split3(x)
    return _dg(h1, m_bf, 1, 0) + (_dg(h2, m_bf, 1, 0) + _dg(h3, m_bf, 1, 0))


def _iota(shape, axis):
    return lax.broadcasted_iota(jnp.int32, shape, axis)


def _tri_masks(n, rev):
    t = _iota((n, n), 0)
    s = _iota((n, n), 1)
    if rev:
        return t < s, t <= s
    return t > s, t >= s


def _level_masks(n):
    t = _iota((n, n), 0)
    u = _iota((n, n), 1)
    masks = []
    s = 1
    while s < n:
        masks.append((t // (2 * s) == u // (2 * s)) & (t // s != u // s))
        s *= 2
    return masks


def _unit_tri_inverse(m, masks, hp):
    n = m.shape[0]
    eye = (_iota((n, n), 0) == _iota((n, n), 1)).astype(F32)
    x = eye - jnp.where(masks[0], m, 0.0)
    for mask in masks[1:]:
        c = jnp.where(mask, m, 0.0)
        x = x - _mm(x, _mm(c, x, hp=hp), hp=hp)
    return x


def _mod_kernel(c_ref, w_ref, b_ref, o_ref):
    s = _silu(c_ref[...])
    o_ref[...] = _mm(s, w_ref[...], hp=True) + b_ref[...]


def _mod_call(cond, ada_w, ada_b):
    rows = cond.shape[0]
    n = ada_w.shape[1]
    bn = 1024
    return pl.pallas_call(
        _mod_kernel,
        out_shape=jax.ShapeDtypeStruct((rows, n), F32),
        grid=(n // bn,),
        in_specs=[pl.BlockSpec((rows, D_MODEL), lambda j: (0, 0)),
                  pl.BlockSpec((D_MODEL, bn), lambda j: (0, j)),
                  pl.BlockSpec((1, bn), lambda j: (0, j))],
        out_specs=pl.BlockSpec((rows, bn), lambda j: (0, j)),
        compiler_params=_cparams(("arbitrary",)),
        name="mod",
    )(cond, ada_w, ada_b)


def _inproj_kernel(x_ref, mod_ref, g_ref, w_ref, or_ref, og_ref):
    x = x_ref[...]
    ms = jnp.mean(x * x, axis=-1, keepdims=True)
    y = x * lax.rsqrt(ms + NORM_EPS) * g_ref[...]
    m = mod_ref[0]
    h = y * (1.0 + m[:, D_MODEL:2 * D_MODEL]) + m[:, :D_MODEL]
    p = jnp.dot(h.astype(BF16), w_ref[...], preferred_element_type=F32)
    or_ref[...] = p[:, :RW]
    og_ref[...] = p[:, RW:]


def _inproj_call(x2, mod3, mod_row, norm_g, w_in_bf):
    nt = x2.shape[0]
    ncol = w_in_bf.shape[1]
    return pl.pallas_call(
        _inproj_kernel,
        out_shape=(jax.ShapeDtypeStruct((nt, RW), F32),
                   jax.ShapeDtypeStruct((nt, GW_PAD), F32)),
        grid=(nt // TOK_TILE,),
        in_specs=[pl.BlockSpec((TOK_TILE, D_MODEL), lambda i: (i, 0)),
                  pl.BlockSpec((1, 1, 6 * D_MODEL), lambda i: (mod_row(i), 0, 0)),
                  pl.BlockSpec((1, D_MODEL), lambda i: (0, 0)),
                  pl.BlockSpec((D_MODEL, ncol), lambda i: (0, 0))],
        out_specs=(pl.BlockSpec((TOK_TILE, RW), lambda i: (i, 0)),
                   pl.BlockSpec((TOK_TILE, GW_PAD), lambda i: (i, 0))),
        compiler_params=_cparams(("arbitrary",)),
        name="inproj",
    )(x2, mod3, norm_g, w_in_bf)


def _head_blocks(width, head):
    return (_iota((width, width), 0) // head == _iota((width, width), 1) // head).astype(BF16)


def _rwkv_kernel(p_ref, s0_ref, w0_ref, wd_ref, a0_ref, wa_ref, g2_ref, kk_ref, ka_ref, rk_ref,
                 lng_ref, lnb_ref, o_ref, st_ref, y_scr, bon_scr, *, seq, hp):
    C = CHUNK
    nc = seq // C
    blk = _head_blocks(R_WIDTH, R_HEAD)
    lane128 = _iota((C, LANES), 1)
    k_k = kk_ref[...]
    k_a = ka_ref[...]
    r_k = rk_ref[...]
    exp_m05 = math.exp(-0.5)
    levels = _level_masks(C)

    def direction(d, rev):
        strict, incl = _tri_masks(C, rev)
        tri = incl.astype(BF16)
        w0 = w0_ref[d]
        a0 = a0_ref[d]
        wdec = wd_ref[d]
        waaa = wa_ref[d]

        def step(i, s_all):
            c = (nc - 1 - i) if rev else i
            row0 = pl.multiple_of(c * C, C)
            rows = pl.ds(row0, C)
            rr = p_ref[0, rows, 0:R_WIDTH]
            kx = p_ref[0, rows, R_WIDTH:2 * R_WIDTH]
            vv = p_ref[0, rows, 2 * R_WIDTH:3 * R_WIDTH]
            la = p_ref[0, rows, 3 * R_WIDTH:3 * R_WIDTH + LANES]
            lhs = jnp.where(lane128 < 64, jnp.tanh(la), la)
            logw = -exp_m05 * _sigmoid(w0 + _mm(lhs, wdec, hp=True))
            a = _sigmoid(a0 + _mm(lhs, waaa, hp=True))
            kkr = kx * k_k
            kk = kkr * lax.rsqrt(_mm_exact_rhs(kkr * kkr, blk) + 1e-6)
            kd = kx * (1.0 + (a - 1.0) * k_a)
            bvec = kk * a
            gi = _mm_exact_lhs(tri, logw)
            ge = gi - logw
            gtot = gi[0:1] if rev else gi[C - 1:C]
            rt = rr * jnp.exp(gi)
            at = -kk * jnp.exp(ge)
            en = jnp.exp(-gi)
            bt = bvec * en
            kt = kd * en
            ee = jnp.exp(gtot - gi)
            bh = bvec * ee
            kh = kd * ee
            gc = jnp.exp(gtot)
            bonus = _mm_exact_rhs(rr * kd * r_k, blk) * vv
            ys, snew = [], []
            for h in range(R_HEADS):
                sl = slice(h * R_HEAD, (h + 1) * R_HEAD)
                ar = jnp.concatenate([at[:, sl], rt[:, sl]], axis=0)
                bk = jnp.concatenate([bt[:, sl], kt[:, sl]], axis=0)
                pm = _mm(ar, bk, 1, 1, hp=hp)
                aab = jnp.where(strict, pm[:C, :C], 0.0)
                aak = jnp.where(strict, pm[:C, C:], 0.0)
                lrb = jnp.where(incl, pm[C:, :C], 0.0)
                lrk = jnp.where(incl, pm[C:, C:], 0.0)
                tm = _unit_tri_inverse(-aab, levels, hp)
                sh = s_all[:, sl]
                x0 = _mm(ar, sh, 1, 1, hp=hp)
                vh = vv[:, sl]
                u = _mm(tm, x0[:C] + _mm(aak, vh, hp=hp), hp=hp)
                uv = jnp.concatenate([u, vh], axis=0)
                y = x0[C:] + _mm(jnp.concatenate([lrb, lrk], axis=1), uv)
                bkh = jnp.concatenate([bh[:, sl], kh[:, sl]], axis=0)
                snew.append(sh * gc[:, sl] + _mm(uv, bkh, 0, 0, hp=hp))
                ys.append(y)
            y_all = jnp.concatenate(ys, axis=1)
            if d == 0:
                y_scr[rows, :] = y_all
                bon_scr[rows, :] = bonus
            else:
                y_scr[rows, :] = y_scr[rows, :] + y_all
                bon_scr[rows, :] = bon_scr[rows, :] + bonus
            return jnp.concatenate(snew, axis=1)

        s_fin = lax.fori_loop(0, nc, step, s0_ref[0, d])
        st_ref[0, d] = s_fin

    direction(0, False)
    direction(1, True)

    g2 = g2_ref[...]
    lng = lng_ref[...]
    lnb = lnb_ref[...]
    inv_n = 1.0 / R_HEAD
    fb = min(seq, 256)

    def finish(j, carry):
        rows = pl.ds(pl.multiple_of(j * fb, fb), fb)
        y = y_scr[rows, :]
        mu = _mm_exact_rhs(y, blk) * inv_n
        yc = y - mu
        var = _mm_exact_rhs(yc * yc, blk) * inv_n
        yn = yc * lax.rsqrt(var + LNX_EPS) * lng + lnb
        gl = p_ref[0, rows, 3 * R_WIDTH + LANES:3 * R_WIDTH + 2 * LANES]
        gate = _mm(_sigmoid(gl), g2)
        o_ref[0, rows, :] = (yn + bon_scr[rows, :]) * gate
        return carry

    lax.fori_loop(0, seq // fb, finish, 0)


def _rwkv_call(proj_r, s0, w, *, batch, seq, hp):
    p3 = proj_r.reshape(batch, seq, RW)
    kern = functools.partial(_rwkv_kernel, seq=seq, hp=hp)
    full = lambda a: pl.BlockSpec(a.shape, lambda b: (0,) * a.ndim)
    ws = [w["w0"], w["wd"], w["a0"], w["wa"], w["g2"], w["k_k"], w["k_a"], w["r_k"], w["lnx_g"], w["lnx_b"]]
    o, st = pl.pallas_call(
        kern,
        out_shape=(jax.ShapeDtypeStruct((batch, seq, R_WIDTH), F32),
                   jax.ShapeDtypeStruct((batch, 2, R_HEAD, R_WIDTH), F32)),
        grid=(batch,),
        in_specs=[pl.BlockSpec((1, seq, RW), lambda b: (b, 0, 0)),
                  pl.BlockSpec((1, 2, R_HEAD, R_WIDTH), lambda b: (b, 0, 0, 0))] + [full(a) for a in ws],
        out_specs=(pl.BlockSpec((1, seq, R_WIDTH), lambda b: (b, 0, 0)),
                   pl.BlockSpec((1, 2, R_HEAD, R_WIDTH), lambda b: (b, 0, 0, 0))),
        scratch_shapes=[pltpu.VMEM((seq, R_WIDTH), F32), pltpu.VMEM((seq, R_WIDTH), F32)],
        compiler_params=_cparams(("arbitrary",)),
        name="rwkv_hp" if hp else "rwkv",
    )(p3, s0, *ws)
    return o.reshape(batch * seq, R_WIDTH), st


def _rwkv_weights(lp):
    zeros = jnp.zeros((2, 64, R_WIDTH), F32)
    row = lambda a: jnp.asarray(a, F32).reshape(1, R_WIDTH)
    return {
        "w0": jnp.asarray(lp["rwkv_w0"], F32).reshape(2, 1, R_WIDTH),
        "wd": jnp.concatenate([jnp.asarray(lp["rwkv_w2"], F32), zeros], axis=1),
        "a0": jnp.asarray(lp["rwkv_a0"], F32).reshape(2, 1, R_WIDTH),
        "wa": jnp.concatenate([zeros, jnp.asarray(lp["rwkv_a2"], F32)], axis=1),
        "g2": jnp.asarray(lp["rwkv_g2"], F32),
        "k_k": row(lp["rwkv_k_k"]), "k_a": row(lp["rwkv_k_a"]), "r_k": row(lp["rwkv_r_k"]),
        "lnx_g": row(lp["rwkv_lnx_g"]), "lnx_b": row(lp["rwkv_lnx_b"]),
    }


def _gdn_kernel(p_ref, s0_ref, cw_ref, alog_ref, dtb_ref, ng_ref, o_ref, st_ref,
                q_scr, k_scr, v_scr, o_scr, *, seq, row_len, hp):
    C = CHUNK
    nc = seq // C
    fb = min(seq, 256)
    W = G_WIDTH
    q_scale = G_HEAD ** -0.5

    cw0 = cw_ref[0:1, :]
    cw1 = cw_ref[1:2, :]
    cw2 = cw_ref[2:3, :]
    tpos = _iota((fb, 1), 0) % row_len
    is_first = tpos == 0
    is_last = tpos == row_len - 1

    def prep(j, carry):
        rows = pl.ds(pl.multiple_of(j * fb, fb), fb)
        x = p_ref[0, rows, 0:3 * W]
        xm = jnp.where(is_first, 0.0, pltpu.roll(x, 1, 0))
        xp = jnp.where(is_last, 0.0, pltpu.roll(x, fb - 1, 0))
        y = _silu(cw0 * xm + cw1 * x + cw2 * xp)
        for h in range(G_HEADS):
            qh = y[:, h * G_HEAD:(h + 1) * G_HEAD]
            kh = y[:, W + h * G_HEAD:W + (h + 1) * G_HEAD]
            qn = qh * lax.rsqrt(jnp.sum(qh * qh, axis=-1, keepdims=True) + 1e-6) * q_scale
            kn = kh * lax.rsqrt(jnp.sum(kh * kh, axis=-1, keepdims=True) + 1e-6)
            q_scr[rows, h * G_HEAD:(h + 1) * G_HEAD] = qn
            k_scr[rows, h * G_HEAD:(h + 1) * G_HEAD] = kn
        v_scr[rows, :] = y[:, 2 * W:3 * W]
        return carry

    lax.fori_loop(0, seq // fb, prep, 0)

    levels = _level_masks(C)
    alog = alog_ref[...]
    dtb = dtb_ref[...]
    sel_row = _iota((LANES, W), 0)
    sel_head = _iota((LANES, W), 1) // G_HEAD

    def direction(d, rev):
        strict, incl = _tri_masks(C, rev)
        tri = incl.astype(BF16)
        e_beta = (sel_row == 4 * d + sel_head).astype(BF16)
        e_g = (sel_row == 8 + 4 * d + sel_head).astype(BF16)

        def step(i, s_all):
            c = (nc - 1 - i) if rev else i
            rows = pl.ds(pl.multiple_of(c * C, C), C)
            ba = p_ref[0, rows, 4 * W:4 * W + LANES]
            beta_b = _mm_exact_rhs(_sigmoid(ba), e_beta)
            g_b = _mm_exact_rhs(-jnp.exp(alog) * _softplus(ba + dtb), e_g)
            qa = q_scr[rows, :]
            ka = k_scr[rows, :]
            va = v_scr[rows, :]
            gcum = _mm_exact_lhs(tri, g_b)
            os_, snew = [], []
            for h in range(G_HEADS):
                hs = slice(h * G_HEAD, (h + 1) * G_HEAD)
                qh, kh, vh, bb = qa[:, hs], ka[:, hs], va[:, hs], beta_b[:, hs]
                gc = gcum[:, hs]
                g3 = _split3(g_b[:, h * G_HEAD:h * G_HEAD + C])
                grow = _dg(g3[0], tri, 0, 1) + (_dg(g3[1], tri, 0, 1) + _dg(g3[2], tri, 0, 1))
                dm = jnp.where(incl, jnp.exp(jnp.where(incl, gc[:, :C] - grow, 0.0)), 0.0)
                kb = kh * bb
                kk = _mm(jnp.concatenate([kb, qh], axis=0), kh, 1, 1, hp=hp)
                m = jnp.where(strict, kk[:C] * dm, 0.0)
                tm = _unit_tri_inverse(m, levels, hp)
                qk = kk[C:] * dm
                eg = jnp.exp(gc)
                uw = _mm(tm, jnp.concatenate([vh * bb, kb * eg], axis=1), hp=hp)
                u, w = uw[:, :G_HEAD], uw[:, G_HEAD:]
                glast = gc[0:1] if rev else gc[C - 1:C]
                sh = s_all[:, hs]
                ws = _mm(jnp.concatenate([w, qh * eg], axis=0), sh, hp=hp)
                v_new = u - ws[:C]
                os_.append(ws[C:] + _mm(qk, v_new))
                k_dec = kh * jnp.exp(glast - gc)
                snew.append(sh * jnp.exp(glast) + _mm(k_dec, v_new, 0, 0, hp=hp))
            o_all = jnp.concatenate(os_, axis=1)
            if d == 0:
                o_scr[rows, :] = o_all
            else:
                o_scr[rows, :] = o_scr[rows, :] + o_all
            return jnp.concatenate(snew, axis=1)

        st_ref[0, d] = lax.fori_loop(0, nc, step, s0_ref[0, d])

    direction(0, False)
    direction(1, True)

    ng = ng_ref[...]

    def finish(j, carry):
        rows = pl.ds(pl.multiple_of(j * fb, fb), fb)
        o = o_scr[rows, :]
        z = p_ref[0, rows, 3 * W:4 * W]
        for h in range(G_HEADS):
            hs = slice(h * G_HEAD, (h + 1) * G_HEAD)
            oh = o[:, hs]
            ms = jnp.mean(oh * oh, axis=-1, keepdims=True)
            o_ref[0, rows, hs] = oh * lax.rsqrt(ms + NORM_EPS) * ng * _silu(z[:, hs])
        return carry

    lax.fori_loop(0, seq // fb, finish, 0)


def _gdn_call(proj_g, s0, w, *, batch, seq, row_len, hp):
    p3 = proj_g.reshape(batch, seq, GW_PAD)
    kern = functools.partial(_gdn_kernel, seq=seq, row_len=row_len, hp=hp)
    full = lambda a: pl.BlockSpec(a.shape, lambda b: (0,) * a.ndim)
    ws = [w["conv"], w["alog"], w["dtb"], w["norm_g"]]
    o, st = pl.pallas_call(
        kern,
        out_shape=(jax.ShapeDtypeStruct((batch, seq, G_WIDTH), F32),
                   jax.ShapeDtypeStruct((batch, 2, G_HEAD, G_WIDTH), F32)),
        grid=(batch,),
        in_specs=[pl.BlockSpec((1, seq, GW_PAD), lambda b: (b, 0, 0)),
                  pl.BlockSpec((1, 2, G_HEAD, G_WIDTH), lambda b: (b, 0, 0, 0))] + [full(a) for a in ws],
        out_specs=(pl.BlockSpec((1, seq, G_WIDTH), lambda b: (b, 0, 0)),
                   pl.BlockSpec((1, 2, G_HEAD, G_WIDTH), lambda b: (b, 0, 0, 0))),
        scratch_shapes=[pltpu.VMEM((seq, G_WIDTH), F32)] * 4,
        compiler_params=_cparams(("arbitrary",)),
        name="gdn_hp" if hp else "gdn",
    )(p3, s0, *ws)
    return o.reshape(batch * seq, G_WIDTH), st


def _gdn_weights(lp):
    lanes = jnp.zeros((1, LANES), F32)
    return {
        "conv": jnp.asarray(lp["gdn_conv"], F32),
        "alog": lanes.at[0, 8:16].set(jnp.asarray(lp["gdn_a_log"], F32).reshape(8)),
        "dtb": lanes.at[0, 8:16].set(jnp.asarray(lp["gdn_dt_bias"], F32).reshape(8)),
        "norm_g": jnp.asarray(lp["gdn_norm_g"], F32).reshape(1, G_HEAD),
    }


def _outproj_kernel(x_ref, or_ref, og_ref, mod_ref, wo_ref, n2_ref, rw_ref, rb_ref,
                    x1_ref, h2_ref, rid_ref, rgate_ref):
    m = mod_ref[0]
    g1 = m[:, 2 * D_MODEL:3 * D_MODEL]
    sh2 = m[:, 3 * D_MODEL:4 * D_MODEL]
    sc2 = m[:, 4 * D_MODEL:5 * D_MODEL]
    mix = (jnp.dot(or_ref[...].astype(BF16), wo_ref[0:R_WIDTH, :], preferred_element_type=F32)
           + jnp.dot(og_ref[...].astype(BF16), wo_ref[R_WIDTH:, :], preferred_element_type=F32))
    x1 = x_ref[...] + g1 * mix
    x1_ref[...] = x1
    ms = jnp.mean(x1 * x1, axis=-1, keepdims=True)
    h2 = x1 * lax.rsqrt(ms + NORM_EPS) * n2_ref[...] * (1.0 + sc2) + sh2
    h2_ref[...] = h2

    logits = _mm(h2, rw_ref[...], hp=True) + rb_ref[...]
    lane = _iota(logits.shape, 1)
    neg = jnp.float32(-1e30)
    big = jnp.int32(1 << 20)

    def first_argmax(v):
        mx = jnp.max(v, axis=-1, keepdims=True)
        idx = jnp.min(jnp.where(v == mx, lane, big), axis=-1, keepdims=True)
        return mx, idx

    lg = jnp.where(lane < N_GROUPS, logits, neg)
    mg, grp = first_argmax(lg)
    p_grp = 1.0 / jnp.sum(jnp.where(lane < N_GROUPS, jnp.exp(lg - mg), 0.0), axis=-1, keepdims=True)
    in_grp = (lane >= N_GROUPS) & (lane < N_GROUPS + N_EXPERTS) & ((lane - N_GROUPS) // EXPERTS_PER_GROUP == grp)
    le = jnp.where(in_grp, logits, neg)
    m1, i1 = first_argmax(le)
    m2, i2 = first_argmax(jnp.where(lane == i1, neg, le))
    e2 = jnp.exp(m2 - m1)
    w1 = p_grp / (1.0 + e2)
    w2 = p_grp * e2 / (1.0 + e2)
    rid_ref[...] = jnp.where(lane == 0, i1 - N_GROUPS, jnp.where(lane == 1, i2 - N_GROUPS, 0))
    rgate_ref[...] = jnp.where(lane == 0, w1, jnp.where(lane == 1, w2, 0.0))


def _outproj_call(x2, o_r, o_g, mod3, mod_row, wo_bf, n2g, rw, rb):
    nt = x2.shape[0]
    tile = lambda w: pl.BlockSpec((TOK_TILE, w), lambda i: (i, 0))
    const = lambda a: pl.BlockSpec(a.shape, lambda i: (0,) * a.ndim)
    return pl.pallas_call(
        _outproj_kernel,
        out_shape=(jax.ShapeDtypeStruct((nt, D_MODEL), F32), jax.ShapeDtypeStruct((nt, D_MODEL), F32),
                   jax.ShapeDtypeStruct((nt, LANES), jnp.int32), jax.ShapeDtypeStruct((nt, LANES), F32)),
        grid=(nt // TOK_TILE,),
        in_specs=[tile(D_MODEL), tile(R_WIDTH), tile(G_WIDTH),
                  pl.BlockSpec((1, 1, 6 * D_MODEL), lambda i: (mod_row(i), 0, 0)),
                  const(wo_bf), const(n2g), const(rw), const(rb)],
        out_specs=(tile(D_MODEL), tile(D_MODEL), tile(LANES), tile(LANES)),
        compiler_params=_cparams(("arbitrary",)),
        name="outproj",
    )(x2, o_r, o_g, mod3, wo_bf, n2g, rw, rb)


def _plan_kernel(rid_ref, rank_ref, cnt_ref, carry):
    i = pl.program_id(0)

    @pl.when(i == 0)
    def _():
        carry[...] = jnp.zeros_like(carry)

    rid = rid_ref[...]
    lane = _iota(rid.shape, 1)
    e0 = jnp.sum(jnp.where(lane == 0, rid, 0), axis=-1, keepdims=True)
    e1 = jnp.sum(jnp.where(lane == 1, rid, 0), axis=-1, keepdims=True)
    oh0 = (lane == e0).astype(F32)
    oh1 = (lane == e1).astype(F32)
    oh = oh0 + oh1
    n = rid.shape[0]
    earlier = (_iota((n, n), 0) > _iota((n, n), 1)).astype(BF16)
    before = jnp.dot(earlier, oh.astype(BF16), preferred_element_type=F32) + carry[0:1, :]
    r0 = jnp.sum(oh0 * before, axis=-1, keepdims=True)
    r1 = jnp.sum(oh1 * before, axis=-1, keepdims=True)
    rank_ref[...] = jnp.where(lane == 0, r0, jnp.where(lane == 1, r1, 0.0)).astype(jnp.int32)
    total = carry[0:1, :] + jnp.sum(oh, axis=0, keepdims=True)
    carry[...] = jnp.broadcast_to(total, carry.shape)
    cnt_ref[...] = jnp.broadcast_to(total, cnt_ref.shape).astype(jnp.int32)


def _plan_call(rid):
    nt = rid.shape[0]
    return pl.pallas_call(
        _plan_kernel,
        out_shape=(jax.ShapeDtypeStruct((nt, LANES), jnp.int32), jax.ShapeDtypeStruct((8, LANES), jnp.int32)),
        grid=(nt // TOK_TILE,),
        in_specs=[pl.BlockSpec((TOK_TILE, LANES), lambda i: (i, 0))],
        out_specs=(pl.BlockSpec((TOK_TILE, LANES), lambda i: (i, 0)), pl.BlockSpec((8, LANES), lambda i: (0, 0))),
        scratch_shapes=[pltpu.VMEM((8, LANES), F32)],
        compiler_params=_cparams(("arbitrary",)),
        name="plan",
    )(rid)


def _row_dest(off_ref, ids_ref, rank_ref, j):
    return off_ref[ids_ref[0, 0, j]] + rank_ref[0, 0, j]


def _dispatch_kernel(off_ref, ids_ref, rank_ref, h2_ref, *rest, fill, n_rows):
    if fill:
        xs_ref, zero_scr, sem = rest
    else:
        _, xs_ref, sem = rest
    i = pl.program_id(0)

    if fill:
        @pl.when(i == 0)
        def _():
            zero_scr[...] = jnp.zeros_like(zero_scr)
            nblk = n_rows // MOE_ROWS

            def fill_copy(b):
                return pltpu.make_async_copy(zero_scr, xs_ref.at[pl.ds(b * MOE_ROWS, MOE_ROWS)], sem)

            def start(b, c):
                fill_copy(b).start()
                return c

            def wait(b, c):
                fill_copy(b).wait()
                return c

            lax.fori_loop(0, nblk, start, 0)
            lax.fori_loop(0, nblk, wait, 0)

    def row_copy(t, k):
        d = _row_dest(off_ref, ids_ref, rank_ref, 2 * t + k)
        return pltpu.make_async_copy(h2_ref.at[pl.ds(t, 1)], xs_ref.at[pl.ds(d, 1)], sem)

    def start(t, c):
        row_copy(t, 0).start()
        row_copy(t, 1).start()
        return c

    def wait(t, c):
        row_copy(t, 0).wait()
        row_copy(t, 1).wait()
        return c

    lax.fori_loop(0, TOK_TILE, start, 0)
    lax.fori_loop(0, TOK_TILE, wait, 0)


def _dispatch_call(off, ids3, rank3, h2, xs, n_rows):
    nt = h2.shape[0]
    fill = xs is None
    smem_blk = pl.BlockSpec((1, 1, 2 * TOK_TILE), lambda i: (i, 0, 0), memory_space=pltpu.SMEM)
    kern = functools.partial(_dispatch_kernel, fill=fill, n_rows=n_rows)
    in_specs = [pl.BlockSpec(memory_space=pltpu.SMEM), smem_blk, smem_blk,
                pl.BlockSpec((TOK_TILE, D_MODEL), lambda i: (i, 0))]
    args = [off, ids3, rank3, h2]
    scratch = [pltpu.SemaphoreType.DMA(())]
    aliases = {}
    if fill:
        scratch = [pltpu.VMEM((MOE_ROWS, D_MODEL), F32)] + scratch
    else:
        in_specs.append(pl.BlockSpec(memory_space=pl.ANY))
        args.append(xs)
        aliases = {4: 0}
    return pl.pallas_call(
        kern,
        out_shape=jax.ShapeDtypeStruct((n_rows, D_MODEL), F32),
        grid=(nt // TOK_TILE,),
        in_specs=in_specs,
        out_specs=pl.BlockSpec(memory_space=pl.ANY),
        scratch_shapes=scratch,
        input_output_aliases=aliases,
        compiler_params=_cparams(("arbitrary",)),
        name="dispatch_fill" if fill else "dispatch",
    )(*args)


def _experts_kernel(be_ref, nu_ref, xs_ref, wg_ref, wu_ref, wd_ref, ys_ref, wg_bf, wu_bf, wd_bf):
    b = pl.program_id(0)
    used = b < nu_ref[0]

    @pl.when(used)
    def _():
        prev = be_ref[jnp.maximum(b - 1, 0)]

        @pl.when((b == 0) | (be_ref[b] != prev))
        def _():
            wg_bf[...] = wg_ref[0].astype(BF16)
            wu_bf[...] = wu_ref[0].astype(BF16)
            wd_bf[...] = wd_ref[0].astype(BF16)

        x = xs_ref[...].astype(BF16)
        g = jnp.dot(x, wg_bf[...], preferred_element_type=F32)
        u = jnp.dot(x, wu_bf[...], preferred_element_type=F32)
        h = (_silu(g) * u).astype(BF16)
        ys_ref[...] = jnp.dot(h, wd_bf[...], preferred_element_type=F32)

    @pl.when(jnp.logical_not(used))
    def _():
        ys_ref[...] = jnp.zeros_like(ys_ref)


def _experts_call(blk_e, n_used, xs, wg, wu, wd):
    n_rows = xs.shape[0]
    nb = n_rows // MOE_ROWS
    grid_spec = pltpu.PrefetchScalarGridSpec(
        num_scalar_prefetch=2,
        grid=(nb,),
        in_specs=[pl.BlockSpec((MOE_ROWS, D_MODEL), lambda b, be, nu: (jnp.minimum(b, nu[0] - 1), 0)),
                  pl.BlockSpec((1, D_MODEL, D_EXPERT), lambda b, be, nu: (be[b], 0, 0)),
                  pl.BlockSpec((1, D_MODEL, D_EXPERT), lambda b, be, nu: (be[b], 0, 0)),
                  pl.BlockSpec((1, D_EXPERT, D_MODEL), lambda b, be, nu: (be[b], 0, 0))],
        out_specs=pl.BlockSpec((MOE_ROWS, D_MODEL), lambda b, be, nu: (b, 0)),
        scratch_shapes=[pltpu.VMEM((D_MODEL, D_EXPERT), BF16), pltpu.VMEM((D_MODEL, D_EXPERT), BF16),
                        pltpu.VMEM((D_EXPERT, D_MODEL), BF16)],
    )
    return pl.pallas_call(
        _experts_kernel,
        out_shape=jax.ShapeDtypeStruct((n_rows, D_MODEL), F32),
        grid_spec=grid_spec,
        compiler_params=_cparams(("arbitrary",)),
        name="experts",
    )(blk_e, n_used, xs, wg, wu, wd)


def _combine_kernel(off_ref, ids_ref, rank_ref, x1_ref, gate_ref, mod_ref, fg_ref, ys_ref, y_ref, buf, sem):
    def row_copy(t, k):
        d = _row_dest(off_ref, ids_ref, rank_ref, 2 * t + k)
        return pltpu.make_async_copy(ys_ref.at[pl.ds(d, 1)], buf.at[k, pl.ds(t, 1)], sem)

    def start(t, c):
        row_copy(t, 0).start()
        row_copy(t, 1).start()
        return c

    def wait(t, c):
        row_copy(t, 0).wait()
        row_copy(t, 1).wait()
        return c

    lax.fori_loop(0, TOK_TILE, start, 0)
    lax.fori_loop(0, TOK_TILE, wait, 0)

    gate = gate_ref[...]
    lane = _iota(gate.shape, 1)
    w0 = jnp.sum(jnp.where(lane == 0, gate, 0.0), axis=-1, keepdims=True)
    w1 = jnp.sum(jnp.where(lane == 1, gate, 0.0), axis=-1, keepdims=True)
    g2 = mod_ref[0][:, 5 * D_MODEL:6 * D_MODEL]
    y = x1_ref[...] + g2 * (w0 * buf[0] + w1 * buf[1])
    ms = jnp.mean(y * y, axis=-1, keepdims=True)
    y_ref[...] = y * lax.rsqrt(ms + NORM_EPS) * fg_ref[...]


def _combine_call(off, ids3, rank3, x1, rgate, mod3, mod_row, fg, ys):
    nt = x1.shape[0]
    smem_blk = pl.BlockSpec((1, 1, 2 * TOK_TILE), lambda i: (i, 0, 0), memory_space=pltpu.SMEM)
    return pl.pallas_call(
        _combine_kernel,
        out_shape=jax.ShapeDtypeStruct((nt, D_MODEL), F32),
        grid=(nt // TOK_TILE,),
        in_specs=[pl.BlockSpec(memory_space=pltpu.SMEM), smem_blk, smem_blk,
                  pl.BlockSpec((TOK_TILE, D_MODEL), lambda i: (i, 0)),
                  pl.BlockSpec((TOK_TILE, LANES), lambda i: (i, 0)),
                  pl.BlockSpec((1, 1, 6 * D_MODEL), lambda i: (mod_row(i), 0, 0)),
                  pl.BlockSpec((1, D_MODEL), lambda i: (0, 0)),
                  pl.BlockSpec(memory_space=pl.ANY)],
        out_specs=pl.BlockSpec((TOK_TILE, D_MODEL), lambda i: (i, 0)),
        scratch_shapes=[pltpu.VMEM((2, TOK_TILE, D_MODEL), F32), pltpu.SemaphoreType.DMA(())],
        compiler_params=_cparams(("arbitrary",)),
        name="combine",
    )(off, ids3, rank3, x1, rgate, mod3, fg, ys)


def kernel(x_prompt, x_sample, state_rwkv, state_gdn, c, c_ctx, ada_w, ada_b, norm1_g, norm2_g, w_in, w_out,
           rwkv_w0, rwkv_w2, rwkv_a0, rwkv_a2, rwkv_g2, rwkv_k_k, rwkv_k_a, rwkv_r_k, rwkv_lnx_g, rwkv_lnx_b,
           gdn_conv, gdn_a_log, gdn_dt_bias, gdn_norm_g, router_group_w, router_group_b, router_expert_w,
           router_expert_b, expert_gate, expert_up, expert_down, final_norm_g):
    assert ada_w.shape[0] == 1, "one layer"
    bp, tp, _ = x_prompt.shape
    bs, ts, _ = x_sample.shape
    lp = dict(rwkv_w0=rwkv_w0[0], rwkv_w2=rwkv_w2[0], rwkv_a0=rwkv_a0[0], rwkv_a2=rwkv_a2[0], rwkv_g2=rwkv_g2[0],
              rwkv_k_k=rwkv_k_k[0], rwkv_k_a=rwkv_k_a[0], rwkv_r_k=rwkv_r_k[0], rwkv_lnx_g=rwkv_lnx_g[0],
              rwkv_lnx_b=rwkv_lnx_b[0], gdn_conv=gdn_conv[0], gdn_a_log=gdn_a_log[0],
              gdn_dt_bias=gdn_dt_bias[0], gdn_norm_g=gdn_norm_g[0])
    rw_w = _rwkv_weights(lp)
    gd_w = _gdn_weights(lp)

    n_cond = 1 + bs
    cond = jnp.concatenate([c_ctx[None, :], c, jnp.zeros((16 - n_cond, D_MODEL), F32)], axis=0)
    mod = _mod_call(cond, ada_w[0], ada_b)
    mod3 = mod.reshape(16, 1, 6 * D_MODEL)

    in_cols = w_in.shape[2]
    w_in_bf = jnp.pad(w_in[0], ((0, 0), (0, RW + GW_PAD - in_cols))).astype(BF16)
    w_out_bf = w_out[0].astype(BF16)
    n1g = norm1_g.reshape(1, D_MODEL)
    n2g = norm2_g.reshape(1, D_MODEL)
    fg = final_norm_g.reshape(1, D_MODEL)
    rw = jnp.zeros((D_MODEL, LANES), F32).at[:, :N_GROUPS].set(router_group_w[0])
    rw = rw.at[:, N_GROUPS:N_GROUPS + N_EXPERTS].set(router_expert_w[0])
    rb = jnp.zeros((1, LANES), F32).at[0, :N_GROUPS].set(router_group_b[0])
    rb = rb.at[0, N_GROUPS:N_GROUPS + N_EXPERTS].set(router_expert_b[0])

    tiles_per_sample = ts // TOK_TILE
    passes = [
        dict(x=x_prompt.reshape(bp * tp, D_MODEL), batch=bp, seq=tp, row_len=tp,
             mod_row=lambda i: 0,
             s_r=jnp.zeros((bp, 2, R_HEAD, R_WIDTH), F32), s_g=jnp.zeros((bp, 2, G_HEAD, G_WIDTH), F32)),
        dict(x=x_sample.reshape(bs * ts, D_MODEL), batch=bs, seq=ts, row_len=GRID_W,
             mod_row=lambda i: 1 + i // tiles_per_sample,
             s_r=jnp.transpose(state_rwkv[:, 0], (0, 1, 3, 2, 4)).reshape(bs, 2, R_HEAD, R_WIDTH),
             s_g=jnp.transpose(state_gdn[:, 0], (0, 1, 3, 2, 4)).reshape(bs, 2, G_HEAD, G_WIDTH)),
    ]

    for p in passes:
        proj_r, proj_g = _inproj_call(p["x"], mod3, p["mod_row"], n1g, w_in_bf)
        o_r, p["st_r"] = _rwkv_call(proj_r, p["s_r"], rw_w, batch=p["batch"], seq=p["seq"], hp=False)
        o_g, p["st_g"] = _gdn_call(proj_g, p["s_g"], gd_w, batch=p["batch"], seq=p["seq"],
                                   row_len=p["row_len"], hp=False)
        p["x1"], p["h2"], p["rid"], p["rgate"] = _outproj_call(
            p["x"], o_r, o_g, mod3, p["mod_row"], w_out_bf, n2g, rw, rb)

    rid = jnp.concatenate([p["rid"] for p in passes], axis=0)
    rank, cnt = _plan_call(rid)
    counts = cnt[0, :N_EXPERTS]
    padded = (counts + MOE_ROWS - 1) // MOE_ROWS * MOE_ROWS
    ends = jnp.cumsum(padded)
    off = (ends - padded).astype(jnp.int32)
    nt_all = rid.shape[0]
    n_blocks = (2 * nt_all) // MOE_ROWS + N_EXPERTS
    n_rows = n_blocks * MOE_ROWS
    n_used = (ends[-1] // MOE_ROWS).astype(jnp.int32)
    blk_start = jnp.minimum(jnp.arange(n_blocks, dtype=jnp.int32), n_used - 1) * MOE_ROWS
    blk_e = jnp.minimum(jnp.searchsorted(ends, blk_start, side="right"), N_EXPERTS - 1).astype(jnp.int32)

    ids3 = rid[:, :2].reshape(nt_all // TOK_TILE, 1, 2 * TOK_TILE)
    rank3 = rank[:, :2].reshape(nt_all // TOK_TILE, 1, 2 * TOK_TILE)
    xs = None
    t0 = 0
    for p in passes:
        nt = p["x"].shape[0] // TOK_TILE
        p["ids3"], p["rank3"] = ids3[t0:t0 + nt], rank3[t0:t0 + nt]
        xs = _dispatch_call(off, p["ids3"], p["rank3"], p["h2"], xs, n_rows)
        t0 += nt
    ys = _experts_call(blk_e, n_used.reshape(1), xs, expert_gate[0], expert_up[0], expert_down[0])
    outs = [_combine_call(off, p["ids3"], p["rank3"], p["x1"], p["rgate"], mod3, p["mod_row"], fg, ys)
            for p in passes]

    y_prompt = outs[0].reshape(bp, tp, D_MODEL)
    y_sample = outs[1].reshape(bs, ts, D_MODEL)
    st_r = passes[0]["st_r"].reshape(bp, 2, R_HEAD, R_HEADS, R_HEAD)
    new_state_rwkv = jnp.transpose(st_r, (0, 1, 3, 2, 4))[:, None]
    st_g = passes[0]["st_g"].reshape(bp, 2, G_HEAD, G_HEADS, G_HEAD)
    new_state_gdn = jnp.transpose(st_g, (0, 1, 3, 2, 4))[:, None]
    return (y_prompt, y_sample, new_state_rwkv, new_state_gdn)
```

```python
import functools
import math

import jax
import jax.numpy as jnp
from jax import lax
from jax.experimental import pallas as pl
from jax.experimental.pallas import tpu as pltpu

F32 = jnp.float32
BF16 = jnp.bfloat16

D_MODEL = 1024
R_HEADS, R_HEAD = 8, 64
R_WIDTH = R_HEADS * R_HEAD
G_HEADS, G_HEAD = 4, 128
G_WIDTH = G_HEADS * G_HEAD
LNX_EPS = 64e-5
NORM_EPS = 1e-6
N_GROUPS, EXPERTS_PER_GROUP = 4, 8
N_EXPERTS = N_GROUPS * EXPERTS_PER_GROUP
D_EXPERT = 512
GRID_W = 64

CHUNK = 64
TOK_TILE = 256
MOE_ROWS = 256
LANES = 128
RW = 3 * R_WIDTH + 256
GW_PAD = 4 * G_WIDTH + LANES
VMEM_LIMIT = 56 * 1024 * 1024


def _cparams(sem):
    return pltpu.CompilerParams(dimension_semantics=sem, vmem_limit_bytes=VMEM_LIMIT)


def _sigmoid(x):
    return 1.0 / (1.0 + jnp.exp(-x))


def _silu(x):
    return x * _sigmoid(x)


def _softplus(x):
    return jnp.maximum(x, 0.0) + jnp.log(1.0 + jnp.exp(-jnp.abs(x)))


def _dg(a, b, ca, cb):
    return lax.dot_general(a, b, (((ca,), (cb,)), ((), ())), preferred_element_type=F32)


def _split2(x):
    hi = x.astype(BF16)
    lo = (x - hi.astype(F32)).astype(BF16)
    return hi, lo


def _split3(x):
    h1 = x.astype(BF16)
    r1 = x - h1.astype(F32)
    h2 = r1.astype(BF16)
    h3 = (r1 - h2.astype(F32)).astype(BF16)
    return h1, h2, h3


def _mm(a, b, ca=1, cb=0, hp=False):
    if not hp:
        return _dg(a.astype(BF16), b.astype(BF16), ca, cb)
    ah, al = _split2(a)
    bh, bl = _split2(b)
    return _dg(ah, bh, ca, cb) + (_dg(ah, bl, ca, cb) + _dg(al, bh, ca, cb))


def _mm_exact_lhs(m_bf, x, ca=1, cb=0):
    h1, h2, h3 = _split3(x)
    return _dg(m_bf, h1, ca, cb) + (_dg(m_bf, h2, ca, cb) + _dg(m_bf, h3, ca, cb))


def _mm_exact_rhs(x, m_bf):
    h1, h2, h3 = _split3(x)
    return _dg(h1, m_bf, 1, 0) + (_dg(h2, m_bf, 1, 0) + _dg(h3, m_bf, 1, 0))


def _iota(shape, axis):
    return lax.broadcasted_iota(jnp.int32, shape, axis)


def _tri_masks(n, rev):
    t = _iota((n, n), 0)
    s = _iota((n, n), 1)
    if rev:
        return t < s, t <= s
    return t > s, t >= s


def _level_masks(n):
    t = _iota((n, n), 0)
    u = _iota((n, n), 1)
    masks = []
    s = 1
    while s < n:
        masks.append((t // (2 * s) == u // (2 * s)) & (t // s != u // s))
        s *= 2
    return masks


def _unit_tri_inverse(ms, masks, hp):
    n = ms[0].shape[0]
    eye = (_iota((n, n), 0) == _iota((n, n), 1)).astype(F32)
    xs = [eye - jnp.where(masks[0], m, 0.0) for m in ms]
    for mask in masks[1:]:
        zs = [_mm(jnp.where(mask, m, 0.0), x, hp=hp) for m, x in zip(ms, xs)]
        xs = [x - _mm(x, z, hp=hp) for x, z in zip(xs, zs)]
    return xs


def _mod_kernel(c_ref, w_ref, b_ref, o_ref):
    s = _silu(c_ref[...])
    o_ref[...] = _mm(s, w_ref[...], hp=True) + b_ref[...]


def _mod_call(cond, ada_w, ada_b):
    rows = cond.shape[0]
    n = ada_w.shape[1]
    bn = 1024
    return pl.pallas_call(
        _mod_kernel,
        out_shape=jax.ShapeDtypeStruct((rows, n), F32),
        grid=(n // bn,),
        in_specs=[pl.BlockSpec((rows, D_MODEL), lambda j: (0, 0)),
                  pl.BlockSpec((D_MODEL, bn), lambda j: (0, j)),
                  pl.BlockSpec((1, bn), lambda j: (0, j))],
        out_specs=pl.BlockSpec((rows, bn), lambda j: (0, j)),
        compiler_params=_cparams(("arbitrary",)),
        name="mod",
    )(cond, ada_w, ada_b)


def _inproj_kernel(x_ref, mod_ref, g_ref, w_ref, or_ref, og_ref):
    x = x_ref[...]
    ms = jnp.mean(x * x, axis=-1, keepdims=True)
    y = x * lax.rsqrt(ms + NORM_EPS) * g_ref[...]
    m = mod_ref[0]
    h = y * (1.0 + m[:, D_MODEL:2 * D_MODEL]) + m[:, :D_MODEL]
    p = jnp.dot(h.astype(BF16), w_ref[...], preferred_element_type=F32)
    or_ref[...] = p[:, :RW]
    og_ref[...] = p[:, RW:]


def _inproj_call(x2, mod3, mod_row, norm_g, w_in_bf):
    nt = x2.shape[0]
    ncol = w_in_bf.shape[1]
    return pl.pallas_call(
        _inproj_kernel,
        out_shape=(jax.ShapeDtypeStruct((nt, RW), F32),
                   jax.ShapeDtypeStruct((nt, GW_PAD), F32)),
        grid=(nt // TOK_TILE,),
        in_specs=[pl.BlockSpec((TOK_TILE, D_MODEL), lambda i: (i, 0)),
                  pl.BlockSpec((1, 1, 6 * D_MODEL), lambda i: (mod_row(i), 0, 0)),
                  pl.BlockSpec((1, D_MODEL), lambda i: (0, 0)),
                  pl.BlockSpec((D_MODEL, ncol), lambda i: (0, 0))],
        out_specs=(pl.BlockSpec((TOK_TILE, RW), lambda i: (i, 0)),
                   pl.BlockSpec((TOK_TILE, GW_PAD), lambda i: (i, 0))),
        compiler_params=_cparams(("arbitrary",)),
        name="inproj",
    )(x2, mod3, norm_g, w_in_bf)


def _head_blocks(width, head):
    return (_iota((width, width), 0) // head == _iota((width, width), 1) // head).astype(BF16)


def _rwkv_kernel(p_ref, s0_ref, w0_ref, wd_ref, a0_ref, wa_ref, g2_ref, kk_ref, ka_ref, rk_ref,
                 lng_ref, lnb_ref, o_ref, st_ref, y_scr, bon_scr, *, seq, hp):
    C = CHUNK
    nc = seq // C
    blk = _head_blocks(R_WIDTH, R_HEAD)
    lane128 = _iota((C, LANES), 1)
    k_k = kk_ref[...]
    k_a = ka_ref[...]
    r_k = rk_ref[...]
    exp_m05 = math.exp(-0.5)
    levels = _level_masks(C)

    t2 = _iota((C, 2 * C), 0)
    s2 = _iota((C, 2 * C), 1) % C
    dirs = []
    for d, rev in ((0, False), (1, True)):
        strict, incl = _tri_masks(C, rev)
        dirs.append(dict(d=d, rev=rev, strict=strict, tri=incl.astype(BF16),
                         incl2=(t2 <= s2) if rev else (t2 >= s2)))
    hd = range(R_HEADS)
    sls = [slice(h * R_HEAD, (h + 1) * R_HEAD) for h in hd]

    def prep(dr, c):
        d, rev, tri = dr["d"], dr["rev"], dr["tri"]
        rows = pl.ds(pl.multiple_of(c * C, C), C)
        rr = p_ref[0, rows, 0:R_WIDTH]
        kx = p_ref[0, rows, R_WIDTH:2 * R_WIDTH]
        vv = p_ref[0, rows, 2 * R_WIDTH:3 * R_WIDTH]
        la = p_ref[0, rows, 3 * R_WIDTH:3 * R_WIDTH + LANES]
        lhs = jnp.where(lane128 < 64, jnp.tanh(la), la)
        logw = -exp_m05 * _sigmoid(w0_ref[d] + _mm(lhs, wd_ref[d], hp=True))
        a = _sigmoid(a0_ref[d] + _mm(lhs, wa_ref[d], hp=True))
        kkr = kx * k_k
        kk = kkr * lax.rsqrt(_mm_exact_rhs(kkr * kkr, blk) + 1e-6)
        kd = kx * (1.0 + (a - 1.0) * k_a)
        bvec = kk * a
        gi = _mm_exact_lhs(tri, logw)
        gtot = gi[0:1] if rev else gi[C - 1:C]
        en = jnp.exp(-gi)
        ee = jnp.exp(gtot - gi)
        return dict(rows=rows, vv=vv, rt=rr * jnp.exp(gi), at=-kk * jnp.exp(gi - logw), bt=bvec * en, kt=kd * en,
                    bh=bvec * ee, kh=kd * ee, gc=jnp.exp(gtot), bonus=_mm_exact_rhs(rr * kd * r_k, blk) * vv)

    def step(i, carry):
        ops = [prep(dirs[0], i), prep(dirs[1], nc - 1 - i)]
        items = [(d, h) for d in (0, 1) for h in hd]
        ars = {(d, h): jnp.concatenate([ops[d]["at"][:, sls[h]], ops[d]["rt"][:, sls[h]]], axis=0) for d, h in items}
        pms = {(d, h): _mm(ars[d, h], jnp.concatenate([ops[d]["bt"][:, sls[h]], ops[d]["kt"][:, sls[h]]], axis=0),
                           1, 1, hp=hp) for d, h in items}
        shs = {(d, h): carry[d][:, sls[h]] for d, h in items}
        x0s = {k: _mm(ars[k], shs[k], 1, 1, hp=hp) for k in items}
        vhs = {(d, h): ops[d]["vv"][:, sls[h]] for d, h in items}
        akv = {(d, h): _mm(jnp.where(dirs[d]["strict"], pms[d, h][:C, C:], 0.0), vhs[d, h], hp=hp) for d, h in items}
        tms = _unit_tri_inverse([jnp.where(dirs[d]["strict"], -pms[d, h][:C, :C], 0.0) for d, h in items], levels, hp)
        us = {k: _mm(tm, x0s[k][:C] + akv[k], hp=hp) for k, tm in zip(items, tms)}
        uvs = {k: jnp.concatenate([us[k], vhs[k]], axis=0) for k in items}
        ys = {(d, h): x0s[d, h][C:] + _mm(jnp.where(dirs[d]["incl2"], pms[d, h][C:], 0.0), uvs[d, h]) for d, h in items}
        snew = {(d, h): shs[d, h] * ops[d]["gc"][:, sls[h]]
                + _mm(uvs[d, h], jnp.concatenate([ops[d]["bh"][:, sls[h]], ops[d]["kh"][:, sls[h]]], axis=0),
                      0, 0, hp=hp) for d, h in items}
        for d in (0, 1):
            y_scr[d, ops[d]["rows"], :] = jnp.concatenate([ys[d, h] for h in hd], axis=1)
            bon_scr[d, ops[d]["rows"], :] = ops[d]["bonus"]
        return tuple(jnp.concatenate([snew[d, h] for h in hd], axis=1) for d in (0, 1))

    s_fin = lax.fori_loop(0, nc, step, (s0_ref[0, 0], s0_ref[0, 1]))
    st_ref[0, 0] = s_fin[0]
    st_ref[0, 1] = s_fin[1]

    g2 = g2_ref[...]
    lng = lng_ref[...]
    lnb = lnb_ref[...]
    inv_n = 1.0 / R_HEAD
    fb = min(seq, 256)

    def finish(j, carry):
        rows = pl.ds(pl.multiple_of(j * fb, fb), fb)
        y = y_scr[0, rows, :] + y_scr[1, rows, :]
        mu = _mm_exact_rhs(y, blk) * inv_n
        yc = y - mu
        var = _mm_exact_rhs(yc * yc, blk) * inv_n
        yn = yc * lax.rsqrt(var + LNX_EPS) * lng + lnb
        gl = p_ref[0, rows, 3 * R_WIDTH + LANES:3 * R_WIDTH + 2 * LANES]
        gate = _mm(_sigmoid(gl), g2)
        o_ref[0, rows, :] = (yn + (bon_scr[0, rows, :] + bon_scr[1, rows, :])) * gate
        return carry

    lax.fori_loop(0, seq // fb, finish, 0)


def _rwkv_call(proj_r, s0, w, *, batch, seq, hp):
    p3 = proj_r.reshape(batch, seq, RW)
    kern = functools.partial(_rwkv_kernel, seq=seq, hp=hp)
    full = lambda a: pl.BlockSpec(a.shape, lambda b: (0,) * a.ndim)
    ws = [w["w0"], w["wd"], w["a0"], w["wa"], w["g2"], w["k_k"], w["k_a"], w["r_k"], w["lnx_g"], w["lnx_b"]]
    o, st = pl.pallas_call(
        kern,
        out_shape=(jax.ShapeDtypeStruct((batch, seq, R_WIDTH), F32),
                   jax.ShapeDtypeStruct((batch, 2, R_HEAD, R_WIDTH), F32)),
        grid=(batch,),
        in_specs=[pl.BlockSpec((1, seq, RW), lambda b: (b, 0, 0)),
                  pl.BlockSpec((1, 2, R_HEAD, R_WIDTH), lambda b: (b, 0, 0, 0))] + [full(a) for a in ws],
        out_specs=(pl.BlockSpec((1, seq, R_WIDTH), lambda b: (b, 0, 0)),
                   pl.BlockSpec((1, 2, R_HEAD, R_WIDTH), lambda b: (b, 0, 0, 0))),
        scratch_shapes=[pltpu.VMEM((2, seq, R_WIDTH), F32), pltpu.VMEM((2, seq, R_WIDTH), F32)],
        compiler_params=_cparams(("arbitrary",)),
        name="rwkv_hp" if hp else "rwkv",
    )(p3, s0, *ws)
    return o.reshape(batch * seq, R_WIDTH), st


def _rwkv_weights(lp):
    zeros = jnp.zeros((2, 64, R_WIDTH), F32)
    row = lambda a: jnp.asarray(a, F32).reshape(1, R_WIDTH)
    return {
        "w0": jnp.asarray(lp["rwkv_w0"], F32).reshape(2, 1, R_WIDTH),
        "wd": jnp.concatenate([jnp.asarray(lp["rwkv_w2"], F32), zeros], axis=1),
        "a0": jnp.asarray(lp["rwkv_a0"], F32).reshape(2, 1, R_WIDTH),
        "wa": jnp.concatenate([zeros, jnp.asarray(lp["rwkv_a2"], F32)], axis=1),
        "g2": jnp.asarray(lp["rwkv_g2"], F32),
        "k_k": row(lp["rwkv_k_k"]), "k_a": row(lp["rwkv_k_a"]), "r_k": row(lp["rwkv_r_k"]),
        "lnx_g": row(lp["rwkv_lnx_g"]), "lnx_b": row(lp["rwkv_lnx_b"]),
    }


def _gdn_kernel(p_ref, s0_ref, cw_ref, alog_ref, dtb_ref, ng_ref, o_ref, st_ref,
                q_scr, k_scr, v_scr, o_scr, *, seq, row_len, hp):
    C = CHUNK
    nc = seq // C
    fb = min(seq, 256)
    W = G_WIDTH
    q_scale = G_HEAD ** -0.5

    cw0 = cw_ref[0:1, :]
    cw1 = cw_ref[1:2, :]
    cw2 = cw_ref[2:3, :]
    tpos = _iota((fb, 1), 0) % row_len
    is_first = tpos == 0
    is_last = tpos == row_len - 1

    def prep(j, carry):
        rows = pl.ds(pl.multiple_of(j * fb, fb), fb)
        x = p_ref[0, rows, 0:3 * W]
        xm = jnp.where(is_first, 0.0, pltpu.roll(x, 1, 0))
        xp = jnp.where(is_last, 0.0, pltpu.roll(x, fb - 1, 0))
        y = _silu(cw0 * xm + cw1 * x + cw2 * xp)
        for h in range(G_HEADS):
            qh = y[:, h * G_HEAD:(h + 1) * G_HEAD]
            kh = y[:, W + h * G_HEAD:W + (h + 1) * G_HEAD]
            qn = qh * lax.rsqrt(jnp.sum(qh * qh, axis=-1, keepdims=True) + 1e-6) * q_scale
            kn = kh * lax.rsqrt(jnp.sum(kh * kh, axis=-1, keepdims=True) + 1e-6)
            q_scr[rows, h * G_HEAD:(h + 1) * G_HEAD] = qn
            k_scr[rows, h * G_HEAD:(h + 1) * G_HEAD] = kn
        v_scr[rows, :] = y[:, 2 * W:3 * W]
        return carry

    lax.fori_loop(0, seq // fb, prep, 0)

    levels = _level_masks(C)
    alog = alog_ref[...]
    dtb = dtb_ref[...]
    sel_row = _iota((LANES, W), 0)
    sel_head = _iota((LANES, W), 1) // G_HEAD

    dirs = []
    for d, rev in ((0, False), (1, True)):
        strict, incl = _tri_masks(C, rev)
        dirs.append(dict(d=d, rev=rev, strict=strict, incl=incl, tri=incl.astype(BF16),
                         e_beta=(sel_row == 4 * d + sel_head).astype(BF16),
                         e_g=(sel_row == 8 + 4 * d + sel_head).astype(BF16)))
    hd = range(G_HEADS)
    hss = [slice(h * G_HEAD, (h + 1) * G_HEAD) for h in hd]
    items = [(d, h) for d in (0, 1) for h in hd]

    def prep_dir(dr, c):
        rows = pl.ds(pl.multiple_of(c * C, C), C)
        ba = p_ref[0, rows, 4 * W:4 * W + LANES]
        beta_b = _mm_exact_rhs(_sigmoid(ba), dr["e_beta"])
        g_b = _mm_exact_rhs(-jnp.exp(alog) * _softplus(ba + dtb), dr["e_g"])
        gcum = _mm_exact_lhs(dr["tri"], g_b)
        return dict(rows=rows, beta=beta_b, g=g_b, gcum=gcum, q=q_scr[rows, :], k=k_scr[rows, :], v=v_scr[rows, :])

    def step(i, carry):
        ops = [prep_dir(dirs[0], i), prep_dir(dirs[1], nc - 1 - i)]
        qhs = {(d, h): ops[d]["q"][:, hss[h]] for d, h in items}
        khs = {(d, h): ops[d]["k"][:, hss[h]] for d, h in items}
        gcs = {(d, h): ops[d]["gcum"][:, hss[h]] for d, h in items}
        bbs = {(d, h): ops[d]["beta"][:, hss[h]] for d, h in items}
        kbs = {k: khs[k] * bbs[k] for k in items}
        dms = {}
        for d, h in items:
            tri, incl = dirs[d]["tri"], dirs[d]["incl"]
            g3 = _split3(ops[d]["g"][:, h * G_HEAD:h * G_HEAD + C])
            grow = _dg(g3[0], tri, 0, 1) + (_dg(g3[1], tri, 0, 1) + _dg(g3[2], tri, 0, 1))
            dms[d, h] = jnp.where(incl, jnp.exp(jnp.where(incl, gcs[d, h][:, :C] - grow, 0.0)), 0.0)
        kks = {k: _mm(jnp.concatenate([kbs[k], qhs[k]], axis=0), khs[k], 1, 1, hp=hp) for k in items}
        tms = _unit_tri_inverse([jnp.where(dirs[d]["strict"], kks[d, h][:C] * dms[d, h], 0.0) for d, h in items],
                                levels, hp)
        egs = {k: jnp.exp(gcs[k]) for k in items}
        uws = {(d, h): _mm(tm, jnp.concatenate([ops[d]["v"][:, hss[h]] * bbs[d, h], kbs[d, h] * egs[d, h]], axis=1),
                           hp=hp) for (d, h), tm in zip(items, tms)}
        shs = {(d, h): carry[d][:, hss[h]] for d, h in items}
        wss = {k: _mm(jnp.concatenate([uws[k][:, G_HEAD:], qhs[k] * egs[k]], axis=0), shs[k], hp=hp) for k in items}
        vns = {k: uws[k][:, :G_HEAD] - wss[k][:C] for k in items}
        outs = {k: wss[k][C:] + _mm(kks[k][C:] * dms[k], vns[k]) for k in items}
        snew = {}
        for d, h in items:
            glast = gcs[d, h][0:1] if dirs[d]["rev"] else gcs[d, h][C - 1:C]
            k_dec = khs[d, h] * jnp.exp(glast - gcs[d, h])
            snew[d, h] = shs[d, h] * jnp.exp(glast) + _mm(k_dec, vns[d, h], 0, 0, hp=hp)
        for d in (0, 1):
            o_scr[d, ops[d]["rows"], :] = jnp.concatenate([outs[d, h] for h in hd], axis=1)
        return tuple(jnp.concatenate([snew[d, h] for h in hd], axis=1) for d in (0, 1))

    s_fin = lax.fori_loop(0, nc, step, (s0_ref[0, 0], s0_ref[0, 1]))
    st_ref[0, 0] = s_fin[0]
    st_ref[0, 1] = s_fin[1]

    ng = ng_ref[...]

    def finish(j, carry):
        rows = pl.ds(pl.multiple_of(j * fb, fb), fb)
        o = o_scr[0, rows, :] + o_scr[1, rows, :]
        z = p_ref[0, rows, 3 * W:4 * W]
        for h in range(G_HEADS):
            hs = slice(h * G_HEAD, (h + 1) * G_HEAD)
            oh = o[:, hs]
            ms = jnp.mean(oh * oh, axis=-1, keepdims=True)
            o_ref[0, rows, hs] = oh * lax.rsqrt(ms + NORM_EPS) * ng * _silu(z[:, hs])
        return carry

    lax.fori_loop(0, seq // fb, finish, 0)


def _gdn_call(proj_g, s0, w, *, batch, seq, row_len, hp):
    p3 = proj_g.reshape(batch, seq, GW_PAD)
    kern = functools.partial(_gdn_kernel, seq=seq, row_len=row_len, hp=hp)
    full = lambda a: pl.BlockSpec(a.shape, lambda b: (0,) * a.ndim)
    ws = [w["conv"], w["alog"], w["dtb"], w["norm_g"]]
    o, st = pl.pallas_call(
        kern,
        out_shape=(jax.ShapeDtypeStruct((batch, seq, G_WIDTH), F32),
                   jax.ShapeDtypeStruct((batch, 2, G_HEAD, G_WIDTH), F32)),
        grid=(batch,),
        in_specs=[pl.BlockSpec((1, seq, GW_PAD), lambda b: (b, 0, 0)),
                  pl.BlockSpec((1, 2, G_HEAD, G_WIDTH), lambda b: (b, 0, 0, 0))] + [full(a) for a in ws],
        out_specs=(pl.BlockSpec((1, seq, G_WIDTH), lambda b: (b, 0, 0)),
                   pl.BlockSpec((1, 2, G_HEAD, G_WIDTH), lambda b: (b, 0, 0, 0))),
        scratch_shapes=[pltpu.VMEM((seq, G_WIDTH), F32)] * 3 + [pltpu.VMEM((2, seq, G_WIDTH), F32)],
        compiler_params=_cparams(("arbitrary",)),
        name="gdn_hp" if hp else "gdn",
    )(p3, s0, *ws)
    return o.reshape(batch * seq, G_WIDTH), st


def _gdn_weights(lp):
    lanes = jnp.zeros((1, LANES), F32)
    return {
        "conv": jnp.asarray(lp["gdn_conv"], F32),
        "alog": lanes.at[0, 8:16].set(jnp.asarray(lp["gdn_a_log"], F32).reshape(8)),
        "dtb": lanes.at[0, 8:16].set(jnp.asarray(lp["gdn_dt_bias"], F32).reshape(8)),
        "norm_g": jnp.asarray(lp["gdn_norm_g"], F32).reshape(1, G_HEAD),
    }


def _outproj_kernel(x_ref, or_ref, og_ref, mod_ref, wo_ref, n2_ref, rw_ref, rb_ref,
                    x1_ref, h2_ref, rid_ref, rgate_ref):
    m = mod_ref[0]
    g1 = m[:, 2 * D_MODEL:3 * D_MODEL]
    sh2 = m[:, 3 * D_MODEL:4 * D_MODEL]
    sc2 = m[:, 4 * D_MODEL:5 * D_MODEL]
    mix = (jnp.dot(or_ref[...].astype(BF16), wo_ref[0:R_WIDTH, :], preferred_element_type=F32)
           + jnp.dot(og_ref[...].astype(BF16), wo_ref[R_WIDTH:, :], preferred_element_type=F32))
    x1 = x_ref[...] + g1 * mix
    x1_ref[...] = x1
    ms = jnp.mean(x1 * x1, axis=-1, keepdims=True)
    h2 = x1 * lax.rsqrt(ms + NORM_EPS) * n2_ref[...] * (1.0 + sc2) + sh2
    h2_ref[...] = h2

    logits = _mm(h2, rw_ref[...], hp=True) + rb_ref[...]
    lane = _iota(logits.shape, 1)
    neg = jnp.float32(-1e30)
    big = jnp.int32(1 << 20)

    def first_argmax(v):
        mx = jnp.max(v, axis=-1, keepdims=True)
        idx = jnp.min(jnp.where(v == mx, lane, big), axis=-1, keepdims=True)
        return mx, idx

    lg = jnp.where(lane < N_GROUPS, logits, neg)
    mg, grp = first_argmax(lg)
    p_grp = 1.0 / jnp.sum(jnp.where(lane < N_GROUPS, jnp.exp(lg - mg), 0.0), axis=-1, keepdims=True)
    in_grp = (lane >= N_GROUPS) & (lane < N_GROUPS + N_EXPERTS) & ((lane - N_GROUPS) // EXPERTS_PER_GROUP == grp)
    le = jnp.where(in_grp, logits, neg)
    m1, i1 = first_argmax(le)
    m2, i2 = first_argmax(jnp.where(lane == i1, neg, le))
    e2 = jnp.exp(m2 - m1)
    w1 = p_grp / (1.0 + e2)
    w2 = p_grp * e2 / (1.0 + e2)
    rid_ref[...] = jnp.where(lane == 0, i1 - N_GROUPS, jnp.where(lane == 1, i2 - N_GROUPS, 0))
    rgate_ref[...] = jnp.where(lane == 0, w1, jnp.where(lane == 1, w2, 0.0))


def _outproj_call(x2, o_r, o_g, mod3, mod_row, wo_bf, n2g, rw, rb):
    nt = x2.shape[0]
    tile = lambda w: pl.BlockSpec((TOK_TILE, w), lambda i: (i, 0))
    const = lambda a: pl.BlockSpec(a.shape, lambda i: (0,) * a.ndim)
    return pl.pallas_call(
        _outproj_kernel,
        out_shape=(jax.ShapeDtypeStruct((nt, D_MODEL), F32), jax.ShapeDtypeStruct((nt, D_MODEL), F32),
                   jax.ShapeDtypeStruct((nt, LANES), jnp.int32), jax.ShapeDtypeStruct((nt, LANES), F32)),
        grid=(nt // TOK_TILE,),
        in_specs=[tile(D_MODEL), tile(R_WIDTH), tile(G_WIDTH),
                  pl.BlockSpec((1, 1, 6 * D_MODEL), lambda i: (mod_row(i), 0, 0)),
                  const(wo_bf), const(n2g), const(rw), const(rb)],
        out_specs=(tile(D_MODEL), tile(D_MODEL), tile(LANES), tile(LANES)),
        compiler_params=_cparams(("arbitrary",)),
        name="outproj",
    )(x2, o_r, o_g, mod3, wo_bf, n2g, rw, rb)


def _plan_kernel(rid_ref, rank_ref, cnt_ref, carry):
    i = pl.program_id(0)

    @pl.when(i == 0)
    def _():
        carry[...] = jnp.zeros_like(carry)

    rid = rid_ref[...]
    lane = _iota(rid.shape, 1)
    e0 = jnp.sum(jnp.where(lane == 0, rid, 0), axis=-1, keepdims=True)
    e1 = jnp.sum(jnp.where(lane == 1, rid, 0), axis=-1, keepdims=True)
    oh0 = (lane == e0).astype(F32)
    oh1 = (lane == e1).astype(F32)
    oh = oh0 + oh1
    n = rid.shape[0]
    earlier = (_iota((n, n), 0) > _iota((n, n), 1)).astype(BF16)
    before = jnp.dot(earlier, oh.astype(BF16), preferred_element_type=F32) + carry[0:1, :]
    r0 = jnp.sum(oh0 * before, axis=-1, keepdims=True)
    r1 = jnp.sum(oh1 * before, axis=-1, keepdims=True)
    rank_ref[...] = jnp.where(lane == 0, r0, jnp.where(lane == 1, r1, 0.0)).astype(jnp.int32)
    total = carry[0:1, :] + jnp.sum(oh, axis=0, keepdims=True)
    carry[...] = jnp.broadcast_to(total, carry.shape)
    cnt_ref[...] = jnp.broadcast_to(total, cnt_ref.shape).astype(jnp.int32)


def _plan_call(rid):
    nt = rid.shape[0]
    return pl.pallas_call(
        _plan_kernel,
        out_shape=(jax.ShapeDtypeStruct((nt, LANES), jnp.int32), jax.ShapeDtypeStruct((8, LANES), jnp.int32)),
        grid=(nt // TOK_TILE,),
        in_specs=[pl.BlockSpec((TOK_TILE, LANES), lambda i: (i, 0))],
        out_specs=(pl.BlockSpec((TOK_TILE, LANES), lambda i: (i, 0)), pl.BlockSpec((8, LANES), lambda i: (0, 0))),
        scratch_shapes=[pltpu.VMEM((8, LANES), F32)],
        compiler_params=_cparams(("arbitrary",)),
        name="plan",
    )(rid)


def _row_dest(off_ref, ids_ref, rank_ref, j):
    return off_ref[ids_ref[0, 0, j]] + rank_ref[0, 0, j]


def _dispatch_kernel(off_ref, ids_ref, rank_ref, h2_ref, *rest, fill, n_rows):
    if fill:
        xs_ref, zero_scr, sem = rest
    else:
        _, xs_ref, sem = rest
    i = pl.program_id(0)

    if fill:
        @pl.when(i == 0)
        def _():
            zero_scr[...] = jnp.zeros_like(zero_scr)
            nblk = n_rows // MOE_ROWS

            def fill_copy(b):
                return pltpu.make_async_copy(zero_scr, xs_ref.at[pl.ds(b * MOE_ROWS, MOE_ROWS)], sem)

            def start(b, c):
                fill_copy(b).start()
                return c

            def wait(b, c):
                fill_copy(b).wait()
                return c

            lax.fori_loop(0, nblk, start, 0)
            lax.fori_loop(0, nblk, wait, 0)

    def row_copy(t, k):
        d = _row_dest(off_ref, ids_ref, rank_ref, 2 * t + k)
        return pltpu.make_async_copy(h2_ref.at[pl.ds(t, 1)], xs_ref.at[pl.ds(d, 1)], sem)

    def start(t, c):
        row_copy(t, 0).start()
        row_copy(t, 1).start()
        return c

    def wait(t, c):
        row_copy(t, 0).wait()
        row_copy(t, 1).wait()
        return c

    lax.fori_loop(0, TOK_TILE, start, 0)
    lax.fori_loop(0, TOK_TILE, wait, 0)


def _dispatch_call(off, ids3, rank3, h2, xs, n_rows):
    nt = h2.shape[0]
    fill = xs is None
    smem_blk = pl.BlockSpec((1, 1, 2 * TOK_TILE), lambda i: (i, 0, 0), memory_space=pltpu.SMEM)
    kern = functools.partial(_dispatch_kernel, fill=fill, n_rows=n_rows)
    in_specs = [pl.BlockSpec(memory_space=pltpu.SMEM), smem_blk, smem_blk,
                pl.BlockSpec((TOK_TILE, D_MODEL), lambda i: (i, 0))]
    args = [off, ids3, rank3, h2]
    scratch = [pltpu.SemaphoreType.DMA(())]
    aliases = {}
    if fill:
        scratch = [pltpu.VMEM((MOE_ROWS, D_MODEL), F32)] + scratch
    else:
        in_specs.append(pl.BlockSpec(memory_space=pl.ANY))
        args.append(xs)
        aliases = {4: 0}
    return pl.pallas_call(
        kern,
        out_shape=jax.ShapeDtypeStruct((n_rows, D_MODEL), F32),
        grid=(nt // TOK_TILE,),
        in_specs=in_specs,
        out_specs=pl.BlockSpec(memory_space=pl.ANY),
        scratch_shapes=scratch,
        input_output_aliases=aliases,
        compiler_params=_cparams(("arbitrary",)),
        name="dispatch_fill" if fill else "dispatch",
    )(*args)


def _experts_kernel(be_ref, nu_ref, xs_ref, wg_ref, wu_ref, wd_ref, ys_ref, wg_bf, wu_bf, wd_bf):
    b = pl.program_id(0)
    used = b < nu_ref[0]

    @pl.when(used)
    def _():
        prev = be_ref[jnp.maximum(b - 1, 0)]

        @pl.when((b == 0) | (be_ref[b] != prev))
        def _():
            wg_bf[...] = wg_ref[0].astype(BF16)
            wu_bf[...] = wu_ref[0].astype(BF16)
            wd_bf[...] = wd_ref[0].astype(BF16)

        x = xs_ref[...].astype(BF16)
        g = jnp.dot(x, wg_bf[...], preferred_element_type=F32)
        u = jnp.dot(x, wu_bf[...], preferred_element_type=F32)
        h = (_silu(g) * u).astype(BF16)
        ys_ref[...] = jnp.dot(h, wd_bf[...], preferred_element_type=F32)

    @pl.when(jnp.logical_not(used))
    def _():
        ys_ref[...] = jnp.zeros_like(ys_ref)


def _experts_call(blk_e, n_used, xs, wg, wu, wd):
    n_rows = xs.shape[0]
    nb = n_rows // MOE_ROWS
    grid_spec = pltpu.PrefetchScalarGridSpec(
        num_scalar_prefetch=2,
        grid=(nb,),
        in_specs=[pl.BlockSpec((MOE_ROWS, D_MODEL), lambda b, be, nu: (jnp.minimum(b, nu[0] - 1), 0)),
                  pl.BlockSpec((1, D_MODEL, D_EXPERT), lambda b, be, nu: (be[b], 0, 0)),
                  pl.BlockSpec((1, D_MODEL, D_EXPERT), lambda b, be, nu: (be[b], 0, 0)),
                  pl.BlockSpec((1, D_EXPERT, D_MODEL), lambda b, be, nu: (be[b], 0, 0))],
        out_specs=pl.BlockSpec((MOE_ROWS, D_MODEL), lambda b, be, nu: (b, 0)),
        scratch_shapes=[pltpu.VMEM((D_MODEL, D_EXPERT), BF16), pltpu.VMEM((D_MODEL, D_EXPERT), BF16),
                        pltpu.VMEM((D_EXPERT, D_MODEL), BF16)],
    )
    return pl.pallas_call(
        _experts_kernel,
        out_shape=jax.ShapeDtypeStruct((n_rows, D_MODEL), F32),
        grid_spec=grid_spec,
        compiler_params=_cparams(("arbitrary",)),
        name="experts",
    )(blk_e, n_used, xs, wg, wu, wd)


def _combine_kernel(off_ref, ids_ref, rank_ref, x1_ref, gate_ref, mod_ref, fg_ref, ys_ref, y_ref, buf, sem):
    def row_copy(t, k):
        d = _row_dest(off_ref, ids_ref, rank_ref, 2 * t + k)
        return pltpu.make_async_copy(ys_ref.at[pl.ds(d, 1)], buf.at[k, pl.ds(t, 1)], sem)

    def start(t, c):
        row_copy(t, 0).start()
        row_copy(t, 1).start()
        return c

    def wait(t, c):
        row_copy(t, 0).wait()
        row_copy(t, 1).wait()
        return c

    lax.fori_loop(0, TOK_TILE, start, 0)
    lax.fori_loop(0, TOK_TILE, wait, 0)

    gate = gate_ref[...]
    lane = _iota(gate.shape, 1)
    w0 = jnp.sum(jnp.where(lane == 0, gate, 0.0), axis=-1, keepdims=True)
    w1 = jnp.sum(jnp.where(lane == 1, gate, 0.0), axis=-1, keepdims=True)
    g2 = mod_ref[0][:, 5 * D_MODEL:6 * D_MODEL]
    y = x1_ref[...] + g2 * (w0 * buf[0] + w1 * buf[1])
    ms = jnp.mean(y * y, axis=-1, keepdims=True)
    y_ref[...] = y * lax.rsqrt(ms + NORM_EPS) * fg_ref[...]


def _combine_call(off, ids3, rank3, x1, rgate, mod3, mod_row, fg, ys):
    nt = x1.shape[0]
    smem_blk = pl.BlockSpec((1, 1, 2 * TOK_TILE), lambda i: (i, 0, 0), memory_space=pltpu.SMEM)
    return pl.pallas_call(
        _combine_kernel,
        out_shape=jax.ShapeDtypeStruct((nt, D_MODEL), F32),
        grid=(nt // TOK_TILE,),
        in_specs=[pl.BlockSpec(memory_space=pltpu.SMEM), smem_blk, smem_blk,
                  pl.BlockSpec((TOK_TILE, D_MODEL), lambda i: (i, 0)),
                  pl.BlockSpec((TOK_TILE, LANES), lambda i: (i, 0)),
                  pl.BlockSpec((1, 1, 6 * D_MODEL), lambda i: (mod_row(i), 0, 0)),
                  pl.BlockSpec((1, D_MODEL), lambda i: (0, 0)),
                  pl.BlockSpec(memory_space=pl.ANY)],
        out_specs=pl.BlockSpec((TOK_TILE, D_MODEL), lambda i: (i, 0)),
        scratch_shapes=[pltpu.VMEM((2, TOK_TILE, D_MODEL), F32), pltpu.SemaphoreType.DMA(())],
        compiler_params=_cparams(("arbitrary",)),
        name="combine",
    )(off, ids3, rank3, x1, rgate, mod3, fg, ys)


def kernel(x_prompt, x_sample, state_rwkv, state_gdn, c, c_ctx, ada_w, ada_b, norm1_g, norm2_g, w_in, w_out,
           rwkv_w0, rwkv_w2, rwkv_a0, rwkv_a2, rwkv_g2, rwkv_k_k, rwkv_k_a, rwkv_r_k, rwkv_lnx_g, rwkv_lnx_b,
           gdn_conv, gdn_a_log, gdn_dt_bias, gdn_norm_g, router_group_w, router_group_b, router_expert_w,
           router_expert_b, expert_gate, expert_up, expert_down, final_norm_g):
    assert ada_w.shape[0] == 1, "one layer"
    bp, tp, _ = x_prompt.shape
    bs, ts, _ = x_sample.shape
    lp = dict(rwkv_w0=rwkv_w0[0], rwkv_w2=rwkv_w2[0], rwkv_a0=rwkv_a0[0], rwkv_a2=rwkv_a2[0], rwkv_g2=rwkv_g2[0],
              rwkv_k_k=rwkv_k_k[0], rwkv_k_a=rwkv_k_a[0], rwkv_r_k=rwkv_r_k[0], rwkv_lnx_g=rwkv_lnx_g[0],
              rwkv_lnx_b=rwkv_lnx_b[0], gdn_conv=gdn_conv[0], gdn_a_log=gdn_a_log[0],
              gdn_dt_bias=gdn_dt_bias[0], gdn_norm_g=gdn_norm_g[0])
    rw_w = _rwkv_weights(lp)
    gd_w = _gdn_weights(lp)

    n_cond = 1 + bs
    cond = jnp.concatenate([c_ctx[None, :], c, jnp.zeros((16 - n_cond, D_MODEL), F32)], axis=0)
    mod = _mod_call(cond, ada_w[0], ada_b)
    mod3 = mod.reshape(16, 1, 6 * D_MODEL)

    in_cols = w_in.shape[2]
    w_in_bf = jnp.pad(w_in[0], ((0, 0), (0, RW + GW_PAD - in_cols))).astype(BF16)
    w_out_bf = w_out[0].astype(BF16)
    n1g = norm1_g.reshape(1, D_MODEL)
    n2g = norm2_g.reshape(1, D_MODEL)
    fg = final_norm_g.reshape(1, D_MODEL)
    rw = jnp.zeros((D_MODEL, LANES), F32).at[:, :N_GROUPS].set(router_group_w[0])
    rw = rw.at[:, N_GROUPS:N_GROUPS + N_EXPERTS].set(router_expert_w[0])
    rb = jnp.zeros((1, LANES), F32).at[0, :N_GROUPS].set(router_group_b[0])
    rb = rb.at[0, N_GROUPS:N_GROUPS + N_EXPERTS].set(router_expert_b[0])

    tiles_per_sample = ts // TOK_TILE
    passes = [
        dict(x=x_prompt.reshape(bp * tp, D_MODEL), batch=bp, seq=tp, row_len=tp,
             mod_row=lambda i: 0,
             s_r=jnp.zeros((bp, 2, R_HEAD, R_WIDTH), F32), s_g=jnp.zeros((bp, 2, G_HEAD, G_WIDTH), F32)),
        dict(x=x_sample.reshape(bs * ts, D_MODEL), batch=bs, seq=ts, row_len=GRID_W,
             mod_row=lambda i: 1 + i // tiles_per_sample,
             s_r=jnp.transpose(state_rwkv[:, 0], (0, 1, 3, 2, 4)).reshape(bs, 2, R_HEAD, R_WIDTH),
             s_g=jnp.transpose(state_gdn[:, 0], (0, 1, 3, 2, 4)).reshape(bs, 2, G_HEAD, G_WIDTH)),
    ]

    for p in passes:
        proj_r, proj_g = _inproj_call(p["x"], mod3, p["mod_row"], n1g, w_in_bf)
        o_r, p["st_r"] = _rwkv_call(proj_r, p["s_r"], rw_w, batch=p["batch"], seq=p["seq"], hp=False)
        o_g, p["st_g"] = _gdn_call(proj_g, p["s_g"], gd_w, batch=p["batch"], seq=p["seq"],
                                   row_len=p["row_len"], hp=False)
        p["x1"], p["h2"], p["rid"], p["rgate"] = _outproj_call(
            p["x"], o_r, o_g, mod3, p["mod_row"], w_out_bf, n2g, rw, rb)

    rid = jnp.concatenate([p["rid"] for p in passes], axis=0)
    rank, cnt = _plan_call(rid)
    counts = cnt[0, :N_EXPERTS]
    padded = (counts + MOE_ROWS - 1) // MOE_ROWS * MOE_ROWS
    ends = jnp.cumsum(padded)
    off = (ends - padded).astype(jnp.int32)
    nt_all = rid.shape[0]
    n_blocks = (2 * nt_all) // MOE_ROWS + N_EXPERTS
    n_rows = n_blocks * MOE_ROWS
    n_used = (ends[-1] // MOE_ROWS).astype(jnp.int32)
    blk_start = jnp.minimum(jnp.arange(n_blocks, dtype=jnp.int32), n_used - 1) * MOE_ROWS
    blk_e = jnp.minimum(jnp.sum((blk_start[:, None] >= ends[None, :]).astype(jnp.int32), axis=1), N_EXPERTS - 1)

    ids3 = rid[:, :2].reshape(nt_all // TOK_TILE, 1, 2 * TOK_TILE)
    rank3 = rank[:, :2].reshape(nt_all // TOK_TILE, 1, 2 * TOK_TILE)
    xs = None
    t0 = 0
    for p in passes:
        nt = p["x"].shape[0] // TOK_TILE
        p["ids3"], p["rank3"] = ids3[t0:t0 + nt], rank3[t0:t0 + nt]
        xs = _dispatch_call(off, p["ids3"], p["rank3"], p["h2"], xs, n_rows)
        t0 += nt
    ys = _experts_call(blk_e, n_used.reshape(1), xs, expert_gate[0], expert_up[0], expert_down[0])
    outs = [_combine_call(off, p["ids3"], p["rank3"], p["x1"], p["rgate"], mod3, p["mod_row"], fg, ys)
            for p in passes]

    y_prompt = outs[0].reshape(bp, tp, D_MODEL)
    y_sample = outs[1].reshape(bs, ts, D_MODEL)
    st_r = passes[0]["st_r"].reshape(bp, 2, R_HEAD, R_HEADS, R_HEAD)
    new_state_rwkv = jnp.transpose(st_r, (0, 1, 3, 2, 4))[:, None]
    st_g = passes[0]["st_g"].reshape(bp, 2, G_HEAD, G_HEADS, G_HEAD)
    new_state_gdn = jnp.transpose(st_g, (0, 1, 3, 2, 4))[:, None]
    return (y_prompt, y_sample, new_state_rwkv, new_state_gdn)
```

```python
import functools
import math

import jax
import jax.numpy as jnp
from jax import lax
from jax.experimental import pallas as pl
from jax.experimental.pallas import tpu as pltpu

F32 = jnp.float32
BF16 = jnp.bfloat16

D_MODEL = 1024
R_HEADS, R_HEAD = 8, 64
R_WIDTH = R_HEADS * R_HEAD
G_HEADS, G_HEAD = 4, 128
G_WIDTH = G_HEADS * G_HEAD
LNX_EPS = 64e-5
NORM_EPS = 1e-6
N_GROUPS, EXPERTS_PER_GROUP = 4, 8
N_EXPERTS = N_GROUPS * EXPERTS_PER_GROUP
D_EXPERT = 512
GRID_W = 64

CHUNK = 64
TOK_TILE = 256
MOE_ROWS = 256
SEQS_PER_STEP = 2
LANES = 128
RW = 3 * R_WIDTH + 256
GW_PAD = 4 * G_WIDTH + LANES
VMEM_LIMIT = 56 * 1024 * 1024


def _cparams(sem):
    return pltpu.CompilerParams(dimension_semantics=sem, vmem_limit_bytes=VMEM_LIMIT)


def _sigmoid(x):
    return 1.0 / (1.0 + jnp.exp(-x))


def _silu(x):
    return x * _sigmoid(x)


def _softplus(x):
    return jnp.maximum(x, 0.0) + jnp.log(1.0 + jnp.exp(-jnp.abs(x)))


def _dg(a, b, ca, cb):
    return lax.dot_general(a, b, (((ca,), (cb,)), ((), ())), preferred_element_type=F32)


def _split2(x):
    hi = x.astype(BF16)
    lo = (x - hi.astype(F32)).astype(BF16)
    return hi, lo


def _split3(x):
    h1 = x.astype(BF16)
    r1 = x - h1.astype(F32)
    h2 = r1.astype(BF16)
    h3 = (r1 - h2.astype(F32)).astype(BF16)
    return h1, h2, h3


def _mm(a, b, ca=1, cb=0, hp=False):
    if not hp:
        return _dg(a.astype(BF16), b.astype(BF16), ca, cb)
    ah, al = _split2(a)
    bh, bl = _split2(b)
    return _dg(ah, bh, ca, cb) + (_dg(ah, bl, ca, cb) + _dg(al, bh, ca, cb))


def _mm_exact_lhs(m_bf, x, ca=1, cb=0):
    h1, h2, h3 = _split3(x)
    return _dg(m_bf, h1, ca, cb) + (_dg(m_bf, h2, ca, cb) + _dg(m_bf, h3, ca, cb))


def _mm_exact_rhs(x, m_bf):
    h1, h2, h3 = _split3(x)
    return _dg(h1, m_bf, 1, 0) + (_dg(h2, m_bf, 1, 0) + _dg(h3, m_bf, 1, 0))


def _group_sum(x, ones_bf):
    hi, lo = _split2(x)
    return _dg(hi, ones_bf, 1, 0) + _dg(lo, ones_bf, 1, 0)


def _iota(shape, axis):
    return lax.broadcasted_iota(jnp.int32, shape, axis)


def _tri_masks(n, rev):
    t = _iota((n, n), 0)
    s = _iota((n, n), 1)
    if rev:
        return t < s, t <= s
    return t > s, t >= s


PACK = 4


def _level_masks(n):
    t = _iota((n, PACK * n), 0)
    u = _iota((n, PACK * n), 1) % n
    levels = []
    s = 1
    while s < n:
        levels.append((t // (2 * s) == u // (2 * s)) & (t // s != u // s))
        s *= 2
    diag = _iota((PACK * n, PACK * n), 0) // n == _iota((PACK * n, PACK * n), 1) // n
    return levels, diag


def _block_diag(x, diag):
    return jnp.where(diag, jnp.concatenate([x] * PACK, axis=0), 0.0).astype(BF16)


def _unit_tri_inverse(ms, masks):
    levels, diag = masks
    n = ms[0].shape[0]
    eye = (_iota((n, PACK * n), 0) == _iota((n, PACK * n), 1) % n).astype(F32)
    xs = [eye - jnp.where(levels[0], m, 0.0) for m in ms]
    for mask in levels[1:]:
        zs = [_dg(jnp.where(mask, m, 0.0).astype(BF16), _block_diag(x, diag), 1, 0) for m, x in zip(ms, xs)]
        xs = [x - _dg(x.astype(BF16), _block_diag(z, diag), 1, 0) for x, z in zip(xs, zs)]
    return xs


def _mod_kernel(c_ref, w_ref, b_ref, o_ref):
    s = _silu(c_ref[...])
    o_ref[...] = _mm(s, w_ref[...], hp=True) + b_ref[...]


def _mod_call(cond, ada_w, ada_b):
    rows = cond.shape[0]
    n = ada_w.shape[1]
    bn = 1024
    return pl.pallas_call(
        _mod_kernel,
        out_shape=jax.ShapeDtypeStruct((rows, n), F32),
        grid=(n // bn,),
        in_specs=[pl.BlockSpec((rows, D_MODEL), lambda j: (0, 0)),
                  pl.BlockSpec((D_MODEL, bn), lambda j: (0, j)),
                  pl.BlockSpec((1, bn), lambda j: (0, j))],
        out_specs=pl.BlockSpec((rows, bn), lambda j: (0, j)),
        compiler_params=_cparams(("arbitrary",)),
        name="mod",
    )(cond, ada_w, ada_b)


def _inproj_kernel(x_ref, mod_ref, g_ref, w_ref, or_ref, og_ref):
    x = x_ref[...]
    ms = jnp.mean(x * x, axis=-1, keepdims=True)
    y = x * lax.rsqrt(ms + NORM_EPS) * g_ref[...]
    m = mod_ref[0]
    h = y * (1.0 + m[:, D_MODEL:2 * D_MODEL]) + m[:, :D_MODEL]
    p = jnp.dot(h.astype(BF16), w_ref[...], preferred_element_type=F32)
    or_ref[...] = p[:, :RW]
    og_ref[...] = p[:, RW:]


def _inproj_call(x2, mod3, mod_row, norm_g, w_in_bf):
    nt = x2.shape[0]
    ncol = w_in_bf.shape[1]
    return pl.pallas_call(
        _inproj_kernel,
        out_shape=(jax.ShapeDtypeStruct((nt, RW), F32),
                   jax.ShapeDtypeStruct((nt, GW_PAD), F32)),
        grid=(nt // TOK_TILE,),
        in_specs=[pl.BlockSpec((TOK_TILE, D_MODEL), lambda i: (i, 0)),
                  pl.BlockSpec((1, 1, 6 * D_MODEL), lambda i: (mod_row(i), 0, 0)),
                  pl.BlockSpec((1, D_MODEL), lambda i: (0, 0)),
                  pl.BlockSpec((D_MODEL, ncol), lambda i: (0, 0))],
        out_specs=(pl.BlockSpec((TOK_TILE, RW), lambda i: (i, 0)),
                   pl.BlockSpec((TOK_TILE, GW_PAD), lambda i: (i, 0))),
        compiler_params=_cparams(("arbitrary",)),
        name="inproj",
    )(x2, mod3, norm_g, w_in_bf)


def _head_blocks(width, head):
    return (_iota((width, width), 0) // head == _iota((width, width), 1) // head).astype(BF16)


def _rwkv_kernel(p_ref, s0_ref, w0_ref, wd_ref, a0_ref, wa_ref, g2_ref, kk_ref, ka_ref, rk_ref,
                 lng_ref, lnb_ref, o_ref, st_ref, y_scr, *, seq, nrow, hp):
    C = CHUNK
    nc = seq // C
    blk = _head_blocks(R_WIDTH, R_HEAD)
    k_k = kk_ref[...]
    k_a = ka_ref[...]
    r_k = rk_ref[...]
    exp_m05 = math.exp(-0.5)
    inv_masks = _level_masks(C)

    def lora_lhs(la):
        return jnp.where(_iota(la.shape, 1) < 64, jnp.tanh(la), la)

    t2 = _iota((C, 2 * C), 0)
    s2 = _iota((C, 2 * C), 1) % C
    dirs = []
    for d, rev in ((0, False), (1, True)):
        strict, incl = _tri_masks(C, rev)
        dirs.append(dict(d=d, rev=rev, strict=strict, tri=incl.astype(BF16),
                         incl2=(t2 <= s2) if rev else (t2 >= s2)))
    hd = range(R_HEADS)
    sls = [slice(h * R_HEAD, (h + 1) * R_HEAD) for h in hd]

    def prep(n, dr, c):
        d, rev, tri = dr["d"], dr["rev"], dr["tri"]
        rows = pl.ds(pl.multiple_of(c * C, C), C)
        rr = p_ref[n, rows, 0:R_WIDTH]
        kx = p_ref[n, rows, R_WIDTH:2 * R_WIDTH]
        vv = p_ref[n, rows, 2 * R_WIDTH:3 * R_WIDTH]
        lhs = lora_lhs(p_ref[n, rows, 3 * R_WIDTH:3 * R_WIDTH + LANES])
        logw = -exp_m05 * _sigmoid(w0_ref[d] + _mm(lhs, wd_ref[d]))
        a = _sigmoid(a0_ref[d] + _mm(lhs, wa_ref[d]))
        kkr = kx * k_k
        kk = kkr * lax.rsqrt(_group_sum(kkr * kkr, blk) + 1e-6)
        kd = kx * (1.0 + (a - 1.0) * k_a)
        bvec = kk * a
        gi = _mm_exact_lhs(tri, logw)
        gtot = gi[0:1] if rev else gi[C - 1:C]
        en = jnp.exp(-gi)
        ee = jnp.exp(gtot - gi)
        return dict(rows=rows, vv=vv, rt=rr * jnp.exp(gi), at=-kk * jnp.exp(gi - logw), bt=bvec * en, kt=kd * en,
                    bh=bvec * ee, kh=kd * ee, gc=jnp.exp(gtot))

    seqs = range(nrow)
    items = [(n, d, h) for n in seqs for d in (0, 1) for h in hd]
    packs = [(n, d, g) for n in seqs for d in (0, 1) for g in range(R_HEADS // PACK)]

    def step(i, carry):
        ops = {(n, d): prep(n, dirs[d], (nc - 1 - i) if d else i) for n in seqs for d in (0, 1)}
        hcol = lambda name, n, d, h: ops[n, d][name][:, sls[h]]
        ars = {(n, d, h): jnp.concatenate([hcol("at", n, d, h), hcol("rt", n, d, h)], axis=0) for n, d, h in items}
        pms = {(n, d, h): _mm(ars[n, d, h], jnp.concatenate([hcol("bt", n, d, h), hcol("kt", n, d, h)], axis=0),
                              1, 1, hp=hp) for n, d, h in items}
        shs = {(n, d, h): carry[2 * n + d][:, sls[h]] for n, d, h in items}
        x0s = {k: _mm(ars[k], shs[k], 1, 1, hp=hp) for k in items}
        vhs = {(n, d, h): hcol("vv", n, d, h) for n, d, h in items}
        akv = {(n, d, h): _mm(jnp.where(dirs[d]["strict"], pms[n, d, h][:C, C:], 0.0), vhs[n, d, h], hp=hp)
               for n, d, h in items}
        packed = [jnp.concatenate([jnp.where(dirs[d]["strict"], -pms[n, d, PACK * g + j][:C, :C], 0.0)
                                   for j in range(PACK)], axis=1) for n, d, g in packs]
        tcat = dict(zip(packs, _unit_tri_inverse(packed, inv_masks)))
        tms = {(n, d, h): tcat[n, d, h // PACK][:, (h % PACK) * C:(h % PACK + 1) * C] for n, d, h in items}
        us = {k: _mm(tms[k], x0s[k][:C] + akv[k], hp=hp) for k in items}
        uvs = {k: jnp.concatenate([us[k], vhs[k]], axis=0) for k in items}
        ys = {(n, d, h): x0s[n, d, h][C:] + _mm(jnp.where(dirs[d]["incl2"], pms[n, d, h][C:], 0.0), uvs[n, d, h])
              for n, d, h in items}
        snew = {(n, d, h): shs[n, d, h] * hcol("gc", n, d, h)
                + _mm(uvs[n, d, h], jnp.concatenate([hcol("bh", n, d, h), hcol("kh", n, d, h)], axis=0),
                      0, 0, hp=hp) for n, d, h in items}
        for n in seqs:
            for d in (0, 1):
                y_scr[n, d, ops[n, d]["rows"], :] = jnp.concatenate([ys[n, d, h] for h in hd], axis=1)
        return tuple(jnp.concatenate([snew[n, d, h] for h in hd], axis=1) for n in seqs for d in (0, 1))

    s_fin = lax.fori_loop(0, nc, step, tuple(s0_ref[n, d] for n in seqs for d in (0, 1)))
    for n in seqs:
        for d in (0, 1):
            st_ref[n, d] = s_fin[2 * n + d]

    g2 = g2_ref[...]
    lng = lng_ref[...]
    lnb = lnb_ref[...]
    inv_n = 1.0 / R_HEAD
    fb = min(seq, 256)

    def finish(j, carry):
        rows = pl.ds(pl.multiple_of(j * fb, fb), fb)
        for n in seqs:
            y = y_scr[n, 0, rows, :] + y_scr[n, 1, rows, :]
            mu = _group_sum(y, blk) * inv_n
            yc = y - mu
            var = _group_sum(yc * yc, blk) * inv_n
            yn = yc * lax.rsqrt(var + LNX_EPS) * lng + lnb
            lhs = lora_lhs(p_ref[n, rows, 3 * R_WIDTH:3 * R_WIDTH + LANES])
            a_sum = _sigmoid(a0_ref[0] + _mm(lhs, wa_ref[0])) + _sigmoid(a0_ref[1] + _mm(lhs, wa_ref[1]))
            rr = p_ref[n, rows, 0:R_WIDTH]
            kx = p_ref[n, rows, R_WIDTH:2 * R_WIDTH]
            kd_sum = kx * (2.0 + (a_sum - 2.0) * k_a)
            bonus = _group_sum(rr * kd_sum * r_k, blk) * p_ref[n, rows, 2 * R_WIDTH:3 * R_WIDTH]
            gl = p_ref[n, rows, 3 * R_WIDTH + LANES:3 * R_WIDTH + 2 * LANES]
            gate = _mm(_sigmoid(gl), g2)
            o_ref[n, rows, :] = (yn + bonus) * gate
        return carry

    lax.fori_loop(0, seq // fb, finish, 0)


def _rwkv_call(proj_r, s0, w, *, batch, seq, hp):
    p3 = proj_r.reshape(batch, seq, RW)
    nrow = SEQS_PER_STEP
    kern = functools.partial(_rwkv_kernel, seq=seq, nrow=nrow, hp=hp)
    full = lambda a: pl.BlockSpec(a.shape, lambda b: (0,) * a.ndim)
    ws = [w["w0"], w["wd"], w["a0"], w["wa"], w["g2"], w["k_k"], w["k_a"], w["r_k"], w["lnx_g"], w["lnx_b"]]
    o, st = pl.pallas_call(
        kern,
        out_shape=(jax.ShapeDtypeStruct((batch, seq, R_WIDTH), F32),
                   jax.ShapeDtypeStruct((batch, 2, R_HEAD, R_WIDTH), F32)),
        grid=(batch // nrow,),
        in_specs=[pl.BlockSpec((nrow, seq, RW), lambda b: (b, 0, 0), pipeline_mode=pl.Buffered(1)),
                  pl.BlockSpec((nrow, 2, R_HEAD, R_WIDTH), lambda b: (b, 0, 0, 0))] + [full(a) for a in ws],
        out_specs=(pl.BlockSpec((nrow, seq, R_WIDTH), lambda b: (b, 0, 0)),
                   pl.BlockSpec((nrow, 2, R_HEAD, R_WIDTH), lambda b: (b, 0, 0, 0))),
        scratch_shapes=[pltpu.VMEM((nrow, 2, seq, R_WIDTH), F32)],
        compiler_params=_cparams(("arbitrary",)),
        name="rwkv_hp" if hp else "rwkv",
    )(p3, s0, *ws)
    return o.reshape(batch * seq, R_WIDTH), st


def _rwkv_weights(lp):
    zeros = jnp.zeros((2, 64, R_WIDTH), F32)
    row = lambda a: jnp.asarray(a, F32).reshape(1, R_WIDTH)
    return {
        "w0": jnp.asarray(lp["rwkv_w0"], F32).reshape(2, 1, R_WIDTH),
        "wd": jnp.concatenate([jnp.asarray(lp["rwkv_w2"], F32), zeros], axis=1),
        "a0": jnp.asarray(lp["rwkv_a0"], F32).reshape(2, 1, R_WIDTH),
        "wa": jnp.concatenate([zeros, jnp.asarray(lp["rwkv_a2"], F32)], axis=1),
        "g2": jnp.asarray(lp["rwkv_g2"], F32),
        "k_k": row(lp["rwkv_k_k"]), "k_a": row(lp["rwkv_k_a"]), "r_k": row(lp["rwkv_r_k"]),
        "lnx_g": row(lp["rwkv_lnx_g"]), "lnx_b": row(lp["rwkv_lnx_b"]),
    }


def _gdn_kernel(p_ref, s0_ref, cw_ref, alog_ref, dtb_ref, ng_ref, o_ref, st_ref,
                q_scr, k_scr, v_scr, o_scr, *, seq, nrow, row_len, hp):
    C = CHUNK
    nc = seq // C
    fb = min(seq, 256)
    W = G_WIDTH
    q_scale = G_HEAD ** -0.5
    seqs = range(nrow)
    hd = range(G_HEADS)
    hss = [slice(h * G_HEAD, (h + 1) * G_HEAD) for h in hd]

    cw0 = cw_ref[0:1, :]
    cw1 = cw_ref[1:2, :]
    cw2 = cw_ref[2:3, :]
    tpos = _iota((fb, 1), 0) % row_len
    is_first = tpos == 0
    is_last = tpos == row_len - 1

    def prep(j, carry):
        rows = pl.ds(pl.multiple_of(j * fb, fb), fb)
        for n in seqs:
            x = p_ref[n, rows, 0:3 * W]
            xm = jnp.where(is_first, 0.0, pltpu.roll(x, 1, 0))
            xp = jnp.where(is_last, 0.0, pltpu.roll(x, fb - 1, 0))
            y = _silu(cw0 * xm + cw1 * x + cw2 * xp)
            for h in hd:
                qh = y[:, hss[h]]
                kh = y[:, W + h * G_HEAD:W + (h + 1) * G_HEAD]
                q_scr[n, rows, hss[h]] = qh * lax.rsqrt(jnp.sum(qh * qh, axis=-1, keepdims=True) + 1e-6) * q_scale
                k_scr[n, rows, hss[h]] = kh * lax.rsqrt(jnp.sum(kh * kh, axis=-1, keepdims=True) + 1e-6)
            v_scr[n, rows, :] = y[:, 2 * W:3 * W]
        return carry

    lax.fori_loop(0, seq // fb, prep, 0)

    assert G_HEADS == PACK
    inv_masks = _level_masks(C)
    alog = alog_ref[...]
    dtb = dtb_ref[...]
    sel_row = _iota((LANES, W), 0)
    sel_head = _iota((LANES, W), 1) // G_HEAD
    pk_row = _iota((LANES, PACK * C), 0)
    pk_head = _iota((LANES, PACK * C), 1) // C
    pk_t = _iota((C, PACK * C), 0)
    pk_s = _iota((C, PACK * C), 1) % C

    dirs = []
    for d, rev in ((0, False), (1, True)):
        dirs.append(dict(
            d=d, rev=rev, tri=_tri_masks(C, rev)[1].astype(BF16),
            strict=(pk_t < pk_s) if rev else (pk_t > pk_s),
            incl=(pk_t <= pk_s) if rev else (pk_t >= pk_s),
            upto=(pk_t >= pk_s) if rev else (pk_t <= pk_s),
            e_beta=(sel_row == 4 * d + sel_head).astype(BF16),
            e_g=(sel_row == 8 + 4 * d + sel_head).astype(BF16),
            e_g_pk=(pk_row == 8 + 4 * d + pk_head).astype(BF16)))
    items = [(n, d, h) for n in seqs for d in (0, 1) for h in hd]
    packs = [(n, d) for n in seqs for d in (0, 1)]

    def prep_dir(n, dr, c):
        rows = pl.ds(pl.multiple_of(c * C, C), C)
        ba = p_ref[n, rows, 4 * W:4 * W + LANES]
        gval = -jnp.exp(alog) * _softplus(ba + dtb)
        beta_b = _mm_exact_rhs(_sigmoid(ba), dr["e_beta"])
        gcum = _mm_exact_lhs(dr["tri"], _mm_exact_rhs(gval, dr["e_g"]))
        g_pk = _mm_exact_rhs(gval, dr["e_g_pk"])
        g_col = _mm_exact_lhs(dr["tri"], g_pk)
        g_row = jnp.sum(jnp.where(dr["upto"], g_pk, 0.0), axis=0, keepdims=True)
        return dict(rows=rows, beta=beta_b, gcum=gcum,
                    decay=jnp.where(dr["incl"], jnp.exp(jnp.where(dr["incl"], g_col - g_row, 0.0)), 0.0),
                    q=q_scr[n, rows, :], k=k_scr[n, rows, :], v=v_scr[n, rows, :])

    def step(i, carry):
        ops = {(n, d): prep_dir(n, dirs[d], (nc - 1 - i) if d else i) for n, d in packs}
        hcol = lambda name, n, d, h: ops[n, d][name][:, hss[h]]
        qhs = {k: hcol("q", *k) for k in items}
        khs = {k: hcol("k", *k) for k in items}
        gcs = {k: hcol("gcum", *k) for k in items}
        bbs = {k: hcol("beta", *k) for k in items}
        kbs = {k: khs[k] * bbs[k] for k in items}
        kks = {k: _mm(jnp.concatenate([kbs[k], qhs[k]], axis=0), khs[k], 1, 1, hp=hp) for k in items}
        kka = {(n, d): jnp.concatenate([kks[n, d, h][:C] for h in hd], axis=1) for n, d in packs}
        tcat = dict(zip(packs, _unit_tri_inverse(
            [jnp.where(dirs[d]["strict"], kka[n, d] * ops[n, d]["decay"], 0.0) for n, d in packs], inv_masks)))
        egs = {k: jnp.exp(gcs[k]) for k in items}
        uws = {(n, d, h): _mm(tcat[n, d][:, h * C:(h + 1) * C],
                              jnp.concatenate([hcol("v", n, d, h) * bbs[n, d, h], kbs[n, d, h] * egs[n, d, h]], axis=1),
                              hp=hp) for n, d, h in items}
        shs = {(n, d, h): carry[2 * n + d][:, hss[h]] for n, d, h in items}
        wss = {k: _mm(jnp.concatenate([uws[k][:, G_HEAD:], qhs[k] * egs[k]], axis=0), shs[k], hp=hp) for k in items}
        vns = {k: uws[k][:, :G_HEAD] - wss[k][:C] for k in items}
        outs = {(n, d, h): wss[n, d, h][C:]
                + _mm(kks[n, d, h][C:] * ops[n, d]["decay"][:, h * C:(h + 1) * C], vns[n, d, h]) for n, d, h in items}
        snew = {}
        for n, d, h in items:
            glast = gcs[n, d, h][0:1] if dirs[d]["rev"] else gcs[n, d, h][C - 1:C]
            k_dec = khs[n, d, h] * jnp.exp(glast - gcs[n, d, h])
            snew[n, d, h] = shs[n, d, h] * jnp.exp(glast) + _mm(k_dec, vns[n, d, h], 0, 0, hp=hp)
        for n, d in packs:
            o_scr[n, d, ops[n, d]["rows"], :] = jnp.concatenate([outs[n, d, h] for h in hd], axis=1)
        return tuple(jnp.concatenate([snew[n, d, h] for h in hd], axis=1) for n, d in packs)

    s_fin = lax.fori_loop(0, nc, step, tuple(s0_ref[n, d] for n, d in packs))
    for j, (n, d) in enumerate(packs):
        st_ref[n, d] = s_fin[j]

    ng = ng_ref[...]

    def finish(j, carry):
        rows = pl.ds(pl.multiple_of(j * fb, fb), fb)
        for n in seqs:
            o = o_scr[n, 0, rows, :] + o_scr[n, 1, rows, :]
            z = p_ref[n, rows, 3 * W:4 * W]
            for h in hd:
                oh = o[:, hss[h]]
                ms = jnp.mean(oh * oh, axis=-1, keepdims=True)
                o_ref[n, rows, hss[h]] = oh * lax.rsqrt(ms + NORM_EPS) * ng * _silu(z[:, hss[h]])
        return carry

    lax.fori_loop(0, seq // fb, finish, 0)


def _gdn_call(proj_g, s0, w, *, batch, seq, row_len, hp):
    p3 = proj_g.reshape(batch, seq, GW_PAD)
    nrow = SEQS_PER_STEP
    kern = functools.partial(_gdn_kernel, seq=seq, nrow=nrow, row_len=row_len, hp=hp)
    full = lambda a: pl.BlockSpec(a.shape, lambda b: (0,) * a.ndim)
    ws = [w["conv"], w["alog"], w["dtb"], w["norm_g"]]
    o, st = pl.pallas_call(
        kern,
        out_shape=(jax.ShapeDtypeStruct((batch, seq, G_WIDTH), F32),
                   jax.ShapeDtypeStruct((batch, 2, G_HEAD, G_WIDTH), F32)),
        grid=(batch // nrow,),
        in_specs=[pl.BlockSpec((nrow, seq, GW_PAD), lambda b: (b, 0, 0), pipeline_mode=pl.Buffered(1)),
                  pl.BlockSpec((nrow, 2, G_HEAD, G_WIDTH), lambda b: (b, 0, 0, 0))] + [full(a) for a in ws],
        out_specs=(pl.BlockSpec((nrow, seq, G_WIDTH), lambda b: (b, 0, 0)),
                   pl.BlockSpec((nrow, 2, G_HEAD, G_WIDTH), lambda b: (b, 0, 0, 0))),
        scratch_shapes=[pltpu.VMEM((nrow, seq, G_WIDTH), F32)] * 3 + [pltpu.VMEM((nrow, 2, seq, G_WIDTH), F32)],
        compiler_params=_cparams(("arbitrary",)),
        name="gdn_hp" if hp else "gdn",
    )(p3, s0, *ws)
    return o.reshape(batch * seq, G_WIDTH), st


def _gdn_weights(lp):
    lanes = jnp.zeros((1, LANES), F32)
    return {
        "conv": jnp.asarray(lp["gdn_conv"], F32),
        "alog": lanes.at[0, 8:16].set(jnp.asarray(lp["gdn_a_log"], F32).reshape(8)),
        "dtb": lanes.at[0, 8:16].set(jnp.asarray(lp["gdn_dt_bias"], F32).reshape(8)),
        "norm_g": jnp.asarray(lp["gdn_norm_g"], F32).reshape(1, G_HEAD),
    }


def _outproj_kernel(x_ref, or_ref, og_ref, mod_ref, wo_ref, n2_ref, rw_ref, rb_ref,
                    x1_ref, h2_ref, rid_ref, rgate_ref):
    m = mod_ref[0]
    g1 = m[:, 2 * D_MODEL:3 * D_MODEL]
    sh2 = m[:, 3 * D_MODEL:4 * D_MODEL]
    sc2 = m[:, 4 * D_MODEL:5 * D_MODEL]
    mix = (jnp.dot(or_ref[...].astype(BF16), wo_ref[0:R_WIDTH, :], preferred_element_type=F32)
           + jnp.dot(og_ref[...].astype(BF16), wo_ref[R_WIDTH:, :], preferred_element_type=F32))
    x1 = x_ref[...] + g1 * mix
    x1_ref[...] = x1
    ms = jnp.mean(x1 * x1, axis=-1, keepdims=True)
    h2 = x1 * lax.rsqrt(ms + NORM_EPS) * n2_ref[...] * (1.0 + sc2) + sh2
    h2_ref[...] = h2

    logits = _mm(h2, rw_ref[...], hp=True) + rb_ref[...]
    lane = _iota(logits.shape, 1)
    neg = jnp.float32(-1e30)
    big = jnp.int32(1 << 20)

    def first_argmax(v):
        mx = jnp.max(v, axis=-1, keepdims=True)
        idx = jnp.min(jnp.where(v == mx, lane, big), axis=-1, keepdims=True)
        return mx, idx

    lg = jnp.where(lane < N_GROUPS, logits, neg)
    mg, grp = first_argmax(lg)
    p_grp = 1.0 / jnp.sum(jnp.where(lane < N_GROUPS, jnp.exp(lg - mg), 0.0), axis=-1, keepdims=True)
    in_grp = (lane >= N_GROUPS) & (lane < N_GROUPS + N_EXPERTS) & ((lane - N_GROUPS) // EXPERTS_PER_GROUP == grp)
    le = jnp.where(in_grp, logits, neg)
    m1, i1 = first_argmax(le)
    m2, i2 = first_argmax(jnp.where(lane == i1, neg, le))
    e2 = jnp.exp(m2 - m1)
    w1 = p_grp / (1.0 + e2)
    w2 = p_grp * e2 / (1.0 + e2)
    rid_ref[...] = jnp.where(lane == 0, i1 - N_GROUPS, jnp.where(lane == 1, i2 - N_GROUPS, 0))
    rgate_ref[...] = jnp.where(lane == 0, w1, jnp.where(lane == 1, w2, 0.0))


def _outproj_call(x2, o_r, o_g, mod3, mod_row, wo_bf, n2g, rw, rb):
    nt = x2.shape[0]
    tile = lambda w: pl.BlockSpec((TOK_TILE, w), lambda i: (i, 0))
    const = lambda a: pl.BlockSpec(a.shape, lambda i: (0,) * a.ndim)
    return pl.pallas_call(
        _outproj_kernel,
        out_shape=(jax.ShapeDtypeStruct((nt, D_MODEL), F32), jax.ShapeDtypeStruct((nt, D_MODEL), F32),
                   jax.ShapeDtypeStruct((nt, LANES), jnp.int32), jax.ShapeDtypeStruct((nt, LANES), F32)),
        grid=(nt // TOK_TILE,),
        in_specs=[tile(D_MODEL), tile(R_WIDTH), tile(G_WIDTH),
                  pl.BlockSpec((1, 1, 6 * D_MODEL), lambda i: (mod_row(i), 0, 0)),
                  const(wo_bf), const(n2g), const(rw), const(rb)],
        out_specs=(tile(D_MODEL), tile(D_MODEL), tile(LANES), tile(LANES)),
        compiler_params=_cparams(("arbitrary",)),
        name="outproj",
    )(x2, o_r, o_g, mod3, wo_bf, n2g, rw, rb)


def _plan_kernel(rid_ref, rank_ref, cnt_ref, carry):
    i = pl.program_id(0)

    @pl.when(i == 0)
    def _():
        carry[...] = jnp.zeros_like(carry)

    rid = rid_ref[...]
    lane = _iota(rid.shape, 1)
    e0 = jnp.sum(jnp.where(lane == 0, rid, 0), axis=-1, keepdims=True)
    e1 = jnp.sum(jnp.where(lane == 1, rid, 0), axis=-1, keepdims=True)
    oh0 = (lane == e0).astype(F32)
    oh1 = (lane == e1).astype(F32)
    oh = oh0 + oh1
    n = rid.shape[0]
    earlier = (_iota((n, n), 0) > _iota((n, n), 1)).astype(BF16)
    before = jnp.dot(earlier, oh.astype(BF16), preferred_element_type=F32) + carry[0:1, :]
    r0 = jnp.sum(oh0 * before, axis=-1, keepdims=True)
    r1 = jnp.sum(oh1 * before, axis=-1, keepdims=True)
    rank_ref[...] = jnp.where(lane == 0, r0, jnp.where(lane == 1, r1, 0.0)).astype(jnp.int32)
    total = carry[0:1, :] + jnp.sum(oh, axis=0, keepdims=True)
    carry[...] = jnp.broadcast_to(total, carry.shape)
    cnt_ref[...] = jnp.broadcast_to(total, cnt_ref.shape).astype(jnp.int32)


def _plan_call(rid):
    nt = rid.shape[0]
    return pl.pallas_call(
        _plan_kernel,
        out_shape=(jax.ShapeDtypeStruct((nt, LANES), jnp.int32), jax.ShapeDtypeStruct((8, LANES), jnp.int32)),
        grid=(nt // TOK_TILE,),
        in_specs=[pl.BlockSpec((TOK_TILE, LANES), lambda i: (i, 0))],
        out_specs=(pl.BlockSpec((TOK_TILE, LANES), lambda i: (i, 0)), pl.BlockSpec((8, LANES), lambda i: (0, 0))),
        scratch_shapes=[pltpu.VMEM((8, LANES), F32)],
        compiler_params=_cparams(("arbitrary",)),
        name="plan",
    )(rid)


def _row_dest(off_ref, ids_ref, rank_ref, j):
    return off_ref[ids_ref[0, 0, j]] + rank_ref[0, 0, j]


def _dispatch_kernel(off_ref, ids_ref, rank_ref, h2_ref, *rest, fill, n_rows):
    if fill:
        xs_ref, zero_scr, sem = rest
    else:
        _, xs_ref, sem = rest
    i = pl.program_id(0)

    if fill:
        @pl.when(i == 0)
        def _():
            zero_scr[...] = jnp.zeros_like(zero_scr)
            nblk = n_rows // MOE_ROWS

            def fill_copy(b):
                return pltpu.make_async_copy(zero_scr, xs_ref.at[pl.ds(b * MOE_ROWS, MOE_ROWS)], sem)

            def start(b, c):
                fill_copy(b).start()
                return c

            def wait(b, c):
                fill_copy(b).wait()
                return c

            lax.fori_loop(0, nblk, start, 0)
            lax.fori_loop(0, nblk, wait, 0)

    def start(t, c):
        for k in range(2):
            d = _row_dest(off_ref, ids_ref, rank_ref, 2 * t + k)
            pltpu.make_async_copy(h2_ref.at[pl.ds(t, 1)], xs_ref.at[pl.ds(d, 1)], sem).start()
        return c

    lax.fori_loop(0, TOK_TILE, start, 0, unroll=8)
    for _ in range(2):
        pltpu.make_async_copy(h2_ref, xs_ref.at[pl.ds(0, TOK_TILE)], sem).wait()


def _dispatch_call(off, ids3, rank3, h2, xs, n_rows):
    nt = h2.shape[0]
    fill = xs is None
    smem_blk = pl.BlockSpec((1, 1, 2 * TOK_TILE), lambda i: (i, 0, 0), memory_space=pltpu.SMEM)
    kern = functools.partial(_dispatch_kernel, fill=fill, n_rows=n_rows)
    in_specs = [pl.BlockSpec(memory_space=pltpu.SMEM), smem_blk, smem_blk,
                pl.BlockSpec((TOK_TILE, D_MODEL), lambda i: (i, 0))]
    args = [off, ids3, rank3, h2]
    scratch = [pltpu.SemaphoreType.DMA(())]
    aliases = {}
    if fill:
        scratch = [pltpu.VMEM((MOE_ROWS, D_MODEL), F32)] + scratch
    else:
        in_specs.append(pl.BlockSpec(memory_space=pl.ANY))
        args.append(xs)
        aliases = {4: 0}
    return pl.pallas_call(
        kern,
        out_shape=jax.ShapeDtypeStruct((n_rows, D_MODEL), F32),
        grid=(nt // TOK_TILE,),
        in_specs=in_specs,
        out_specs=pl.BlockSpec(memory_space=pl.ANY),
        scratch_shapes=scratch,
        input_output_aliases=aliases,
        compiler_params=_cparams(("arbitrary",)),
        name="dispatch_fill" if fill else "dispatch",
    )(*args)


def _experts_kernel(be_ref, nu_ref, xs_ref, wg_ref, wu_ref, wd_ref, ys_ref, wg_bf, wu_bf, wd_bf):
    b = pl.program_id(0)
    used = b < nu_ref[0]

    @pl.when(used)
    def _():
        prev = be_ref[jnp.maximum(b - 1, 0)]

        @pl.when((b == 0) | (be_ref[b] != prev))
        def _():
            wg_bf[...] = wg_ref[0].astype(BF16)
            wu_bf[...] = wu_ref[0].astype(BF16)
            wd_bf[...] = wd_ref[0].astype(BF16)

        x = xs_ref[...].astype(BF16)
        g = jnp.dot(x, wg_bf[...], preferred_element_type=F32)
        u = jnp.dot(x, wu_bf[...], preferred_element_type=F32)
        h = (_silu(g) * u).astype(BF16)
        ys_ref[...] = jnp.dot(h, wd_bf[...], preferred_element_type=F32)

    @pl.when(jnp.logical_not(used))
    def _():
        ys_ref[...] = jnp.zeros_like(ys_ref)


def _experts_call(blk_e, n_used, xs, wg, wu, wd):
    n_rows = xs.shape[0]
    nb = n_rows // MOE_ROWS
    grid_spec = pltpu.PrefetchScalarGridSpec(
        num_scalar_prefetch=2,
        grid=(nb,),
        in_specs=[pl.BlockSpec((MOE_ROWS, D_MODEL), lambda b, be, nu: (jnp.minimum(b, nu[0] - 1), 0)),
                  pl.BlockSpec((1, D_MODEL, D_EXPERT), lambda b, be, nu: (be[b], 0, 0)),
                  pl.BlockSpec((1, D_MODEL, D_EXPERT), lambda b, be, nu: (be[b], 0, 0)),
                  pl.BlockSpec((1, D_EXPERT, D_MODEL), lambda b, be, nu: (be[b], 0, 0))],
        out_specs=pl.BlockSpec((MOE_ROWS, D_MODEL), lambda b, be, nu: (b, 0)),
        scratch_shapes=[pltpu.VMEM((D_MODEL, D_EXPERT), BF16), pltpu.VMEM((D_MODEL, D_EXPERT), BF16),
                        pltpu.VMEM((D_EXPERT, D_MODEL), BF16)],
    )
    return pl.pallas_call(
        _experts_kernel,
        out_shape=jax.ShapeDtypeStruct((n_rows, D_MODEL), F32),
        grid_spec=grid_spec,
        compiler_params=_cparams(("arbitrary",)),
        name="experts",
    )(blk_e, n_used, xs, wg, wu, wd)


def _combine_kernel(off_ref, ids_ref, rank_ref, x1_ref, gate_ref, mod_ref, fg_ref, ys_ref, y_ref, buf, sem):
    def start(t, c):
        for k in range(2):
            d = _row_dest(off_ref, ids_ref, rank_ref, 2 * t + k)
            pltpu.make_async_copy(ys_ref.at[pl.ds(d, 1)], buf.at[k, pl.ds(t, 1)], sem).start()
        return c

    lax.fori_loop(0, TOK_TILE, start, 0, unroll=8)
    for k in range(2):
        pltpu.make_async_copy(ys_ref.at[pl.ds(0, TOK_TILE)], buf.at[k], sem).wait()

    gate = gate_ref[...]
    lane = _iota(gate.shape, 1)
    w0 = jnp.sum(jnp.where(lane == 0, gate, 0.0), axis=-1, keepdims=True)
    w1 = jnp.sum(jnp.where(lane == 1, gate, 0.0), axis=-1, keepdims=True)
    g2 = mod_ref[0][:, 5 * D_MODEL:6 * D_MODEL]
    y = x1_ref[...] + g2 * (w0 * buf[0] + w1 * buf[1])
    ms = jnp.mean(y * y, axis=-1, keepdims=True)
    y_ref[...] = y * lax.rsqrt(ms + NORM_EPS) * fg_ref[...]


def _combine_call(off, ids3, rank3, x1, rgate, mod3, mod_row, fg, ys):
    nt = x1.shape[0]
    smem_blk = pl.BlockSpec((1, 1, 2 * TOK_TILE), lambda i: (i, 0, 0), memory_space=pltpu.SMEM)
    return pl.pallas_call(
        _combine_kernel,
        out_shape=jax.ShapeDtypeStruct((nt, D_MODEL), F32),
        grid=(nt // TOK_TILE,),
        in_specs=[pl.BlockSpec(memory_space=pltpu.SMEM), smem_blk, smem_blk,
                  pl.BlockSpec((TOK_TILE, D_MODEL), lambda i: (i, 0)),
                  pl.BlockSpec((TOK_TILE, LANES), lambda i: (i, 0)),
                  pl.BlockSpec((1, 1, 6 * D_MODEL), lambda i: (mod_row(i), 0, 0)),
                  pl.BlockSpec((1, D_MODEL), lambda i: (0, 0)),
                  pl.BlockSpec(memory_space=pl.ANY)],
        out_specs=pl.BlockSpec((TOK_TILE, D_MODEL), lambda i: (i, 0)),
        scratch_shapes=[pltpu.VMEM((2, TOK_TILE, D_MODEL), F32), pltpu.SemaphoreType.DMA(())],
        compiler_params=_cparams(("arbitrary",)),
        name="combine",
    )(off, ids3, rank3, x1, rgate, mod3, fg, ys)


def kernel(x_prompt, x_sample, state_rwkv, state_gdn, c, c_ctx, ada_w, ada_b, norm1_g, norm2_g, w_in, w_out,
           rwkv_w0, rwkv_w2, rwkv_a0, rwkv_a2, rwkv_g2, rwkv_k_k, rwkv_k_a, rwkv_r_k, rwkv_lnx_g, rwkv_lnx_b,
           gdn_conv, gdn_a_log, gdn_dt_bias, gdn_norm_g, router_group_w, router_group_b, router_expert_w,
           router_expert_b, expert_gate, expert_up, expert_down, final_norm_g):
    assert ada_w.shape[0] == 1, "one layer"
    bp, tp, _ = x_prompt.shape
    bs, ts, _ = x_sample.shape
    lp = dict(rwkv_w0=rwkv_w0[0], rwkv_w2=rwkv_w2[0], rwkv_a0=rwkv_a0[0], rwkv_a2=rwkv_a2[0], rwkv_g2=rwkv_g2[0],
              rwkv_k_k=rwkv_k_k[0], rwkv_k_a=rwkv_k_a[0], rwkv_r_k=rwkv_r_k[0], rwkv_lnx_g=rwkv_lnx_g[0],
              rwkv_lnx_b=rwkv_lnx_b[0], gdn_conv=gdn_conv[0], gdn_a_log=gdn_a_log[0],
              gdn_dt_bias=gdn_dt_bias[0], gdn_norm_g=gdn_norm_g[0])
    rw_w = _rwkv_weights(lp)
    gd_w = _gdn_weights(lp)

    n_cond = 1 + bs
    cond = jnp.concatenate([c_ctx[None, :], c, jnp.zeros((16 - n_cond, D_MODEL), F32)], axis=0)
    mod = _mod_call(cond, ada_w[0], ada_b)
    mod3 = mod.reshape(16, 1, 6 * D_MODEL)

    in_cols = w_in.shape[2]
    w_in_bf = jnp.pad(w_in[0], ((0, 0), (0, RW + GW_PAD - in_cols))).astype(BF16)
    w_out_bf = w_out[0].astype(BF16)
    n1g = norm1_g.reshape(1, D_MODEL)
    n2g = norm2_g.reshape(1, D_MODEL)
    fg = final_norm_g.reshape(1, D_MODEL)
    rw = jnp.zeros((D_MODEL, LANES), F32).at[:, :N_GROUPS].set(router_group_w[0])
    rw = rw.at[:, N_GROUPS:N_GROUPS + N_EXPERTS].set(router_expert_w[0])
    rb = jnp.zeros((1, LANES), F32).at[0, :N_GROUPS].set(router_group_b[0])
    rb = rb.at[0, N_GROUPS:N_GROUPS + N_EXPERTS].set(router_expert_b[0])

    tiles_per_sample = ts // TOK_TILE
    passes = [
        dict(x=x_prompt.reshape(bp * tp, D_MODEL), batch=bp, seq=tp, row_len=tp,
             mod_row=lambda i: 0,
             s_r=jnp.zeros((bp, 2, R_HEAD, R_WIDTH), F32), s_g=jnp.zeros((bp, 2, G_HEAD, G_WIDTH), F32)),
        dict(x=x_sample.reshape(bs * ts, D_MODEL), batch=bs, seq=ts, row_len=GRID_W,
             mod_row=lambda i: 1 + i // tiles_per_sample,
             s_r=jnp.transpose(state_rwkv[:, 0], (0, 1, 3, 2, 4)).reshape(bs, 2, R_HEAD, R_WIDTH),
             s_g=jnp.transpose(state_gdn[:, 0], (0, 1, 3, 2, 4)).reshape(bs, 2, G_HEAD, G_WIDTH)),
    ]

    for p in passes:
        proj_r, proj_g = _inproj_call(p["x"], mod3, p["mod_row"], n1g, w_in_bf)
        o_r, p["st_r"] = _rwkv_call(proj_r, p["s_r"], rw_w, batch=p["batch"], seq=p["seq"], hp=False)
        o_g, p["st_g"] = _gdn_call(proj_g, p["s_g"], gd_w, batch=p["batch"], seq=p["seq"],
                                   row_len=p["row_len"], hp=False)
        p["x1"], p["h2"], p["rid"], p["rgate"] = _outproj_call(
            p["x"], o_r, o_g, mod3, p["mod_row"], w_out_bf, n2g, rw, rb)

    rid = jnp.concatenate([p["rid"] for p in passes], axis=0)
    rank, cnt = _plan_call(rid)
    counts = cnt[0, :N_EXPERTS]
    padded = (counts + MOE_ROWS - 1) // MOE_ROWS * MOE_ROWS
    ends = jnp.cumsum(padded)
    off = (ends - padded).astype(jnp.int32)
    nt_all = rid.shape[0]
    n_blocks = (2 * nt_all) // MOE_ROWS + N_EXPERTS
    n_rows = n_blocks * MOE_ROWS
    n_used = (ends[-1] // MOE_ROWS).astype(jnp.int32)
    blk_start = jnp.minimum(jnp.arange(n_blocks, dtype=jnp.int32), n_used - 1) * MOE_ROWS
    blk_e = jnp.minimum(jnp.sum((blk_start[:, None] >= ends[None, :]).astype(jnp.int32), axis=1), N_EXPERTS - 1)

    ids3 = rid[:, :2].reshape(nt_all // TOK_TILE, 1, 2 * TOK_TILE)
    rank3 = rank[:, :2].reshape(nt_all // TOK_TILE, 1, 2 * TOK_TILE)
    xs = None
    t0 = 0
    for p in passes:
        nt = p["x"].shape[0] // TOK_TILE
        p["ids3"], p["rank3"] = ids3[t0:t0 + nt], rank3[t0:t0 + nt]
        xs = _dispatch_call(off, p["ids3"], p["rank3"], p["h2"], xs, n_rows)
        t0 += nt
    ys = _experts_call(blk_e, n_used.reshape(1), xs, expert_gate[0], expert_up[0], expert_down[0])
    outs = [_combine_call(off, p["ids3"], p["rank3"], p["x1"], p["rgate"], mod3, p["mod_row"], fg, ys)
            for p in passes]

    y_prompt = outs[0].reshape(bp, tp, D_MODEL)
    y_sample = outs[1].reshape(bs, ts, D_MODEL)
    st_r = passes[0]["st_r"].reshape(bp, 2, R_HEAD, R_HEADS, R_HEAD)
    new_state_rwkv = jnp.transpose(st_r, (0, 1, 3, 2, 4))[:, None]
    st_g = passes[0]["st_g"].reshape(bp, 2, G_HEAD, G_HEADS, G_HEAD)
    new_state_gdn = jnp.transpose(st_g, (0, 1, 3, 2, 4))[:, None]
    return (y_prompt, y_sample, new_state_rwkv, new_state_gdn)
```

```python
import functools
import math

import jax
import jax.numpy as jnp
from jax import lax
from jax.experimental import pallas as pl
from jax.experimental.pallas import tpu as pltpu

F32 = jnp.float32
BF16 = jnp.bfloat16

D_MODEL = 1024
R_HEADS, R_HEAD = 8, 64
R_WIDTH = R_HEADS * R_HEAD
G_HEADS, G_HEAD = 4, 128
G_WIDTH = G_HEADS * G_HEAD
LNX_EPS = 64e-5
NORM_EPS = 1e-6
N_GROUPS, EXPERTS_PER_GROUP = 4, 8
N_EXPERTS = N_GROUPS * EXPERTS_PER_GROUP
D_EXPERT = 512
GRID_W = 64

CHUNK = 64
TOK_TILE = 256
MOE_ROWS = 256
SEQS_PER_STEP = 2
GDN_SEQS_PER_STEP_SHORT = 4
LANES = 128
RW = 3 * R_WIDTH + 256
GW_PAD = 4 * G_WIDTH + LANES
VMEM_LIMIT = 56 * 1024 * 1024


def _cparams(sem):
    return pltpu.CompilerParams(dimension_semantics=sem, vmem_limit_bytes=VMEM_LIMIT)


def _sigmoid(x):
    return 1.0 / (1.0 + jnp.exp(-x))


def _silu(x):
    return x * _sigmoid(x)


def _softplus(x):
    return jnp.maximum(x, 0.0) + jnp.log(1.0 + jnp.exp(-jnp.abs(x)))


def _dg(a, b, ca, cb):
    return lax.dot_general(a, b, (((ca,), (cb,)), ((), ())), preferred_element_type=F32)


def _split2(x):
    hi = x.astype(BF16)
    lo = (x - hi.astype(F32)).astype(BF16)
    return hi, lo


def _split3(x):
    h1 = x.astype(BF16)
    r1 = x - h1.astype(F32)
    h2 = r1.astype(BF16)
    h3 = (r1 - h2.astype(F32)).astype(BF16)
    return h1, h2, h3


def _mm(a, b, ca=1, cb=0, hp=False):
    if not hp:
        return _dg(a.astype(BF16), b.astype(BF16), ca, cb)
    ah, al = _split2(a)
    bh, bl = _split2(b)
    return _dg(ah, bh, ca, cb) + (_dg(ah, bl, ca, cb) + _dg(al, bh, ca, cb))


def _mm_exact_lhs(m_bf, x, ca=1, cb=0):
    h1, h2, h3 = _split3(x)
    return _dg(m_bf, h1, ca, cb) + (_dg(m_bf, h2, ca, cb) + _dg(m_bf, h3, ca, cb))


def _mm_exact_rhs(x, m_bf):
    h1, h2, h3 = _split3(x)
    return _dg(h1, m_bf, 1, 0) + (_dg(h2, m_bf, 1, 0) + _dg(h3, m_bf, 1, 0))


def _group_sum(x, ones_bf):
    hi, lo = _split2(x)
    return _dg(hi, ones_bf, 1, 0) + _dg(lo, ones_bf, 1, 0)


def _iota(shape, axis):
    return lax.broadcasted_iota(jnp.int32, shape, axis)


def _tri_masks(n, rev):
    t = _iota((n, n), 0)
    s = _iota((n, n), 1)
    if rev:
        return t < s, t <= s
    return t > s, t >= s


PACK = 4


def _level_masks(n):
    t = _iota((n, PACK * n), 0)
    u = _iota((n, PACK * n), 1) % n
    levels = []
    s = 1
    while s < n:
        levels.append((t // (2 * s) == u // (2 * s)) & (t // s != u // s))
        s *= 2
    diag = _iota((PACK * n, PACK * n), 0) // n == _iota((PACK * n, PACK * n), 1) // n
    return levels, diag


def _block_diag(x, diag):
    return jnp.where(diag, jnp.concatenate([x] * PACK, axis=0), 0.0).astype(BF16)


def _unit_tri_inverse(ms, masks):
    levels, diag = masks
    n = ms[0].shape[0]
    eye = (_iota((n, PACK * n), 0) == _iota((n, PACK * n), 1) % n).astype(F32)
    xs = [eye - jnp.where(levels[0], m, 0.0) for m in ms]
    for mask in levels[1:]:
        zs = [_dg(jnp.where(mask, m, 0.0).astype(BF16), _block_diag(x, diag), 1, 0) for m, x in zip(ms, xs)]
        xs = [x - _dg(x.astype(BF16), _block_diag(z, diag), 1, 0) for x, z in zip(xs, zs)]
    return xs


def _mod_kernel(c_ref, w_ref, b_ref, o_ref):
    s = _silu(c_ref[...])
    o_ref[...] = _mm(s, w_ref[...], hp=True) + b_ref[...]


def _mod_call(cond, ada_w, ada_b):
    rows = cond.shape[0]
    n = ada_w.shape[1]
    bn = 1024
    return pl.pallas_call(
        _mod_kernel,
        out_shape=jax.ShapeDtypeStruct((rows, n), F32),
        grid=(n // bn,),
        in_specs=[pl.BlockSpec((rows, D_MODEL), lambda j: (0, 0)),
                  pl.BlockSpec((D_MODEL, bn), lambda j: (0, j)),
                  pl.BlockSpec((1, bn), lambda j: (0, j))],
        out_specs=pl.BlockSpec((rows, bn), lambda j: (0, j)),
        compiler_params=_cparams(("arbitrary",)),
        name="mod",
    )(cond, ada_w, ada_b)


def _inproj_kernel(x_ref, mod_ref, g_ref, w_ref, or_ref, og_ref):
    x = x_ref[...]
    ms = jnp.mean(x * x, axis=-1, keepdims=True)
    y = x * lax.rsqrt(ms + NORM_EPS) * g_ref[...]
    m = mod_ref[0]
    h = y * (1.0 + m[:, D_MODEL:2 * D_MODEL]) + m[:, :D_MODEL]
    p = jnp.dot(h.astype(BF16), w_ref[...], preferred_element_type=F32)
    or_ref[...] = p[:, :RW].astype(BF16)
    og_ref[...] = p[:, RW:].astype(BF16)


def _inproj_call(x2, mod3, mod_row, norm_g, w_in_bf):
    nt = x2.shape[0]
    ncol = w_in_bf.shape[1]
    return pl.pallas_call(
        _inproj_kernel,
        out_shape=(jax.ShapeDtypeStruct((nt, RW), BF16),
                   jax.ShapeDtypeStruct((nt, GW_PAD), BF16)),
        grid=(nt // TOK_TILE,),
        in_specs=[pl.BlockSpec((TOK_TILE, D_MODEL), lambda i: (i, 0)),
                  pl.BlockSpec((1, 1, 6 * D_MODEL), lambda i: (mod_row(i), 0, 0)),
                  pl.BlockSpec((1, D_MODEL), lambda i: (0, 0)),
                  pl.BlockSpec((D_MODEL, ncol), lambda i: (0, 0))],
        out_specs=(pl.BlockSpec((TOK_TILE, RW), lambda i: (i, 0)),
                   pl.BlockSpec((TOK_TILE, GW_PAD), lambda i: (i, 0))),
        compiler_params=_cparams(("arbitrary",)),
        name="inproj",
    )(x2, mod3, norm_g, w_in_bf)


def _head_blocks(width, head):
    return (_iota((width, width), 0) // head == _iota((width, width), 1) // head).astype(BF16)


def _rwkv_kernel(p_ref, *refs, seq, nrow, hp, has_s0):
    s0_ref = refs[0] if has_s0 else None
    (w0_ref, wd_ref, a0_ref, wa_ref, g2_ref, kk_ref, ka_ref, rk_ref, lng_ref, lnb_ref,
     o_ref, st_ref, y_scr) = refs[1:] if has_s0 else refs
    C = CHUNK
    nc = seq // C
    blk = _head_blocks(R_WIDTH, R_HEAD)
    k_k = kk_ref[...]
    k_a = ka_ref[...]
    r_k = rk_ref[...]
    exp_m05 = math.exp(-0.5)
    inv_masks = _level_masks(C)

    def lora_lhs(la):
        return jnp.where(_iota(la.shape, 1) < 64, jnp.tanh(la), la)

    def cols(n, rows, lo, hi):
        return p_ref[n, rows, lo:hi].astype(F32)

    t2 = _iota((C, 2 * C), 0)
    s2 = _iota((C, 2 * C), 1) % C
    dirs = []
    for d, rev in ((0, False), (1, True)):
        strict, incl = _tri_masks(C, rev)
        dirs.append(dict(d=d, rev=rev, strict=strict, tri=incl.astype(BF16),
                         incl2=(t2 <= s2) if rev else (t2 >= s2)))
    hd = range(R_HEADS)
    sls = [slice(h * R_HEAD, (h + 1) * R_HEAD) for h in hd]

    def prep(n, dr, c):
        d, rev, tri = dr["d"], dr["rev"], dr["tri"]
        rows = pl.ds(pl.multiple_of(c * C, C), C)
        rr = cols(n, rows, 0, R_WIDTH)
        kx = cols(n, rows, R_WIDTH, 2 * R_WIDTH)
        vv = cols(n, rows, 2 * R_WIDTH, 3 * R_WIDTH)
        lhs = lora_lhs(cols(n, rows, 3 * R_WIDTH, 3 * R_WIDTH + LANES))
        logw = -exp_m05 * _sigmoid(w0_ref[d] + _mm(lhs, wd_ref[d]))
        a = _sigmoid(a0_ref[d] + _mm(lhs, wa_ref[d]))
        kkr = kx * k_k
        kk = kkr * lax.rsqrt(_group_sum(kkr * kkr, blk) + 1e-6)
        kd = kx * (1.0 + (a - 1.0) * k_a)
        bvec = kk * a
        gi = _mm_exact_lhs(tri, logw)
        gtot = gi[0:1] if rev else gi[C - 1:C]
        en = jnp.exp(-gi)
        ee = jnp.exp(gtot - gi)
        return dict(rows=rows, vv=vv, rt=rr * jnp.exp(gi), at=-kk * jnp.exp(gi - logw), bt=bvec * en, kt=kd * en,
                    bh=bvec * ee, kh=kd * ee, gc=jnp.exp(gtot))

    seqs = range(nrow)
    items = [(n, d, h) for n in seqs for d in (0, 1) for h in hd]
    packs = [(n, d, g) for n in seqs for d in (0, 1) for g in range(R_HEADS // PACK)]

    def step(i, carry):
        ops = {(n, d): prep(n, dirs[d], (nc - 1 - i) if d else i) for n in seqs for d in (0, 1)}
        hcol = lambda name, n, d, h: ops[n, d][name][:, sls[h]]
        ars = {(n, d, h): jnp.concatenate([hcol("at", n, d, h), hcol("rt", n, d, h)], axis=0) for n, d, h in items}
        pms = {(n, d, h): _mm(ars[n, d, h], jnp.concatenate([hcol("bt", n, d, h), hcol("kt", n, d, h)], axis=0),
                              1, 1, hp=hp) for n, d, h in items}
        shs = {(n, d, h): carry[2 * n + d][:, sls[h]] for n, d, h in items}
        x0s = {k: _mm(ars[k], shs[k], 1, 1, hp=hp) for k in items}
        vhs = {(n, d, h): hcol("vv", n, d, h) for n, d, h in items}
        akv = {(n, d, h): _mm(jnp.where(dirs[d]["strict"], pms[n, d, h][:C, C:], 0.0), vhs[n, d, h], hp=hp)
               for n, d, h in items}
        packed = [jnp.concatenate([jnp.where(dirs[d]["strict"], -pms[n, d, PACK * g + j][:C, :C], 0.0)
                                   for j in range(PACK)], axis=1) for n, d, g in packs]
        tcat = dict(zip(packs, _unit_tri_inverse(packed, inv_masks)))
        tms = {(n, d, h): tcat[n, d, h // PACK][:, (h % PACK) * C:(h % PACK + 1) * C] for n, d, h in items}
        us = {k: _mm(tms[k], x0s[k][:C] + akv[k], hp=hp) for k in items}
        uvs = {k: jnp.concatenate([us[k], vhs[k]], axis=0) for k in items}
        ys = {(n, d, h): x0s[n, d, h][C:] + _mm(jnp.where(dirs[d]["incl2"], pms[n, d, h][C:], 0.0), uvs[n, d, h])
              for n, d, h in items}
        snew = {(n, d, h): shs[n, d, h] * hcol("gc", n, d, h)
                + _mm(uvs[n, d, h], jnp.concatenate([hcol("bh", n, d, h), hcol("kh", n, d, h)], axis=0),
                      0, 0, hp=hp) for n, d, h in items}
        for n in seqs:
            for d in (0, 1):
                y_scr[n, d, ops[n, d]["rows"], :] = jnp.concatenate([ys[n, d, h] for h in hd], axis=1)
        return tuple(jnp.concatenate([snew[n, d, h] for h in hd], axis=1) for n in seqs for d in (0, 1))

    def initial(n, d):
        if not has_s0:
            return jnp.zeros((R_HEAD, R_WIDTH), F32)
        return jnp.concatenate([s0_ref[n, d, h] for h in hd], axis=1)

    s_fin = lax.fori_loop(0, nc, step, tuple(initial(n, d) for n in seqs for d in (0, 1)))
    for n in seqs:
        for d in (0, 1):
            for h in hd:
                st_ref[n, d, h] = s_fin[2 * n + d][:, sls[h]]

    g2 = g2_ref[...]
    lng = lng_ref[...]
    lnb = lnb_ref[...]
    inv_n = 1.0 / R_HEAD
    fb = min(seq, 256)

    def finish(j, carry):
        rows = pl.ds(pl.multiple_of(j * fb, fb), fb)
        for n in seqs:
            y = y_scr[n, 0, rows, :] + y_scr[n, 1, rows, :]
            mu = _group_sum(y, blk) * inv_n
            yc = y - mu
            var = _group_sum(yc * yc, blk) * inv_n
            yn = yc * lax.rsqrt(var + LNX_EPS) * lng + lnb
            lhs = lora_lhs(cols(n, rows, 3 * R_WIDTH, 3 * R_WIDTH + LANES))
            a_sum = _sigmoid(a0_ref[0] + _mm(lhs, wa_ref[0])) + _sigmoid(a0_ref[1] + _mm(lhs, wa_ref[1]))
            rr = cols(n, rows, 0, R_WIDTH)
            kx = cols(n, rows, R_WIDTH, 2 * R_WIDTH)
            kd_sum = kx * (2.0 + (a_sum - 2.0) * k_a)
            bonus = _group_sum(rr * kd_sum * r_k, blk) * cols(n, rows, 2 * R_WIDTH, 3 * R_WIDTH)
            gl = cols(n, rows, 3 * R_WIDTH + LANES, 3 * R_WIDTH + 2 * LANES)
            gate = _mm(_sigmoid(gl), g2)
            o_ref[n, rows, :] = (yn + bonus) * gate
        return carry

    lax.fori_loop(0, seq // fb, finish, 0)


def _rwkv_call(proj_r, s0, w, *, batch, seq, nrow, hp):
    p3 = proj_r.reshape(batch, seq, RW)
    kern = functools.partial(_rwkv_kernel, seq=seq, nrow=nrow, hp=hp, has_s0=s0 is not None)
    full = lambda a: pl.BlockSpec(a.shape, lambda b: (0,) * a.ndim)
    ws = [w["w0"], w["wd"], w["a0"], w["wa"], w["g2"], w["k_k"], w["k_a"], w["r_k"], w["lnx_g"], w["lnx_b"]]
    state_spec = pl.BlockSpec((nrow, 2, R_HEADS, R_HEAD, R_HEAD), lambda b: (b, 0, 0, 0, 0))
    states = [] if s0 is None else [s0]
    o, st = pl.pallas_call(
        kern,
        out_shape=(jax.ShapeDtypeStruct((batch, seq, R_WIDTH), F32),
                   jax.ShapeDtypeStruct((batch, 2, R_HEADS, R_HEAD, R_HEAD), F32)),
        grid=(batch // nrow,),
        in_specs=[pl.BlockSpec((nrow, seq, RW), lambda b: (b, 0, 0))] + [state_spec] * len(states)
        + [full(a) for a in ws],
        out_specs=(pl.BlockSpec((nrow, seq, R_WIDTH), lambda b: (b, 0, 0)), state_spec),
        scratch_shapes=[pltpu.VMEM((nrow, 2, seq, R_WIDTH), F32)],
        compiler_params=_cparams(("arbitrary",)),
        name="rwkv_hp" if hp else "rwkv",
    )(p3, *states, *ws)
    return o.reshape(batch * seq, R_WIDTH), st


def _rwkv_weights(lp):
    zeros = jnp.zeros((2, 64, R_WIDTH), F32)
    row = lambda a: jnp.asarray(a, F32).reshape(1, R_WIDTH)
    return {
        "w0": jnp.asarray(lp["rwkv_w0"], F32).reshape(2, 1, R_WIDTH),
        "wd": jnp.concatenate([jnp.asarray(lp["rwkv_w2"], F32), zeros], axis=1),
        "a0": jnp.asarray(lp["rwkv_a0"], F32).reshape(2, 1, R_WIDTH),
        "wa": jnp.concatenate([zeros, jnp.asarray(lp["rwkv_a2"], F32)], axis=1),
        "g2": jnp.asarray(lp["rwkv_g2"], F32),
        "k_k": row(lp["rwkv_k_k"]), "k_a": row(lp["rwkv_k_a"]), "r_k": row(lp["rwkv_r_k"]),
        "lnx_g": row(lp["rwkv_lnx_g"]), "lnx_b": row(lp["rwkv_lnx_b"]),
    }


def _gdn_kernel(p_ref, *refs, seq, nrow, row_len, hp, has_s0):
    s0_ref = refs[0] if has_s0 else None
    (cw_ref, alog_ref, dtb_ref, ng_ref, o_ref, st_ref, q_scr, k_scr, v_scr, o_scr) = refs[1:] if has_s0 else refs
    C = CHUNK
    nc = seq // C
    fb = min(seq, 256)
    W = G_WIDTH
    q_scale = G_HEAD ** -0.5
    seqs = range(nrow)
    hd = range(G_HEADS)
    hss = [slice(h * G_HEAD, (h + 1) * G_HEAD) for h in hd]

    cw0 = cw_ref[0:1, :]
    cw1 = cw_ref[1:2, :]
    cw2 = cw_ref[2:3, :]
    tpos = _iota((fb, 1), 0) % row_len
    is_first = tpos == 0
    is_last = tpos == row_len - 1

    def prep(j, carry):
        rows = pl.ds(pl.multiple_of(j * fb, fb), fb)
        for n in seqs:
            x = p_ref[n, rows, 0:3 * W].astype(F32)
            xm = jnp.where(is_first, 0.0, pltpu.roll(x, 1, 0))
            xp = jnp.where(is_last, 0.0, pltpu.roll(x, fb - 1, 0))
            y = _silu(cw0 * xm + cw1 * x + cw2 * xp)
            for h in hd:
                qh = y[:, hss[h]]
                kh = y[:, W + h * G_HEAD:W + (h + 1) * G_HEAD]
                q_scr[n, rows, hss[h]] = qh * lax.rsqrt(jnp.sum(qh * qh, axis=-1, keepdims=True) + 1e-6) * q_scale
                k_scr[n, rows, hss[h]] = kh * lax.rsqrt(jnp.sum(kh * kh, axis=-1, keepdims=True) + 1e-6)
            v_scr[n, rows, :] = y[:, 2 * W:3 * W]
        return carry

    lax.fori_loop(0, seq // fb, prep, 0)

    assert G_HEADS == PACK
    inv_masks = _level_masks(C)
    alog = alog_ref[...]
    dtb = dtb_ref[...]
    sel_row = _iota((LANES, W), 0)
    sel_head = _iota((LANES, W), 1) // G_HEAD
    pk_row = _iota((LANES, PACK * C), 0)
    pk_head = _iota((LANES, PACK * C), 1) // C
    pk_t = _iota((C, PACK * C), 0)
    pk_s = _iota((C, PACK * C), 1) % C

    dirs = []
    for d, rev in ((0, False), (1, True)):
        dirs.append(dict(
            d=d, rev=rev, tri=_tri_masks(C, rev)[1].astype(BF16),
            strict=(pk_t < pk_s) if rev else (pk_t > pk_s),
            incl=(pk_t <= pk_s) if rev else (pk_t >= pk_s),
            upto=(pk_t >= pk_s) if rev else (pk_t <= pk_s),
            e_beta=(sel_row == 4 * d + sel_head).astype(BF16),
            e_g=(sel_row == 8 + 4 * d + sel_head).astype(BF16),
            e_g_pk=(pk_row == 8 + 4 * d + pk_head).astype(BF16)))
    items = [(n, d, h) for n in seqs for d in (0, 1) for h in hd]
    packs = [(n, d) for n in seqs for d in (0, 1)]

    def prep_dir(n, dr, c):
        rows = pl.ds(pl.multiple_of(c * C, C), C)
        ba = p_ref[n, rows, 4 * W:4 * W + LANES].astype(F32)
        gval = -jnp.exp(alog) * _softplus(ba + dtb)
        beta_b = _mm_exact_rhs(_sigmoid(ba), dr["e_beta"])
        gcum = _mm_exact_lhs(dr["tri"], _mm_exact_rhs(gval, dr["e_g"]))
        g_pk = _mm_exact_rhs(gval, dr["e_g_pk"])
        g_col = _mm_exact_lhs(dr["tri"], g_pk)
        g_row = jnp.sum(jnp.where(dr["upto"], g_pk, 0.0), axis=0, keepdims=True)
        return dict(rows=rows, beta=beta_b, gcum=gcum,
                    decay=jnp.where(dr["incl"], jnp.exp(jnp.where(dr["incl"], g_col - g_row, 0.0)), 0.0),
                    q=q_scr[n, rows, :], k=k_scr[n, rows, :], v=v_scr[n, rows, :])

    def step(i, carry):
        ops = {(n, d): prep_dir(n, dirs[d], (nc - 1 - i) if d else i) for n, d in packs}
        hcol = lambda name, n, d, h: ops[n, d][name][:, hss[h]]
        qhs = {k: hcol("q", *k) for k in items}
        khs = {k: hcol("k", *k) for k in items}
        gcs = {k: hcol("gcum", *k) for k in items}
        bbs = {k: hcol("beta", *k) for k in items}
        kbs = {k: khs[k] * bbs[k] for k in items}
        kks = {k: _mm(jnp.concatenate([kbs[k], qhs[k]], axis=0), khs[k], 1, 1, hp=hp) for k in items}
        kka = {(n, d): jnp.concatenate([kks[n, d, h][:C] for h in hd], axis=1) for n, d in packs}
        tcat = dict(zip(packs, _unit_tri_inverse(
            [jnp.where(dirs[d]["strict"], kka[n, d] * ops[n, d]["decay"], 0.0) for n, d in packs], inv_masks)))
        egs = {k: jnp.exp(gcs[k]) for k in items}
        uws = {(n, d, h): _mm(tcat[n, d][:, h * C:(h + 1) * C],
                              jnp.concatenate([hcol("v", n, d, h) * bbs[n, d, h], kbs[n, d, h] * egs[n, d, h]], axis=1),
                              hp=hp) for n, d, h in items}
        shs = {(n, d, h): carry[2 * n + d][:, hss[h]] for n, d, h in items}
        wss = {k: _mm(jnp.concatenate([uws[k][:, G_HEAD:], qhs[k] * egs[k]], axis=0), shs[k], hp=hp) for k in items}
        vns = {k: uws[k][:, :G_HEAD] - wss[k][:C] for k in items}
        outs = {(n, d, h): wss[n, d, h][C:]
                + _mm(kks[n, d, h][C:] * ops[n, d]["decay"][:, h * C:(h + 1) * C], vns[n, d, h]) for n, d, h in items}
        snew = {}
        for n, d, h in items:
            glast = gcs[n, d, h][0:1] if dirs[d]["rev"] else gcs[n, d, h][C - 1:C]
            k_dec = khs[n, d, h] * jnp.exp(glast - gcs[n, d, h])
            snew[n, d, h] = shs[n, d, h] * jnp.exp(glast) + _mm(k_dec, vns[n, d, h], 0, 0, hp=hp)
        for n, d in packs:
            o_scr[n, d, ops[n, d]["rows"], :] = jnp.concatenate([outs[n, d, h] for h in hd], axis=1)
        return tuple(jnp.concatenate([snew[n, d, h] for h in hd], axis=1) for n, d in packs)

    def initial(n, d):
        if not has_s0:
            return jnp.zeros((G_HEAD, W), F32)
        return jnp.concatenate([s0_ref[n, d, h] for h in hd], axis=1)

    s_fin = lax.fori_loop(0, nc, step, tuple(initial(n, d) for n, d in packs))
    for j, (n, d) in enumerate(packs):
        for h in hd:
            st_ref[n, d, h] = s_fin[j][:, hss[h]]

    ng = ng_ref[...]

    def finish(j, carry):
        rows = pl.ds(pl.multiple_of(j * fb, fb), fb)
        for n in seqs:
            o = o_scr[n, 0, rows, :] + o_scr[n, 1, rows, :]
            z = p_ref[n, rows, 3 * W:4 * W].astype(F32)
            for h in hd:
                oh = o[:, hss[h]]
                ms = jnp.mean(oh * oh, axis=-1, keepdims=True)
                o_ref[n, rows, hss[h]] = oh * lax.rsqrt(ms + NORM_EPS) * ng * _silu(z[:, hss[h]])
        return carry

    lax.fori_loop(0, seq // fb, finish, 0)


def _gdn_call(proj_g, s0, w, *, batch, seq, nrow, row_len, hp):
    p3 = proj_g.reshape(batch, seq, GW_PAD)
    kern = functools.partial(_gdn_kernel, seq=seq, nrow=nrow, row_len=row_len, hp=hp, has_s0=s0 is not None)
    full = lambda a: pl.BlockSpec(a.shape, lambda b: (0,) * a.ndim)
    ws = [w["conv"], w["alog"], w["dtb"], w["norm_g"]]
    state_spec = pl.BlockSpec((nrow, 2, G_HEADS, G_HEAD, G_HEAD), lambda b: (b, 0, 0, 0, 0))
    states = [] if s0 is None else [s0]
    o, st = pl.pallas_call(
        kern,
        out_shape=(jax.ShapeDtypeStruct((batch, seq, G_WIDTH), F32),
                   jax.ShapeDtypeStruct((batch, 2, G_HEADS, G_HEAD, G_HEAD), F32)),
        grid=(batch // nrow,),
        in_specs=[pl.BlockSpec((nrow, seq, GW_PAD), lambda b: (b, 0, 0))] + [state_spec] * len(states)
        + [full(a) for a in ws],
        out_specs=(pl.BlockSpec((nrow, seq, G_WIDTH), lambda b: (b, 0, 0)), state_spec),
        scratch_shapes=[pltpu.VMEM((nrow, seq, G_WIDTH), F32)] * 3 + [pltpu.VMEM((nrow, 2, seq, G_WIDTH), F32)],
        compiler_params=_cparams(("arbitrary",)),
        name="gdn_hp" if hp else "gdn",
    )(p3, *states, *ws)
    return o.reshape(batch * seq, G_WIDTH), st


def _gdn_weights(lp):
    lanes = jnp.zeros((1, LANES), F32)
    return {
        "conv": jnp.asarray(lp["gdn_conv"], F32),
        "alog": lanes.at[0, 8:16].set(jnp.asarray(lp["gdn_a_log"], F32).reshape(8)),
        "dtb": lanes.at[0, 8:16].set(jnp.asarray(lp["gdn_dt_bias"], F32).reshape(8)),
        "norm_g": jnp.asarray(lp["gdn_norm_g"], F32).reshape(1, G_HEAD),
    }


def _outproj_kernel(x_ref, or_ref, og_ref, mod_ref, wo_ref, n2_ref, rw_ref, rb_ref,
                    x1_ref, h2_ref, rid_ref, rgate_ref):
    m = mod_ref[0]
    g1 = m[:, 2 * D_MODEL:3 * D_MODEL]
    sh2 = m[:, 3 * D_MODEL:4 * D_MODEL]
    sc2 = m[:, 4 * D_MODEL:5 * D_MODEL]
    mix = (jnp.dot(or_ref[...].astype(BF16), wo_ref[0:R_WIDTH, :], preferred_element_type=F32)
           + jnp.dot(og_ref[...].astype(BF16), wo_ref[R_WIDTH:, :], preferred_element_type=F32))
    x1 = x_ref[...] + g1 * mix
    x1_ref[...] = x1
    ms = jnp.mean(x1 * x1, axis=-1, keepdims=True)
    h2 = x1 * lax.rsqrt(ms + NORM_EPS) * n2_ref[...] * (1.0 + sc2) + sh2
    h2_ref[...] = h2

    logits = _mm(h2, rw_ref[...], hp=True) + rb_ref[...]
    lane = _iota(logits.shape, 1)
    neg = jnp.float32(-1e30)
    big = jnp.int32(1 << 20)

    def first_argmax(v):
        mx = jnp.max(v, axis=-1, keepdims=True)
        idx = jnp.min(jnp.where(v == mx, lane, big), axis=-1, keepdims=True)
        return mx, idx

    lg = jnp.where(lane < N_GROUPS, logits, neg)
    mg, grp = first_argmax(lg)
    p_grp = 1.0 / jnp.sum(jnp.where(lane < N_GROUPS, jnp.exp(lg - mg), 0.0), axis=-1, keepdims=True)
    in_grp = (lane >= N_GROUPS) & (lane < N_GROUPS + N_EXPERTS) & ((lane - N_GROUPS) // EXPERTS_PER_GROUP == grp)
    le = jnp.where(in_grp, logits, neg)
    m1, i1 = first_argmax(le)
    m2, i2 = first_argmax(jnp.where(lane == i1, neg, le))
    e2 = jnp.exp(m2 - m1)
    w1 = p_grp / (1.0 + e2)
    w2 = p_grp * e2 / (1.0 + e2)
    rid_ref[...] = jnp.where(lane == 0, i1 - N_GROUPS, jnp.where(lane == 1, i2 - N_GROUPS, 0))
    rgate_ref[...] = jnp.where(lane == 0, w1, jnp.where(lane == 1, w2, 0.0))


def _outproj_call(x2, o_r, o_g, mod3, mod_row, wo_bf, n2g, rw, rb):
    nt = x2.shape[0]
    tile = lambda w: pl.BlockSpec((TOK_TILE, w), lambda i: (i, 0))
    const = lambda a: pl.BlockSpec(a.shape, lambda i: (0,) * a.ndim)
    return pl.pallas_call(
        _outproj_kernel,
        out_shape=(jax.ShapeDtypeStruct((nt, D_MODEL), F32), jax.ShapeDtypeStruct((nt, D_MODEL), F32),
                   jax.ShapeDtypeStruct((nt, LANES), jnp.int32), jax.ShapeDtypeStruct((nt, LANES), F32)),
        grid=(nt // TOK_TILE,),
        in_specs=[tile(D_MODEL), tile(R_WIDTH), tile(G_WIDTH),
                  pl.BlockSpec((1, 1, 6 * D_MODEL), lambda i: (mod_row(i), 0, 0)),
                  const(wo_bf), const(n2g), const(rw), const(rb)],
        out_specs=(tile(D_MODEL), tile(D_MODEL), tile(LANES), tile(LANES)),
        compiler_params=_cparams(("arbitrary",)),
        name="outproj",
    )(x2, o_r, o_g, mod3, wo_bf, n2g, rw, rb)


def _plan_kernel(rid_ref, rank_ref, cnt_ref, carry):
    i = pl.program_id(0)

    @pl.when(i == 0)
    def _():
        carry[...] = jnp.zeros_like(carry)

    rid = rid_ref[...]
    lane = _iota(rid.shape, 1)
    e0 = jnp.sum(jnp.where(lane == 0, rid, 0), axis=-1, keepdims=True)
    e1 = jnp.sum(jnp.where(lane == 1, rid, 0), axis=-1, keepdims=True)
    oh0 = (lane == e0).astype(F32)
    oh1 = (lane == e1).astype(F32)
    oh = oh0 + oh1
    n = rid.shape[0]
    earlier = (_iota((n, n), 0) > _iota((n, n), 1)).astype(BF16)
    before = jnp.dot(earlier, oh.astype(BF16), preferred_element_type=F32) + carry[0:1, :]
    r0 = jnp.sum(oh0 * before, axis=-1, keepdims=True)
    r1 = jnp.sum(oh1 * before, axis=-1, keepdims=True)
    rank_ref[...] = jnp.where(lane == 0, r0, jnp.where(lane == 1, r1, 0.0)).astype(jnp.int32)
    total = carry[0:1, :] + jnp.sum(oh, axis=0, keepdims=True)
    carry[...] = jnp.broadcast_to(total, carry.shape)
    cnt_ref[...] = jnp.broadcast_to(total, cnt_ref.shape).astype(jnp.int32)


def _plan_call(rid):
    nt = rid.shape[0]
    return pl.pallas_call(
        _plan_kernel,
        out_shape=(jax.ShapeDtypeStruct((nt, LANES), jnp.int32), jax.ShapeDtypeStruct((8, LANES), jnp.int32)),
        grid=(nt // TOK_TILE,),
        in_specs=[pl.BlockSpec((TOK_TILE, LANES), lambda i: (i, 0))],
        out_specs=(pl.BlockSpec((TOK_TILE, LANES), lambda i: (i, 0)), pl.BlockSpec((8, LANES), lambda i: (0, 0))),
        scratch_shapes=[pltpu.VMEM((8, LANES), F32)],
        compiler_params=_cparams(("arbitrary",)),
        name="plan",
    )(rid)


def _row_dest(off_ref, ids_ref, rank_ref, j):
    return off_ref[ids_ref[0, 0, j]] + rank_ref[0, 0, j]


def _dispatch_kernel(off_ref, ids_ref, rank_ref, h2_ref, *rest, fill, n_rows):
    if fill:
        xs_ref, zero_scr, sem = rest
    else:
        _, xs_ref, sem = rest
    i = pl.program_id(0)

    if fill:
        @pl.when(i == 0)
        def _():
            zero_scr[...] = jnp.zeros_like(zero_scr)
            nblk = n_rows // MOE_ROWS

            def fill_copy(b):
                return pltpu.make_async_copy(zero_scr, xs_ref.at[pl.ds(b * MOE_ROWS, MOE_ROWS)], sem)

            def start(b, c):
                fill_copy(b).start()
                return c

            def wait(b, c):
                fill_copy(b).wait()
                return c

            lax.fori_loop(0, nblk, start, 0)
            lax.fori_loop(0, nblk, wait, 0)

    def start(t, c):
        for k in range(2):
            d = _row_dest(off_ref, ids_ref, rank_ref, 2 * t + k)
            pltpu.make_async_copy(h2_ref.at[pl.ds(t, 1)], xs_ref.at[pl.ds(d, 1)], sem).start()
        return c

    lax.fori_loop(0, TOK_TILE, start, 0, unroll=8)
    for _ in range(2):
        pltpu.make_async_copy(h2_ref, xs_ref.at[pl.ds(0, TOK_TILE)], sem).wait()


def _dispatch_call(off, ids3, rank3, h2, xs, n_rows):
    nt = h2.shape[0]
    fill = xs is None
    smem_blk = pl.BlockSpec((1, 1, 2 * TOK_TILE), lambda i: (i, 0, 0), memory_space=pltpu.SMEM)
    kern = functools.partial(_dispatch_kernel, fill=fill, n_rows=n_rows)
    in_specs = [pl.BlockSpec(memory_space=pltpu.SMEM), smem_blk, smem_blk,
                pl.BlockSpec((TOK_TILE, D_MODEL), lambda i: (i, 0))]
    args = [off, ids3, rank3, h2]
    scratch = [pltpu.SemaphoreType.DMA(())]
    aliases = {}
    if fill:
        scratch = [pltpu.VMEM((MOE_ROWS, D_MODEL), F32)] + scratch
    else:
        in_specs.append(pl.BlockSpec(memory_space=pl.ANY))
        args.append(xs)
        aliases = {4: 0}
    return pl.pallas_call(
        kern,
        out_shape=jax.ShapeDtypeStruct((n_rows, D_MODEL), F32),
        grid=(nt // TOK_TILE,),
        in_specs=in_specs,
        out_specs=pl.BlockSpec(memory_space=pl.ANY),
        scratch_shapes=scratch,
        input_output_aliases=aliases,
        compiler_params=_cparams(("arbitrary",)),
        name="dispatch_fill" if fill else "dispatch",
    )(*args)


def _experts_kernel(be_ref, nu_ref, xs_ref, wg_ref, wu_ref, wd_ref, ys_ref, wg_bf, wu_bf, wd_bf):
    b = pl.program_id(0)
    used = b < nu_ref[0]

    @pl.when(used)
    def _():
        prev = be_ref[jnp.maximum(b - 1, 0)]

        @pl.when((b == 0) | (be_ref[b] != prev))
        def _():
            wg_bf[...] = wg_ref[0].astype(BF16)
            wu_bf[...] = wu_ref[0].astype(BF16)
            wd_bf[...] = wd_ref[0].astype(BF16)

        x = xs_ref[...].astype(BF16)
        g = jnp.dot(x, wg_bf[...], preferred_element_type=F32)
        u = jnp.dot(x, wu_bf[...], preferred_element_type=F32)
        h = (_silu(g) * u).astype(BF16)
        ys_ref[...] = jnp.dot(h, wd_bf[...], preferred_element_type=F32)

    @pl.when(jnp.logical_not(used))
    def _():
        ys_ref[...] = jnp.zeros_like(ys_ref)


def _experts_call(blk_e, n_used, xs, wg, wu, wd):
    n_rows = xs.shape[0]
    nb = n_rows // MOE_ROWS
    grid_spec = pltpu.PrefetchScalarGridSpec(
        num_scalar_prefetch=2,
        grid=(nb,),
        in_specs=[pl.BlockSpec((MOE_ROWS, D_MODEL), lambda b, be, nu: (jnp.minimum(b, nu[0] - 1), 0)),
                  pl.BlockSpec((1, D_MODEL, D_EXPERT), lambda b, be, nu: (be[b], 0, 0)),
                  pl.BlockSpec((1, D_MODEL, D_EXPERT), lambda b, be, nu: (be[b], 0, 0)),
                  pl.BlockSpec((1, D_EXPERT, D_MODEL), lambda b, be, nu: (be[b], 0, 0))],
        out_specs=pl.BlockSpec((MOE_ROWS, D_MODEL), lambda b, be, nu: (b, 0)),
        scratch_shapes=[pltpu.VMEM((D_MODEL, D_EXPERT), BF16), pltpu.VMEM((D_MODEL, D_EXPERT), BF16),
                        pltpu.VMEM((D_EXPERT, D_MODEL), BF16)],
    )
    return pl.pallas_call(
        _experts_kernel,
        out_shape=jax.ShapeDtypeStruct((n_rows, D_MODEL), F32),
        grid_spec=grid_spec,
        compiler_params=_cparams(("arbitrary",)),
        name="experts",
    )(blk_e, n_used, xs, wg, wu, wd)


def _combine_kernel(off_ref, ids_ref, rank_ref, x1_ref, gate_ref, mod_ref, fg_ref, ys_ref, y_ref, buf, sem):
    def start(t, c):
        for k in range(2):
            d = _row_dest(off_ref, ids_ref, rank_ref, 2 * t + k)
            pltpu.make_async_copy(ys_ref.at[pl.ds(d, 1)], buf.at[k, pl.ds(t, 1)], sem).start()
        return c

    lax.fori_loop(0, TOK_TILE, start, 0, unroll=8)
    for k in range(2):
        pltpu.make_async_copy(ys_ref.at[pl.ds(0, TOK_TILE)], buf.at[k], sem).wait()

    gate = gate_ref[...]
    lane = _iota(gate.shape, 1)
    w0 = jnp.sum(jnp.where(lane == 0, gate, 0.0), axis=-1, keepdims=True)
    w1 = jnp.sum(jnp.where(lane == 1, gate, 0.0), axis=-1, keepdims=True)
    g2 = mod_ref[0][:, 5 * D_MODEL:6 * D_MODEL]
    y = x1_ref[...] + g2 * (w0 * buf[0] + w1 * buf[1])
    ms = jnp.mean(y * y, axis=-1, keepdims=True)
    y_ref[...] = y * lax.rsqrt(ms + NORM_EPS) * fg_ref[...]


def _combine_call(off, ids3, rank3, x1, rgate, mod3, mod_row, fg, ys):
    nt = x1.shape[0]
    smem_blk = pl.BlockSpec((1, 1, 2 * TOK_TILE), lambda i: (i, 0, 0), memory_space=pltpu.SMEM)
    return pl.pallas_call(
        _combine_kernel,
        out_shape=jax.ShapeDtypeStruct((nt, D_MODEL), F32),
        grid=(nt // TOK_TILE,),
        in_specs=[pl.BlockSpec(memory_space=pltpu.SMEM), smem_blk, smem_blk,
                  pl.BlockSpec((TOK_TILE, D_MODEL), lambda i: (i, 0)),
                  pl.BlockSpec((TOK_TILE, LANES), lambda i: (i, 0)),
                  pl.BlockSpec((1, 1, 6 * D_MODEL), lambda i: (mod_row(i), 0, 0)),
                  pl.BlockSpec((1, D_MODEL), lambda i: (0, 0)),
                  pl.BlockSpec(memory_space=pl.ANY)],
        out_specs=pl.BlockSpec((TOK_TILE, D_MODEL), lambda i: (i, 0)),
        scratch_shapes=[pltpu.VMEM((2, TOK_TILE, D_MODEL), F32), pltpu.SemaphoreType.DMA(())],
        compiler_params=_cparams(("arbitrary",)),
        name="combine",
    )(off, ids3, rank3, x1, rgate, mod3, fg, ys)


def kernel(x_prompt, x_sample, state_rwkv, state_gdn, c, c_ctx, ada_w, ada_b, norm1_g, norm2_g, w_in, w_out,
           rwkv_w0, rwkv_w2, rwkv_a0, rwkv_a2, rwkv_g2, rwkv_k_k, rwkv_k_a, rwkv_r_k, rwkv_lnx_g, rwkv_lnx_b,
           gdn_conv, gdn_a_log, gdn_dt_bias, gdn_norm_g, router_group_w, router_group_b, router_expert_w,
           router_expert_b, expert_gate, expert_up, expert_down, final_norm_g):
    assert ada_w.shape[0] == 1, "one layer"
    bp, tp, _ = x_prompt.shape
    bs, ts, _ = x_sample.shape
    lp = dict(rwkv_w0=rwkv_w0[0], rwkv_w2=rwkv_w2[0], rwkv_a0=rwkv_a0[0], rwkv_a2=rwkv_a2[0], rwkv_g2=rwkv_g2[0],
              rwkv_k_k=rwkv_k_k[0], rwkv_k_a=rwkv_k_a[0], rwkv_r_k=rwkv_r_k[0], rwkv_lnx_g=rwkv_lnx_g[0],
              rwkv_lnx_b=rwkv_lnx_b[0], gdn_conv=gdn_conv[0], gdn_a_log=gdn_a_log[0],
              gdn_dt_bias=gdn_dt_bias[0], gdn_norm_g=gdn_norm_g[0])
    rw_w = _rwkv_weights(lp)
    gd_w = _gdn_weights(lp)

    n_cond = 1 + bs
    cond = jnp.concatenate([c_ctx[None, :], c, jnp.zeros((16 - n_cond, D_MODEL), F32)], axis=0)
    mod = _mod_call(cond, ada_w[0], ada_b)
    mod3 = mod.reshape(16, 1, 6 * D_MODEL)

    in_cols = w_in.shape[2]
    w_in_bf = jnp.pad(w_in[0], ((0, 0), (0, RW + GW_PAD - in_cols))).astype(BF16)
    w_out_bf = w_out[0].astype(BF16)
    n1g = norm1_g.reshape(1, D_MODEL)
    n2g = norm2_g.reshape(1, D_MODEL)
    fg = final_norm_g.reshape(1, D_MODEL)
    rw = jnp.zeros((D_MODEL, LANES), F32).at[:, :N_GROUPS].set(router_group_w[0])
    rw = rw.at[:, N_GROUPS:N_GROUPS + N_EXPERTS].set(router_expert_w[0])
    rb = jnp.zeros((1, LANES), F32).at[0, :N_GROUPS].set(router_group_b[0])
    rb = rb.at[0, N_GROUPS:N_GROUPS + N_EXPERTS].set(router_expert_b[0])

    tiles_per_sample = ts // TOK_TILE
    passes = [
        dict(x=x_prompt.reshape(bp * tp, D_MODEL), batch=bp, seq=tp, row_len=tp,
             mod_row=lambda i: 0,
             s_r=None, s_g=None),
        dict(x=x_sample.reshape(bs * ts, D_MODEL), batch=bs, seq=ts, row_len=GRID_W,
             mod_row=lambda i: 1 + i // tiles_per_sample,
             s_r=state_rwkv[:, 0], s_g=state_gdn[:, 0]),
    ]

    for p in passes:
        proj_r, proj_g = _inproj_call(p["x"], mod3, p["mod_row"], n1g, w_in_bf)
        o_r, p["st_r"] = _rwkv_call(proj_r, p["s_r"], rw_w, batch=p["batch"], seq=p["seq"], nrow=SEQS_PER_STEP,
                                    hp=False)
        nrow_g = GDN_SEQS_PER_STEP_SHORT if p["seq"] <= 256 else SEQS_PER_STEP
        o_g, p["st_g"] = _gdn_call(proj_g, p["s_g"], gd_w, batch=p["batch"], seq=p["seq"], nrow=nrow_g,
                                   row_len=p["row_len"], hp=False)
        p["x1"], p["h2"], p["rid"], p["rgate"] = _outproj_call(
            p["x"], o_r, o_g, mod3, p["mod_row"], w_out_bf, n2g, rw, rb)

    rid = jnp.concatenate([p["rid"] for p in passes], axis=0)
    rank, cnt = _plan_call(rid)
    counts = cnt[0, :N_EXPERTS]
    padded = (counts + MOE_ROWS - 1) // MOE_ROWS * MOE_ROWS
    ends = jnp.cumsum(padded)
    off = (ends - padded).astype(jnp.int32)
    nt_all = rid.shape[0]
    n_blocks = (2 * nt_all) // MOE_ROWS + N_EXPERTS
    n_rows = n_blocks * MOE_ROWS
    n_used = (ends[-1] // MOE_ROWS).astype(jnp.int32)
    blk_start = jnp.minimum(jnp.arange(n_blocks, dtype=jnp.int32), n_used - 1) * MOE_ROWS
    blk_e = jnp.minimum(jnp.sum((blk_start[:, None] >= ends[None, :]).astype(jnp.int32), axis=1), N_EXPERTS - 1)

    ids3 = rid[:, :2].reshape(nt_all // TOK_TILE, 1, 2 * TOK_TILE)
    rank3 = rank[:, :2].reshape(nt_all // TOK_TILE, 1, 2 * TOK_TILE)
    xs = None
    t0 = 0
    for p in passes:
        nt = p["x"].shape[0] // TOK_TILE
        p["ids3"], p["rank3"] = ids3[t0:t0 + nt], rank3[t0:t0 + nt]
        xs = _dispatch_call(off, p["ids3"], p["rank3"], p["h2"], xs, n_rows)
        t0 += nt
    ys = _experts_call(blk_e, n_used.reshape(1), xs, expert_gate[0], expert_up[0], expert_down[0])
    outs = [_combine_call(off, p["ids3"], p["rank3"], p["x1"], p["rgate"], mod3, p["mod_row"], fg, ys)
            for p in passes]

    y_prompt = outs[0].reshape(bp, tp, D_MODEL)
    y_sample = outs[1].reshape(bs, ts, D_MODEL)
    new_state_rwkv = passes[0]["st_r"][:, None]
    new_state_gdn = passes[0]["st_g"][:, None]
    return (y_prompt, y_sample, new_state_rwkv, new_state_gdn)
```

```python
import functools
import math

import jax
import jax.numpy as jnp
from jax import lax
from jax.experimental import pallas as pl
from jax.experimental.pallas import tpu as pltpu

F32 = jnp.float32
BF16 = jnp.bfloat16

D_MODEL = 1024
R_HEADS, R_HEAD = 8, 64
R_WIDTH = R_HEADS * R_HEAD
G_HEADS, G_HEAD = 4, 128
G_WIDTH = G_HEADS * G_HEAD
LNX_EPS = 64e-5
NORM_EPS = 1e-6
N_GROUPS, EXPERTS_PER_GROUP = 4, 8
N_EXPERTS = N_GROUPS * EXPERTS_PER_GROUP
D_EXPERT = 512
GRID_W = 64

CHUNK = 64
TOK_TILE = 256
MOE_ROWS = 256
SEQS_PER_STEP = 2
GDN_SEQS_PER_STEP_SHORT = 4
LANES = 128
RW = 3 * R_WIDTH + 256
GW_PAD = 4 * G_WIDTH + LANES
VMEM_LIMIT = 56 * 1024 * 1024


def _cparams(sem):
    return pltpu.CompilerParams(dimension_semantics=sem, vmem_limit_bytes=VMEM_LIMIT)


def _sigmoid(x):
    return 1.0 / (1.0 + jnp.exp(-x))


def _silu(x):
    return x * _sigmoid(x)


def _softplus(x):
    return jnp.maximum(x, 0.0) + jnp.log(1.0 + jnp.exp(-jnp.abs(x)))


def _dg(a, b, ca, cb):
    return lax.dot_general(a, b, (((ca,), (cb,)), ((), ())), preferred_element_type=F32)


def _split2(x):
    hi = x.astype(BF16)
    lo = (x - hi.astype(F32)).astype(BF16)
    return hi, lo


def _split3(x):
    h1 = x.astype(BF16)
    r1 = x - h1.astype(F32)
    h2 = r1.astype(BF16)
    h3 = (r1 - h2.astype(F32)).astype(BF16)
    return h1, h2, h3


def _mm(a, b, ca=1, cb=0, hp=False):
    if not hp:
        return _dg(a.astype(BF16), b.astype(BF16), ca, cb)
    ah, al = _split2(a)
    bh, bl = _split2(b)
    return _dg(ah, bh, ca, cb) + (_dg(ah, bl, ca, cb) + _dg(al, bh, ca, cb))


def _mm_exact_lhs(m_bf, x, ca=1, cb=0, parts=3):
    if parts == 2:
        hi, lo = _split2(x)
        return _dg(m_bf, hi, ca, cb) + _dg(m_bf, lo, ca, cb)
    h1, h2, h3 = _split3(x)
    return _dg(m_bf, h1, ca, cb) + (_dg(m_bf, h2, ca, cb) + _dg(m_bf, h3, ca, cb))


def _mm_exact_rhs(x, m_bf):
    h1, h2, h3 = _split3(x)
    return _dg(h1, m_bf, 1, 0) + (_dg(h2, m_bf, 1, 0) + _dg(h3, m_bf, 1, 0))


def _group_sum(x, ones_bf):
    w = ones_bf.shape[0]
    hi, lo = _split2(x)
    return jnp.concatenate([_dg(hi[:, j:j + w], ones_bf, 1, 0) + _dg(lo[:, j:j + w], ones_bf, 1, 0)
                            for j in range(0, x.shape[1], w)], axis=1)


def _iota(shape, axis):
    return lax.broadcasted_iota(jnp.int32, shape, axis)


def _tri_masks(n, rev):
    t = _iota((n, n), 0)
    s = _iota((n, n), 1)
    if rev:
        return t < s, t <= s
    return t > s, t >= s


PACK = 4


def _level_masks(n):
    t = _iota((n, PACK * n), 0)
    u = _iota((n, PACK * n), 1) % n
    levels = []
    s = 1
    while s < n:
        levels.append((t // (2 * s) == u // (2 * s)) & (t // s != u // s))
        s *= 2
    diag = _iota((PACK * n, PACK * n), 0) // n == _iota((PACK * n, PACK * n), 1) // n
    return levels, diag


def _block_diag(x, diag):
    return jnp.where(diag, jnp.concatenate([x] * PACK, axis=0), 0.0).astype(BF16)


def _unit_tri_inverse(ms, masks):
    levels, diag = masks
    n = ms[0].shape[0]
    eye = (_iota((n, PACK * n), 0) == _iota((n, PACK * n), 1) % n).astype(F32)
    xs = [eye - jnp.where(levels[0], m, 0.0) for m in ms]
    for mask in levels[1:]:
        zs = [_dg(jnp.where(mask, m, 0.0).astype(BF16), _block_diag(x, diag), 1, 0) for m, x in zip(ms, xs)]
        xs = [x - _dg(x.astype(BF16), _block_diag(z, diag), 1, 0) for x, z in zip(xs, zs)]
    return xs


def _mod_kernel(c_ref, w_ref, b_ref, o_ref):
    s = _silu(c_ref[...])
    o_ref[...] = _mm(s, w_ref[...], hp=True) + b_ref[...]


def _mod_call(cond, ada_w, ada_b):
    rows = cond.shape[0]
    n = ada_w.shape[1]
    bn = 1024
    return pl.pallas_call(
        _mod_kernel,
        out_shape=jax.ShapeDtypeStruct((rows, n), F32),
        grid=(n // bn,),
        in_specs=[pl.BlockSpec((rows, D_MODEL), lambda j: (0, 0)),
                  pl.BlockSpec((D_MODEL, bn), lambda j: (0, j)),
                  pl.BlockSpec((1, bn), lambda j: (0, j))],
        out_specs=pl.BlockSpec((rows, bn), lambda j: (0, j)),
        compiler_params=_cparams(("arbitrary",)),
        name="mod",
    )(cond, ada_w, ada_b)


def _inproj_kernel(x_ref, mod_ref, g_ref, w_ref, or_ref, og_ref):
    x = x_ref[...]
    ms = jnp.mean(x * x, axis=-1, keepdims=True)
    y = x * lax.rsqrt(ms + NORM_EPS) * g_ref[...]
    m = mod_ref[0]
    h = y * (1.0 + m[:, D_MODEL:2 * D_MODEL]) + m[:, :D_MODEL]
    p = jnp.dot(h.astype(BF16), w_ref[...], preferred_element_type=F32)
    or_ref[...] = p[:, :RW].astype(BF16)
    og_ref[...] = p[:, RW:].astype(BF16)


def _inproj_call(x2, mod3, mod_row, norm_g, w_in_bf):
    nt = x2.shape[0]
    ncol = w_in_bf.shape[1]
    return pl.pallas_call(
        _inproj_kernel,
        out_shape=(jax.ShapeDtypeStruct((nt, RW), BF16),
                   jax.ShapeDtypeStruct((nt, GW_PAD), BF16)),
        grid=(nt // TOK_TILE,),
        in_specs=[pl.BlockSpec((TOK_TILE, D_MODEL), lambda i: (i, 0)),
                  pl.BlockSpec((1, 1, 6 * D_MODEL), lambda i: (mod_row(i), 0, 0)),
                  pl.BlockSpec((1, D_MODEL), lambda i: (0, 0)),
                  pl.BlockSpec((D_MODEL, ncol), lambda i: (0, 0))],
        out_specs=(pl.BlockSpec((TOK_TILE, RW), lambda i: (i, 0)),
                   pl.BlockSpec((TOK_TILE, GW_PAD), lambda i: (i, 0))),
        compiler_params=_cparams(("arbitrary",)),
        name="inproj",
    )(x2, mod3, norm_g, w_in_bf)


def _head_blocks(width, head):
    return (_iota((width, width), 0) // head == _iota((width, width), 1) // head).astype(BF16)


def _rwkv_kernel(p_ref, *refs, seq, nrow, hp, has_s0):
    s0_ref = refs[0] if has_s0 else None
    (w0_ref, wd_ref, a0_ref, wa_ref, g2_ref, kk_ref, ka_ref, rk_ref, lng_ref, lnb_ref,
     o_ref, st_ref, y_scr) = refs[1:] if has_s0 else refs
    C = CHUNK
    nc = seq // C
    blk = _head_blocks(2 * LANES, R_HEAD)
    k_k = kk_ref[...]
    k_a = ka_ref[...]
    r_k = rk_ref[...]
    exp_m05 = math.exp(-0.5)
    inv_masks = _level_masks(C)

    def lora_lhs(la):
        return jnp.where(_iota(la.shape, 1) < 64, jnp.tanh(la), la)

    def cols(n, rows, lo, hi):
        return p_ref[n, rows, lo:hi].astype(F32)

    t2 = _iota((C, 2 * C), 0)
    s2 = _iota((C, 2 * C), 1) % C
    dirs = []
    for d, rev in ((0, False), (1, True)):
        strict, incl = _tri_masks(C, rev)
        dirs.append(dict(d=d, rev=rev, strict=strict, tri=incl.astype(BF16),
                         incl2=(t2 <= s2) if rev else (t2 >= s2)))
    hd = range(R_HEADS)
    sls = [slice(h * R_HEAD, (h + 1) * R_HEAD) for h in hd]

    def prep(n, dr, c):
        d, rev, tri = dr["d"], dr["rev"], dr["tri"]
        rows = pl.ds(pl.multiple_of(c * C, C), C)
        rr = cols(n, rows, 0, R_WIDTH)
        kx = cols(n, rows, R_WIDTH, 2 * R_WIDTH)
        vv = cols(n, rows, 2 * R_WIDTH, 3 * R_WIDTH)
        lhs = lora_lhs(cols(n, rows, 3 * R_WIDTH, 3 * R_WIDTH + LANES))
        logw = -exp_m05 * _sigmoid(w0_ref[d] + _mm(lhs, wd_ref[d]))
        a = _sigmoid(a0_ref[d] + _mm(lhs, wa_ref[d]))
        kkr = kx * k_k
        kk = kkr * lax.rsqrt(_group_sum(kkr * kkr, blk) + 1e-6)
        kd = kx * (1.0 + (a - 1.0) * k_a)
        bvec = kk * a
        gi = _mm_exact_lhs(tri, logw, parts=2)
        gtot = gi[0:1] if rev else gi[C - 1:C]
        en = jnp.exp(-gi)
        ee = jnp.exp(gtot - gi)
        return dict(rows=rows, vv=vv, rt=rr * jnp.exp(gi), at=-kk * jnp.exp(gi - logw), bt=bvec * en, kt=kd * en,
                    bh=bvec * ee, kh=kd * ee, gc=jnp.exp(gtot))

    seqs = range(nrow)
    items = [(n, d, h) for n in seqs for d in (0, 1) for h in hd]
    packs = [(n, d, g) for n in seqs for d in (0, 1) for g in range(R_HEADS // PACK)]

    def step(i, carry):
        ops = {(n, d): prep(n, dirs[d], (nc - 1 - i) if d else i) for n in seqs for d in (0, 1)}
        hcol = lambda name, n, d, h: ops[n, d][name][:, sls[h]]
        ars = {(n, d, h): jnp.concatenate([hcol("at", n, d, h), hcol("rt", n, d, h)], axis=0) for n, d, h in items}
        pms = {(n, d, h): _mm(ars[n, d, h], jnp.concatenate([hcol("bt", n, d, h), hcol("kt", n, d, h)], axis=0),
                              1, 1, hp=hp) for n, d, h in items}
        shs = {(n, d, h): carry[2 * n + d][:, sls[h]] for n, d, h in items}
        x0s = {k: _mm(ars[k], shs[k], 1, 1, hp=hp) for k in items}
        vhs = {(n, d, h): hcol("vv", n, d, h) for n, d, h in items}
        akv = {(n, d, h): _mm(jnp.where(dirs[d]["strict"], pms[n, d, h][:C, C:], 0.0), vhs[n, d, h], hp=hp)
               for n, d, h in items}
        packed = [jnp.concatenate([jnp.where(dirs[d]["strict"], -pms[n, d, PACK * g + j][:C, :C], 0.0)
                                   for j in range(PACK)], axis=1) for n, d, g in packs]
        tcat = dict(zip(packs, _unit_tri_inverse(packed, inv_masks)))
        tms = {(n, d, h): tcat[n, d, h // PACK][:, (h % PACK) * C:(h % PACK + 1) * C] for n, d, h in items}
        us = {k: _mm(tms[k], x0s[k][:C] + akv[k], hp=hp) for k in items}
        uvs = {k: jnp.concatenate([us[k], vhs[k]], axis=0) for k in items}
        ys = {(n, d, h): x0s[n, d, h][C:] + _mm(jnp.where(dirs[d]["incl2"], pms[n, d, h][C:], 0.0), uvs[n, d, h])
              for n, d, h in items}
        snew = {(n, d, h): shs[n, d, h] * hcol("gc", n, d, h)
                + _mm(uvs[n, d, h], jnp.concatenate([hcol("bh", n, d, h), hcol("kh", n, d, h)], axis=0),
                      0, 0, hp=hp) for n, d, h in items}
        for n in seqs:
            for d in (0, 1):
                y_scr[n, d, ops[n, d]["rows"], :] = jnp.concatenate([ys[n, d, h] for h in hd], axis=1)
        return tuple(jnp.concatenate([snew[n, d, h] for h in hd], axis=1) for n in seqs for d in (0, 1))

    def initial(n, d):
        if not has_s0:
            return jnp.zeros((R_HEAD, R_WIDTH), F32)
        return jnp.concatenate([s0_ref[n, d, h] for h in hd], axis=1)

    s_fin = lax.fori_loop(0, nc, step, tuple(initial(n, d) for n in seqs for d in (0, 1)))
    for n in seqs:
        for d in (0, 1):
            for h in hd:
                st_ref[n, d, h] = s_fin[2 * n + d][:, sls[h]]

    g2 = g2_ref[...]
    lng = lng_ref[...]
    lnb = lnb_ref[...]
    inv_n = 1.0 / R_HEAD
    fb = min(seq, 256)

    def finish(j, carry):
        rows = pl.ds(pl.multiple_of(j * fb, fb), fb)
        for n in seqs:
            y = y_scr[n, 0, rows, :] + y_scr[n, 1, rows, :]
            mu = _group_sum(y, blk) * inv_n
            yc = y - mu
            var = _group_sum(yc * yc, blk) * inv_n
            yn = yc * lax.rsqrt(var + LNX_EPS) * lng + lnb
            lhs = lora_lhs(cols(n, rows, 3 * R_WIDTH, 3 * R_WIDTH + LANES))
            a_sum = _sigmoid(a0_ref[0] + _mm(lhs, wa_ref[0])) + _sigmoid(a0_ref[1] + _mm(lhs, wa_ref[1]))
            rr = cols(n, rows, 0, R_WIDTH)
            kx = cols(n, rows, R_WIDTH, 2 * R_WIDTH)
            kd_sum = kx * (2.0 + (a_sum - 2.0) * k_a)
            bonus = _group_sum(rr * kd_sum * r_k, blk) * cols(n, rows, 2 * R_WIDTH, 3 * R_WIDTH)
            gl = cols(n, rows, 3 * R_WIDTH + LANES, 3 * R_WIDTH + 2 * LANES)
            gate = _mm(_sigmoid(gl), g2)
            o_ref[n, rows, :] = (yn + bonus) * gate
        return carry

    lax.fori_loop(0, seq // fb, finish, 0)


def _rwkv_call(proj_r, s0, w, *, batch, seq, nrow, hp):
    p3 = proj_r.reshape(batch, seq, RW)
    kern = functools.partial(_rwkv_kernel, seq=seq, nrow=nrow, hp=hp, has_s0=s0 is not None)
    full = lambda a: pl.BlockSpec(a.shape, lambda b: (0,) * a.ndim)
    ws = [w["w0"], w["wd"], w["a0"], w["wa"], w["g2"], w["k_k"], w["k_a"], w["r_k"], w["lnx_g"], w["lnx_b"]]
    state_spec = pl.BlockSpec((nrow, 2, R_HEADS, R_HEAD, R_HEAD), lambda b: (b, 0, 0, 0, 0))
    states = [] if s0 is None else [s0]
    o, st = pl.pallas_call(
        kern,
        out_shape=(jax.ShapeDtypeStruct((batch, seq, R_WIDTH), F32),
                   jax.ShapeDtypeStruct((batch, 2, R_HEADS, R_HEAD, R_HEAD), F32)),
        grid=(batch // nrow,),
        in_specs=[pl.BlockSpec((nrow, seq, RW), lambda b: (b, 0, 0))] + [state_spec] * len(states)
        + [full(a) for a in ws],
        out_specs=(pl.BlockSpec((nrow, seq, R_WIDTH), lambda b: (b, 0, 0)), state_spec),
        scratch_shapes=[pltpu.VMEM((nrow, 2, seq, R_WIDTH), F32)],
        compiler_params=_cparams(("arbitrary",)),
        name="rwkv_hp" if hp else "rwkv",
    )(p3, *states, *ws)
    return o.reshape(batch * seq, R_WIDTH), st


def _rwkv_weights(lp):
    zeros = jnp.zeros((2, 64, R_WIDTH), F32)
    row = lambda a: jnp.asarray(a, F32).reshape(1, R_WIDTH)
    return {
        "w0": jnp.asarray(lp["rwkv_w0"], F32).reshape(2, 1, R_WIDTH),
        "wd": jnp.concatenate([jnp.asarray(lp["rwkv_w2"], F32), zeros], axis=1),
        "a0": jnp.asarray(lp["rwkv_a0"], F32).reshape(2, 1, R_WIDTH),
        "wa": jnp.concatenate([zeros, jnp.asarray(lp["rwkv_a2"], F32)], axis=1),
        "g2": jnp.asarray(lp["rwkv_g2"], F32),
        "k_k": row(lp["rwkv_k_k"]), "k_a": row(lp["rwkv_k_a"]), "r_k": row(lp["rwkv_r_k"]),
        "lnx_g": row(lp["rwkv_lnx_g"]), "lnx_b": row(lp["rwkv_lnx_b"]),
    }


def _gdn_kernel(p_ref, *refs, seq, nrow, row_len, hp, has_s0):
    s0_ref = refs[0] if has_s0 else None
    (cw_ref, alog_ref, dtb_ref, ng_ref, o_ref, st_ref, q_scr, k_scr, v_scr, o_scr) = refs[1:] if has_s0 else refs
    C = CHUNK
    nc = seq // C
    fb = min(seq, 256)
    W = G_WIDTH
    q_scale = G_HEAD ** -0.5
    seqs = range(nrow)
    hd = range(G_HEADS)
    hss = [slice(h * G_HEAD, (h + 1) * G_HEAD) for h in hd]

    cw0 = cw_ref[0:1, :]
    cw1 = cw_ref[1:2, :]
    cw2 = cw_ref[2:3, :]
    tpos = _iota((fb, 1), 0) % row_len
    is_first = tpos == 0
    is_last = tpos == row_len - 1

    def prep(j, carry):
        rows = pl.ds(pl.multiple_of(j * fb, fb), fb)
        for n in seqs:
            x = p_ref[n, rows, 0:3 * W].astype(F32)
            xm = jnp.where(is_first, 0.0, pltpu.roll(x, 1, 0))
            xp = jnp.where(is_last, 0.0, pltpu.roll(x, fb - 1, 0))
            y = _silu(cw0 * xm + cw1 * x + cw2 * xp)
            for h in hd:
                qh = y[:, hss[h]]
                kh = y[:, W + h * G_HEAD:W + (h + 1) * G_HEAD]
                q_scr[n, rows, hss[h]] = qh * lax.rsqrt(jnp.sum(qh * qh, axis=-1, keepdims=True) + 1e-6) * q_scale
                k_scr[n, rows, hss[h]] = kh * lax.rsqrt(jnp.sum(kh * kh, axis=-1, keepdims=True) + 1e-6)
            v_scr[n, rows, :] = y[:, 2 * W:3 * W]
        return carry

    lax.fori_loop(0, seq // fb, prep, 0)

    assert G_HEADS == PACK
    inv_masks = _level_masks(C)
    alog = alog_ref[...]
    dtb = dtb_ref[...]
    sel_row = _iota((LANES, W), 0)
    sel_head = _iota((LANES, W), 1) // G_HEAD
    pk_row = _iota((LANES, PACK * C), 0)
    pk_head = _iota((LANES, PACK * C), 1) // C
    pk_t = _iota((C, PACK * C), 0)
    pk_s = _iota((C, PACK * C), 1) % C

    dirs = []
    for d, rev in ((0, False), (1, True)):
        dirs.append(dict(
            d=d, rev=rev, tri=_tri_masks(C, rev)[1].astype(BF16),
            strict=(pk_t < pk_s) if rev else (pk_t > pk_s),
            incl=(pk_t <= pk_s) if rev else (pk_t >= pk_s),
            upto=(pk_t >= pk_s) if rev else (pk_t <= pk_s),
            e_beta=(sel_row == 4 * d + sel_head).astype(BF16),
            e_g=(sel_row == 8 + 4 * d + sel_head).astype(BF16),
            e_g_pk=(pk_row == 8 + 4 * d + pk_head).astype(BF16)))
    items = [(n, d, h) for n in seqs for d in (0, 1) for h in hd]
    packs = [(n, d) for n in seqs for d in (0, 1)]

    def prep_dir(n, dr, c):
        rows = pl.ds(pl.multiple_of(c * C, C), C)
        ba = p_ref[n, rows, 4 * W:4 * W + LANES].astype(F32)
        gval = -jnp.exp(alog) * _softplus(ba + dtb)
        beta_b = _mm_exact_rhs(_sigmoid(ba), dr["e_beta"])
        gcum = _mm_exact_lhs(dr["tri"], _mm_exact_rhs(gval, dr["e_g"]))
        g_pk = _mm_exact_rhs(gval, dr["e_g_pk"])
        g_col = _mm_exact_lhs(dr["tri"], g_pk)
        g_row = jnp.sum(jnp.where(dr["upto"], g_pk, 0.0), axis=0, keepdims=True)
        return dict(rows=rows, beta=beta_b, gcum=gcum,
                    decay=jnp.where(dr["incl"], jnp.exp(jnp.where(dr["incl"], g_col - g_row, 0.0)), 0.0),
                    q=q_scr[n, rows, :], k=k_scr[n, rows, :], v=v_scr[n, rows, :])

    def step(i, carry):
        ops = {(n, d): prep_dir(n, dirs[d], (nc - 1 - i) if d else i) for n, d in packs}
        hcol = lambda name, n, d, h: ops[n, d][name][:, hss[h]]
        qhs = {k: hcol("q", *k) for k in items}
        khs = {k: hcol("k", *k) for k in items}
        gcs = {k: hcol("gcum", *k) for k in items}
        bbs = {k: hcol("beta", *k) for k in items}
        kbs = {k: khs[k] * bbs[k] for k in items}
        kks = {k: _mm(jnp.concatenate([kbs[k], qhs[k]], axis=0), khs[k], 1, 1, hp=hp) for k in items}
        kka = {(n, d): jnp.concatenate([kks[n, d, h][:C] for h in hd], axis=1) for n, d in packs}
        tcat = dict(zip(packs, _unit_tri_inverse(
            [jnp.where(dirs[d]["strict"], kka[n, d] * ops[n, d]["decay"], 0.0) for n, d in packs], inv_masks)))
        egs = {k: jnp.exp(gcs[k]) for k in items}
        uws = {(n, d, h): _mm(tcat[n, d][:, h * C:(h + 1) * C],
                              jnp.concatenate([hcol("v", n, d, h) * bbs[n, d, h], kbs[n, d, h] * egs[n, d, h]], axis=1),
                              hp=hp) for n, d, h in items}
        shs = {(n, d, h): carry[2 * n + d][:, hss[h]] for n, d, h in items}
        wss = {k: _mm(jnp.concatenate([uws[k][:, G_HEAD:], qhs[k] * egs[k]], axis=0), shs[k], hp=hp) for k in items}
        vns = {k: uws[k][:, :G_HEAD] - wss[k][:C] for k in items}
        outs = {(n, d, h): wss[n, d, h][C:]
                + _mm(kks[n, d, h][C:] * ops[n, d]["decay"][:, h * C:(h + 1) * C], vns[n, d, h]) for n, d, h in items}
        snew = {}
        for n, d, h in items:
            glast = gcs[n, d, h][0:1] if dirs[d]["rev"] else gcs[n, d, h][C - 1:C]
            k_dec = khs[n, d, h] * jnp.exp(glast - gcs[n, d, h])
            snew[n, d, h] = shs[n, d, h] * jnp.exp(glast) + _mm(k_dec, vns[n, d, h], 0, 0, hp=hp)
        for n, d in packs:
            o_scr[n, d, ops[n, d]["rows"], :] = jnp.concatenate([outs[n, d, h] for h in hd], axis=1)
        return tuple(jnp.concatenate([snew[n, d, h] for h in hd], axis=1) for n, d in packs)

    def initial(n, d):
        if not has_s0:
            return jnp.zeros((G_HEAD, W), F32)
        return jnp.concatenate([s0_ref[n, d, h] for h in hd], axis=1)

    s_fin = lax.fori_loop(0, nc, step, tuple(initial(n, d) for n, d in packs))
    for j, (n, d) in enumerate(packs):
        for h in hd:
            st_ref[n, d, h] = s_fin[j][:, hss[h]]

    ng = ng_ref[...]

    def finish(j, carry):
        rows = pl.ds(pl.multiple_of(j * fb, fb), fb)
        for n in seqs:
            o = o_scr[n, 0, rows, :] + o_scr[n, 1, rows, :]
            z = p_ref[n, rows, 3 * W:4 * W].astype(F32)
            for h in hd:
                oh = o[:, hss[h]]
                ms = jnp.mean(oh * oh, axis=-1, keepdims=True)
                o_ref[n, rows, hss[h]] = oh * lax.rsqrt(ms + NORM_EPS) * ng * _silu(z[:, hss[h]])
        return carry

    lax.fori_loop(0, seq // fb, finish, 0)


def _gdn_call(proj_g, s0, w, *, batch, seq, nrow, row_len, hp):
    p3 = proj_g.reshape(batch, seq, GW_PAD)
    kern = functools.partial(_gdn_kernel, seq=seq, nrow=nrow, row_len=row_len, hp=hp, has_s0=s0 is not None)
    full = lambda a: pl.BlockSpec(a.shape, lambda b: (0,) * a.ndim)
    ws = [w["conv"], w["alog"], w["dtb"], w["norm_g"]]
    state_spec = pl.BlockSpec((nrow, 2, G_HEADS, G_HEAD, G_HEAD), lambda b: (b, 0, 0, 0, 0))
    states = [] if s0 is None else [s0]
    o, st = pl.pallas_call(
        kern,
        out_shape=(jax.ShapeDtypeStruct((batch, seq, G_WIDTH), F32),
                   jax.ShapeDtypeStruct((batch, 2, G_HEADS, G_HEAD, G_HEAD), F32)),
        grid=(batch // nrow,),
        in_specs=[pl.BlockSpec((nrow, seq, GW_PAD), lambda b: (b, 0, 0))] + [state_spec] * len(states)
        + [full(a) for a in ws],
        out_specs=(pl.BlockSpec((nrow, seq, G_WIDTH), lambda b: (b, 0, 0)), state_spec),
        scratch_shapes=[pltpu.VMEM((nrow, seq, G_WIDTH), F32)] * 3 + [pltpu.VMEM((nrow, 2, seq, G_WIDTH), F32)],
        compiler_params=_cparams(("arbitrary",)),
        name="gdn_hp" if hp else "gdn",
    )(p3, *states, *ws)
    return o.reshape(batch * seq, G_WIDTH), st


def _gdn_weights(lp):
    lanes = jnp.zeros((1, LANES), F32)
    return {
        "conv": jnp.asarray(lp["gdn_conv"], F32),
        "alog": lanes.at[0, 8:16].set(jnp.asarray(lp["gdn_a_log"], F32).reshape(8)),
        "dtb": lanes.at[0, 8:16].set(jnp.asarray(lp["gdn_dt_bias"], F32).reshape(8)),
        "norm_g": jnp.asarray(lp["gdn_norm_g"], F32).reshape(1, G_HEAD),
    }


def _outproj_kernel(x_ref, or_ref, og_ref, mod_ref, wo_ref, n2_ref, rw_ref, rb_ref,
                    x1_ref, h2_ref, rid_ref, rgate_ref):
    m = mod_ref[0]
    g1 = m[:, 2 * D_MODEL:3 * D_MODEL]
    sh2 = m[:, 3 * D_MODEL:4 * D_MODEL]
    sc2 = m[:, 4 * D_MODEL:5 * D_MODEL]
    mix = (jnp.dot(or_ref[...].astype(BF16), wo_ref[0:R_WIDTH, :], preferred_element_type=F32)
           + jnp.dot(og_ref[...].astype(BF16), wo_ref[R_WIDTH:, :], preferred_element_type=F32))
    x1 = x_ref[...] + g1 * mix
    x1_ref[...] = x1
    ms = jnp.mean(x1 * x1, axis=-1, keepdims=True)
    h2 = x1 * lax.rsqrt(ms + NORM_EPS) * n2_ref[...] * (1.0 + sc2) + sh2
    h2_ref[...] = h2

    logits = _mm(h2, rw_ref[...], hp=True) + rb_ref[...]
    lane = _iota(logits.shape, 1)
    neg = jnp.float32(-1e30)
    big = jnp.int32(1 << 20)

    def first_argmax(v):
        mx = jnp.max(v, axis=-1, keepdims=True)
        idx = jnp.min(jnp.where(v == mx, lane, big), axis=-1, keepdims=True)
        return mx, idx

    lg = jnp.where(lane < N_GROUPS, logits, neg)
    mg, grp = first_argmax(lg)
    p_grp = 1.0 / jnp.sum(jnp.where(lane < N_GROUPS, jnp.exp(lg - mg), 0.0), axis=-1, keepdims=True)
    in_grp = (lane >= N_GROUPS) & (lane < N_GROUPS + N_EXPERTS) & ((lane - N_GROUPS) // EXPERTS_PER_GROUP == grp)
    le = jnp.where(in_grp, logits, neg)
    m1, i1 = first_argmax(le)
    m2, i2 = first_argmax(jnp.where(lane == i1, neg, le))
    e2 = jnp.exp(m2 - m1)
    w1 = p_grp / (1.0 + e2)
    w2 = p_grp * e2 / (1.0 + e2)
    rid_ref[...] = jnp.where(lane == 0, i1 - N_GROUPS, jnp.where(lane == 1, i2 - N_GROUPS, 0))
    rgate_ref[...] = jnp.where(lane == 0, w1, jnp.where(lane == 1, w2, 0.0))


def _outproj_call(x2, o_r, o_g, mod3, mod_row, wo_bf, n2g, rw, rb):
    nt = x2.shape[0]
    tile = lambda w: pl.BlockSpec((TOK_TILE, w), lambda i: (i, 0))
    const = lambda a: pl.BlockSpec(a.shape, lambda i: (0,) * a.ndim)
    return pl.pallas_call(
        _outproj_kernel,
        out_shape=(jax.ShapeDtypeStruct((nt, D_MODEL), F32), jax.ShapeDtypeStruct((nt, D_MODEL), F32),
                   jax.ShapeDtypeStruct((nt, LANES), jnp.int32), jax.ShapeDtypeStruct((nt, LANES), F32)),
        grid=(nt // TOK_TILE,),
        in_specs=[tile(D_MODEL), tile(R_WIDTH), tile(G_WIDTH),
                  pl.BlockSpec((1, 1, 6 * D_MODEL), lambda i: (mod_row(i), 0, 0)),
                  const(wo_bf), const(n2g), const(rw), const(rb)],
        out_specs=(tile(D_MODEL), tile(D_MODEL), tile(LANES), tile(LANES)),
        compiler_params=_cparams(("arbitrary",)),
        name="outproj",
    )(x2, o_r, o_g, mod3, wo_bf, n2g, rw, rb)


PLAN_TILE = 1024


def _plan_kernel(rid_ref, cnt_in_ref, route_ref, cnt_ref, carry):
    i = pl.program_id(0)

    @pl.when(i == 0)
    def _():
        carry[...] = cnt_in_ref[...].astype(F32)

    rid = rid_ref[...]
    lane = _iota(rid.shape, 1)
    e0 = jnp.sum(jnp.where(lane == 0, rid, 0), axis=-1, keepdims=True)
    e1 = jnp.sum(jnp.where(lane == 1, rid, 0), axis=-1, keepdims=True)
    oh0 = (lane == e0).astype(F32)
    oh1 = (lane == e1).astype(F32)
    oh = oh0 + oh1
    n = rid.shape[0]
    earlier = (_iota((n, n), 0) > _iota((n, n), 1)).astype(BF16)
    before = jnp.dot(earlier, oh.astype(BF16), preferred_element_type=F32) + carry[0:1, :]
    r0 = jnp.sum(oh0 * before, axis=-1, keepdims=True)
    r1 = jnp.sum(oh1 * before, axis=-1, keepdims=True)
    cols = jnp.where(lane == 0, e0.astype(F32), jnp.where(lane == 1, e1.astype(F32),
                                                          jnp.where(lane == 2, r0, jnp.where(lane == 3, r1, 0.0))))
    route_ref[...] = jnp.transpose(cols)[0:8, :].astype(jnp.int32)
    total = carry[0:1, :] + jnp.sum(oh, axis=0, keepdims=True)
    carry[...] = jnp.broadcast_to(total, carry.shape)
    cnt_ref[...] = jnp.broadcast_to(total, cnt_ref.shape).astype(jnp.int32)


def _plan_call(rid, cnt_in):
    nt = rid.shape[0]
    return pl.pallas_call(
        _plan_kernel,
        out_shape=(jax.ShapeDtypeStruct((8, nt), jnp.int32), jax.ShapeDtypeStruct((8, LANES), jnp.int32)),
        grid=(nt // PLAN_TILE,),
        in_specs=[pl.BlockSpec((PLAN_TILE, LANES), lambda i: (i, 0)), pl.BlockSpec((8, LANES), lambda i: (0, 0))],
        out_specs=(pl.BlockSpec((8, PLAN_TILE), lambda i: (0, i)), pl.BlockSpec((8, LANES), lambda i: (0, 0))),
        scratch_shapes=[pltpu.VMEM((8, LANES), F32)],
        compiler_params=_cparams(("arbitrary",)),
        name="plan",
    )(rid, cnt_in)


def _slots_kernel(off_ref, route_ref, slot_ref):
    route = route_ref[...]
    ids = jnp.concatenate([route[0:2], route[0:2], route[0:2], route[0:2]], axis=0)
    rank = jnp.concatenate([route[2:4], route[2:4], route[2:4], route[2:4]], axis=0)
    base = jnp.zeros_like(ids)
    for e in range(N_EXPERTS):
        base = jnp.where(ids == e, off_ref[e], base)
    slot_ref[...] = base + rank


def _slots_call(off, route):
    nt = route.shape[1]
    tile = min(nt, 2048)
    return pl.pallas_call(
        _slots_kernel,
        out_shape=jax.ShapeDtypeStruct((8, nt), jnp.int32),
        grid=(nt // tile,),
        in_specs=[pl.BlockSpec(memory_space=pltpu.SMEM), pl.BlockSpec((8, tile), lambda i: (0, i))],
        out_specs=pl.BlockSpec((8, tile), lambda i: (0, i)),
        compiler_params=_cparams(("arbitrary",)),
        name="slots",
    )(off, route)


def _dispatch_kernel(slot_ref, h2_ref, *rest, fill, n_rows):
    if fill:
        xs_ref, zero_scr, sem = rest
    else:
        _, xs_ref, sem = rest
    i = pl.program_id(0)

    if fill:
        @pl.when(i == 0)
        def _():
            zero_scr[...] = jnp.zeros_like(zero_scr)
            nblk = n_rows // MOE_ROWS

            def fill_copy(b):
                return pltpu.make_async_copy(zero_scr, xs_ref.at[pl.ds(b * MOE_ROWS, MOE_ROWS)], sem)

            def start(b, c):
                fill_copy(b).start()
                return c

            def wait(b, c):
                fill_copy(b).wait()
                return c

            lax.fori_loop(0, nblk, start, 0)
            lax.fori_loop(0, nblk, wait, 0)

    def start(t, c):
        for k in range(2):
            pltpu.make_async_copy(h2_ref.at[pl.ds(t, 1)], xs_ref.at[pl.ds(slot_ref[k, t], 1)], sem).start()
        return c

    lax.fori_loop(0, TOK_TILE, start, 0, unroll=8)
    for _ in range(2):
        pltpu.make_async_copy(h2_ref, xs_ref.at[pl.ds(0, TOK_TILE)], sem).wait()


def _route_spec(shift=0, last=None):
    if shift == 0:
        return pl.BlockSpec((8, TOK_TILE), lambda i: (0, i), memory_space=pltpu.SMEM)
    return pl.BlockSpec((8, TOK_TILE), lambda i: (0, jnp.minimum(i + shift, last)), memory_space=pltpu.SMEM)


def _dispatch_call(slots, h2, xs, n_rows):
    nt = h2.shape[0]
    fill = xs is None
    kern = functools.partial(_dispatch_kernel, fill=fill, n_rows=n_rows)
    in_specs = [_route_spec(), pl.BlockSpec((TOK_TILE, D_MODEL), lambda i: (i, 0))]
    args = [slots, h2]
    scratch = [pltpu.SemaphoreType.DMA(())]
    aliases = {}
    if fill:
        scratch = [pltpu.VMEM((MOE_ROWS, D_MODEL), F32)] + scratch
    else:
        in_specs.append(pl.BlockSpec(memory_space=pl.ANY))
        args.append(xs)
        aliases = {2: 0}
    return pl.pallas_call(
        kern,
        out_shape=jax.ShapeDtypeStruct((n_rows, D_MODEL), F32),
        grid=(nt // TOK_TILE,),
        in_specs=in_specs,
        out_specs=pl.BlockSpec(memory_space=pl.ANY),
        scratch_shapes=scratch,
        input_output_aliases=aliases,
        compiler_params=_cparams(("arbitrary",)),
        name="dispatch_fill" if fill else "dispatch",
    )(*args)


def _experts_kernel(be_ref, nu_ref, xs_ref, wg_ref, wu_ref, wd_ref, ys_ref, wg_bf, wu_bf, wd_bf):
    b = pl.program_id(0)
    used = b < nu_ref[0]

    @pl.when(used)
    def _():
        prev = be_ref[jnp.maximum(b - 1, 0)]

        @pl.when((b == 0) | (be_ref[b] != prev))
        def _():
            wg_bf[...] = wg_ref[0].astype(BF16)
            wu_bf[...] = wu_ref[0].astype(BF16)
            wd_bf[...] = wd_ref[0].astype(BF16)

        x = xs_ref[...].astype(BF16)
        g = jnp.dot(x, wg_bf[...], preferred_element_type=F32)
        u = jnp.dot(x, wu_bf[...], preferred_element_type=F32)
        h = (_silu(g) * u).astype(BF16)
        ys_ref[...] = jnp.dot(h, wd_bf[...], preferred_element_type=F32)

    @pl.when(jnp.logical_not(used))
    def _():
        ys_ref[...] = jnp.zeros_like(ys_ref)


def _experts_call(blk_e, n_used, xs, wg, wu, wd):
    n_rows = xs.shape[0]
    nb = n_rows // MOE_ROWS
    grid_spec = pltpu.PrefetchScalarGridSpec(
        num_scalar_prefetch=2,
        grid=(nb,),
        in_specs=[pl.BlockSpec((MOE_ROWS, D_MODEL), lambda b, be, nu: (jnp.minimum(b, nu[0] - 1), 0)),
                  pl.BlockSpec((1, D_MODEL, D_EXPERT), lambda b, be, nu: (be[b], 0, 0)),
                  pl.BlockSpec((1, D_MODEL, D_EXPERT), lambda b, be, nu: (be[b], 0, 0)),
                  pl.BlockSpec((1, D_EXPERT, D_MODEL), lambda b, be, nu: (be[b], 0, 0))],
        out_specs=pl.BlockSpec((MOE_ROWS, D_MODEL), lambda b, be, nu: (b, 0)),
        scratch_shapes=[pltpu.VMEM((D_MODEL, D_EXPERT), BF16), pltpu.VMEM((D_MODEL, D_EXPERT), BF16),
                        pltpu.VMEM((D_EXPERT, D_MODEL), BF16)],
    )
    return pl.pallas_call(
        _experts_kernel,
        out_shape=jax.ShapeDtypeStruct((n_rows, D_MODEL), F32),
        grid_spec=grid_spec,
        compiler_params=_cparams(("arbitrary",)),
        name="experts",
    )(blk_e, n_used, xs, wg, wu, wd)


def _combine_kernel(slot_ref, slot_next_ref, x1_ref, gate_ref, mod_ref, fg_ref, ys_ref, y_ref, buf, sems):
    i = pl.program_id(0)
    n = pl.num_programs(0)
    slot = i % 2

    def gather(slots, s):
        def start(t, c):
            for k in range(2):
                pltpu.make_async_copy(ys_ref.at[pl.ds(slots[k, t], 1)], buf.at[s, k, pl.ds(t, 1)], sems.at[s]).start()
            return c

        lax.fori_loop(0, TOK_TILE, start, 0, unroll=8)

    @pl.when(i == 0)
    def _():
        gather(slot_ref, 0)

    @pl.when(i + 1 < n)
    def _():
        gather(slot_next_ref, 1 - slot)

    for k in range(2):
        pltpu.make_async_copy(ys_ref.at[pl.ds(0, TOK_TILE)], buf.at[slot, k], sems.at[slot]).wait()

    gate = gate_ref[...]
    lane = _iota(gate.shape, 1)
    w0 = jnp.sum(jnp.where(lane == 0, gate, 0.0), axis=-1, keepdims=True)
    w1 = jnp.sum(jnp.where(lane == 1, gate, 0.0), axis=-1, keepdims=True)
    g2 = mod_ref[0][:, 5 * D_MODEL:6 * D_MODEL]
    y = x1_ref[...] + g2 * (w0 * buf[slot, 0] + w1 * buf[slot, 1])
    ms = jnp.mean(y * y, axis=-1, keepdims=True)
    y_ref[...] = y * lax.rsqrt(ms + NORM_EPS) * fg_ref[...]


def _combine_call(slots, x1, rgate, mod3, mod_row, fg, ys):
    nt = x1.shape[0]
    n_tiles = nt // TOK_TILE
    return pl.pallas_call(
        _combine_kernel,
        out_shape=jax.ShapeDtypeStruct((nt, D_MODEL), F32),
        grid=(n_tiles,),
        in_specs=[_route_spec(), _route_spec(1, n_tiles - 1),
                  pl.BlockSpec((TOK_TILE, D_MODEL), lambda i: (i, 0)),
                  pl.BlockSpec((TOK_TILE, LANES), lambda i: (i, 0)),
                  pl.BlockSpec((1, 1, 6 * D_MODEL), lambda i: (mod_row(i), 0, 0)),
                  pl.BlockSpec((1, D_MODEL), lambda i: (0, 0)),
                  pl.BlockSpec(memory_space=pl.ANY)],
        out_specs=pl.BlockSpec((TOK_TILE, D_MODEL), lambda i: (i, 0)),
        scratch_shapes=[pltpu.VMEM((2, 2, TOK_TILE, D_MODEL), F32), pltpu.SemaphoreType.DMA((2,))],
        compiler_params=_cparams(("arbitrary",)),
        name="combine",
    )(slots, slots, x1, rgate, mod3, fg, ys)


def kernel(x_prompt, x_sample, state_rwkv, state_gdn, c, c_ctx, ada_w, ada_b, norm1_g, norm2_g, w_in, w_out,
           rwkv_w0, rwkv_w2, rwkv_a0, rwkv_a2, rwkv_g2, rwkv_k_k, rwkv_k_a, rwkv_r_k, rwkv_lnx_g, rwkv_lnx_b,
           gdn_conv, gdn_a_log, gdn_dt_bias, gdn_norm_g, router_group_w, router_group_b, router_expert_w,
           router_expert_b, expert_gate, expert_up, expert_down, final_norm_g):
    assert ada_w.shape[0] == 1, "one layer"
    bp, tp, _ = x_prompt.shape
    bs, ts, _ = x_sample.shape
    lp = dict(rwkv_w0=rwkv_w0[0], rwkv_w2=rwkv_w2[0], rwkv_a0=rwkv_a0[0], rwkv_a2=rwkv_a2[0], rwkv_g2=rwkv_g2[0],
              rwkv_k_k=rwkv_k_k[0], rwkv_k_a=rwkv_k_a[0], rwkv_r_k=rwkv_r_k[0], rwkv_lnx_g=rwkv_lnx_g[0],
              rwkv_lnx_b=rwkv_lnx_b[0], gdn_conv=gdn_conv[0], gdn_a_log=gdn_a_log[0],
              gdn_dt_bias=gdn_dt_bias[0], gdn_norm_g=gdn_norm_g[0])
    rw_w = _rwkv_weights(lp)
    gd_w = _gdn_weights(lp)

    n_cond = 1 + bs
    cond = jnp.concatenate([c_ctx[None, :], c, jnp.zeros((16 - n_cond, D_MODEL), F32)], axis=0)
    mod = _mod_call(cond, ada_w[0], ada_b)
    mod3 = mod.reshape(16, 1, 6 * D_MODEL)

    in_cols = w_in.shape[2]
    w_in_bf = jnp.pad(w_in[0], ((0, 0), (0, RW + GW_PAD - in_cols))).astype(BF16)
    w_out_bf = w_out[0].astype(BF16)
    n1g = norm1_g.reshape(1, D_MODEL)
    n2g = norm2_g.reshape(1, D_MODEL)
    fg = final_norm_g.reshape(1, D_MODEL)
    rw = jnp.zeros((D_MODEL, LANES), F32).at[:, :N_GROUPS].set(router_group_w[0])
    rw = rw.at[:, N_GROUPS:N_GROUPS + N_EXPERTS].set(router_expert_w[0])
    rb = jnp.zeros((1, LANES), F32).at[0, :N_GROUPS].set(router_group_b[0])
    rb = rb.at[0, N_GROUPS:N_GROUPS + N_EXPERTS].set(router_expert_b[0])

    tiles_per_sample = ts // TOK_TILE
    passes = [
        dict(x=x_prompt.reshape(bp * tp, D_MODEL), batch=bp, seq=tp, row_len=tp,
             mod_row=lambda i: 0,
             s_r=None, s_g=None),
        dict(x=x_sample.reshape(bs * ts, D_MODEL), batch=bs, seq=ts, row_len=GRID_W,
             mod_row=lambda i: 1 + i // tiles_per_sample,
             s_r=state_rwkv[:, 0], s_g=state_gdn[:, 0]),
    ]

    for p in passes:
        proj_r, proj_g = _inproj_call(p["x"], mod3, p["mod_row"], n1g, w_in_bf)
        o_r, p["st_r"] = _rwkv_call(proj_r, p["s_r"], rw_w, batch=p["batch"], seq=p["seq"], nrow=SEQS_PER_STEP,
                                    hp=False)
        nrow_g = GDN_SEQS_PER_STEP_SHORT if p["seq"] <= 256 else SEQS_PER_STEP
        o_g, p["st_g"] = _gdn_call(proj_g, p["s_g"], gd_w, batch=p["batch"], seq=p["seq"], nrow=nrow_g,
                                   row_len=p["row_len"], hp=False)
        p["x1"], p["h2"], p["rid"], p["rgate"] = _outproj_call(
            p["x"], o_r, o_g, mod3, p["mod_row"], w_out_bf, n2g, rw, rb)

    cnt = jnp.zeros((8, LANES), jnp.int32)
    for p in passes:
        p["route"], cnt = _plan_call(p["rid"], cnt)
    counts = cnt[0, :N_EXPERTS]
    padded = (counts + MOE_ROWS - 1) // MOE_ROWS * MOE_ROWS
    ends = jnp.cumsum(padded)
    off = (ends - padded).astype(jnp.int32)
    nt_all = sum(p["rid"].shape[0] for p in passes)
    n_blocks = (2 * nt_all) // MOE_ROWS + N_EXPERTS
    n_rows = n_blocks * MOE_ROWS
    n_used = (ends[-1] // MOE_ROWS).astype(jnp.int32)
    blk_start = jnp.minimum(jnp.arange(n_blocks, dtype=jnp.int32), n_used - 1) * MOE_ROWS
    blk_e = jnp.minimum(jnp.sum((blk_start[:, None] >= ends[None, :]).astype(jnp.int32), axis=1), N_EXPERTS - 1)

    xs = None
    for p in passes:
        p["slots"] = _slots_call(off, p["route"])
        xs = _dispatch_call(p["slots"], p["h2"], xs, n_rows)
    ys = _experts_call(blk_e, n_used.reshape(1), xs, expert_gate[0], expert_up[0], expert_down[0])
    outs = [_combine_call(p["slots"], p["x1"], p["rgate"], mod3, p["mod_row"], fg, ys) for p in passes]

    y_prompt = outs[0].reshape(bp, tp, D_MODEL)
    y_sample = outs[1].reshape(bs, ts, D_MODEL)
    new_state_rwkv = passes[0]["st_r"][:, None]
    new_state_gdn = passes[0]["st_g"][:, None]
    return (y_prompt, y_sample, new_state_rwkv, new_state_gdn)
```

```python
import functools
import math

import jax
import jax.numpy as jnp
from jax import lax
from jax.experimental import pallas as pl
from jax.experimental.pallas import tpu as pltpu

F32 = jnp.float32
BF16 = jnp.bfloat16

D_MODEL = 1024
R_HEADS, R_HEAD = 8, 64
R_WIDTH = R_HEADS * R_HEAD
G_HEADS, G_HEAD = 4, 128
G_WIDTH = G_HEADS * G_HEAD
LNX_EPS = 64e-5
NORM_EPS = 1e-6
N_GROUPS, EXPERTS_PER_GROUP = 4, 8
N_EXPERTS = N_GROUPS * EXPERTS_PER_GROUP
D_EXPERT = 512
GRID_W = 64

CHUNK = 64
TOK_TILE = 256
MOE_ROWS = 256
SEQS_PER_STEP = 2
GDN_SEQS_PER_STEP_SHORT = 4
LANES = 128
RW = 3 * R_WIDTH + 256
GW_PAD = 4 * G_WIDTH + LANES
VMEM_LIMIT = 56 * 1024 * 1024


def _cparams(sem):
    return pltpu.CompilerParams(dimension_semantics=sem, vmem_limit_bytes=VMEM_LIMIT)


def _sigmoid(x):
    return 1.0 / (1.0 + jnp.exp(-x))


def _silu(x):
    return x * _sigmoid(x)


def _softplus(x):
    return jnp.maximum(x, 0.0) + jnp.log(1.0 + jnp.exp(-jnp.abs(x)))


def _dg(a, b, ca, cb):
    return lax.dot_general(a, b, (((ca,), (cb,)), ((), ())), preferred_element_type=F32)


def _split2(x):
    hi = x.astype(BF16)
    lo = (x - hi.astype(F32)).astype(BF16)
    return hi, lo


def _split3(x):
    h1 = x.astype(BF16)
    r1 = x - h1.astype(F32)
    h2 = r1.astype(BF16)
    h3 = (r1 - h2.astype(F32)).astype(BF16)
    return h1, h2, h3


def _mm(a, b, ca=1, cb=0, hp=False):
    if not hp:
        return _dg(a.astype(BF16), b.astype(BF16), ca, cb)
    ah, al = _split2(a)
    bh, bl = _split2(b)
    return _dg(ah, bh, ca, cb) + (_dg(ah, bl, ca, cb) + _dg(al, bh, ca, cb))


def _mm_exact_lhs(m_bf, x, ca=1, cb=0, parts=3):
    if parts == 2:
        hi, lo = _split2(x)
        return _dg(m_bf, hi, ca, cb) + _dg(m_bf, lo, ca, cb)
    h1, h2, h3 = _split3(x)
    return _dg(m_bf, h1, ca, cb) + (_dg(m_bf, h2, ca, cb) + _dg(m_bf, h3, ca, cb))


def _mm_exact_rhs(x, m_bf):
    h1, h2, h3 = _split3(x)
    return _dg(h1, m_bf, 1, 0) + (_dg(h2, m_bf, 1, 0) + _dg(h3, m_bf, 1, 0))


def _group_sum(x, ones_bf):
    w = ones_bf.shape[0]
    hi, lo = _split2(x)
    return jnp.concatenate([_dg(hi[:, j:j + w], ones_bf, 1, 0) + _dg(lo[:, j:j + w], ones_bf, 1, 0)
                            for j in range(0, x.shape[1], w)], axis=1)


def _iota(shape, axis):
    return lax.broadcasted_iota(jnp.int32, shape, axis)


def _tri_masks(n, rev):
    t = _iota((n, n), 0)
    s = _iota((n, n), 1)
    if rev:
        return t < s, t <= s
    return t > s, t >= s


PACK = 4


def _level_masks(n):
    t = _iota((n, PACK * n), 0)
    u = _iota((n, PACK * n), 1) % n
    levels = []
    s = 1
    while s < n:
        levels.append((t // (2 * s) == u // (2 * s)) & (t // s != u // s))
        s *= 2
    diag = _iota((PACK * n, PACK * n), 0) // n == _iota((PACK * n, PACK * n), 1) // n
    return levels, diag


def _block_diag(x, diag):
    return jnp.where(diag, jnp.concatenate([x] * PACK, axis=0), 0.0).astype(BF16)


def _unit_tri_inverse(ms, masks):
    levels, diag = masks
    n = ms[0].shape[0]
    eye = (_iota((n, PACK * n), 0) == _iota((n, PACK * n), 1) % n).astype(F32)
    xs = [eye - jnp.where(levels[0], m, 0.0) for m in ms]
    for mask in levels[1:]:
        zs = [_dg(jnp.where(mask, m, 0.0).astype(BF16), _block_diag(x, diag), 1, 0) for m, x in zip(ms, xs)]
        xs = [x - _dg(x.astype(BF16), _block_diag(z, diag), 1, 0) for x, z in zip(xs, zs)]
    return xs


def _mod_kernel(c_ref, w_ref, b_ref, o_ref):
    s = _silu(c_ref[...])
    o_ref[...] = _mm(s, w_ref[...], hp=True) + b_ref[...]


def _mod_call(cond, ada_w, ada_b):
    rows = cond.shape[0]
    n = ada_w.shape[1]
    bn = 1024
    return pl.pallas_call(
        _mod_kernel,
        out_shape=jax.ShapeDtypeStruct((rows, n), F32),
        grid=(n // bn,),
        in_specs=[pl.BlockSpec((rows, D_MODEL), lambda j: (0, 0)),
                  pl.BlockSpec((D_MODEL, bn), lambda j: (0, j)),
                  pl.BlockSpec((1, bn), lambda j: (0, j))],
        out_specs=pl.BlockSpec((rows, bn), lambda j: (0, j)),
        compiler_params=_cparams(("arbitrary",)),
        name="mod",
    )(cond, ada_w, ada_b)


def _inproj_kernel(x_ref, mod_ref, g_ref, w_ref, or_ref, og_ref):
    x = x_ref[...]
    ms = jnp.mean(x * x, axis=-1, keepdims=True)
    y = x * lax.rsqrt(ms + NORM_EPS) * g_ref[...]
    m = mod_ref[0]
    h = y * (1.0 + m[:, D_MODEL:2 * D_MODEL]) + m[:, :D_MODEL]
    p = jnp.dot(h.astype(BF16), w_ref[...], preferred_element_type=F32)
    or_ref[...] = p[:, :RW].astype(BF16)
    og_ref[...] = p[:, RW:].astype(BF16)


def _inproj_call(x2, mod3, mod_row, norm_g, w_in_bf):
    nt = x2.shape[0]
    ncol = w_in_bf.shape[1]
    return pl.pallas_call(
        _inproj_kernel,
        out_shape=(jax.ShapeDtypeStruct((nt, RW), BF16),
                   jax.ShapeDtypeStruct((nt, GW_PAD), BF16)),
        grid=(nt // TOK_TILE,),
        in_specs=[pl.BlockSpec((TOK_TILE, D_MODEL), lambda i: (i, 0)),
                  pl.BlockSpec((1, 1, 6 * D_MODEL), lambda i: (mod_row(i), 0, 0)),
                  pl.BlockSpec((1, D_MODEL), lambda i: (0, 0)),
                  pl.BlockSpec((D_MODEL, ncol), lambda i: (0, 0))],
        out_specs=(pl.BlockSpec((TOK_TILE, RW), lambda i: (i, 0)),
                   pl.BlockSpec((TOK_TILE, GW_PAD), lambda i: (i, 0))),
        compiler_params=_cparams(("arbitrary",)),
        name="inproj",
    )(x2, mod3, norm_g, w_in_bf)


def _head_blocks(width, head):
    return (_iota((width, width), 0) // head == _iota((width, width), 1) // head).astype(BF16)


def _rwkv_kernel(p_ref, *refs, seq, nrow, hp, has_s0):
    s0_ref = refs[0] if has_s0 else None
    (w0_ref, wd_ref, a0_ref, wa_ref, g2_ref, kk_ref, ka_ref, rk_ref, lng_ref, lnb_ref,
     o_ref, st_ref, y_scr) = refs[1:] if has_s0 else refs
    C = CHUNK
    nc = seq // C
    blk = _head_blocks(2 * LANES, R_HEAD)
    k_k = kk_ref[...]
    k_a = ka_ref[...]
    r_k = rk_ref[...]
    exp_m05 = math.exp(-0.5)
    inv_masks = _level_masks(C)

    def lora_lhs(la):
        return jnp.where(_iota(la.shape, 1) < 64, jnp.tanh(la), la)

    def cols(n, rows, lo, hi):
        return p_ref[n, rows, lo:hi].astype(F32)

    assert R_HEAD == C
    PW = PACK * R_HEAD
    diag = inv_masks[1]
    pk_t = _iota((C, PW), 0)
    pk_s = _iota((C, PW), 1) % C
    dirs = []
    for d, rev in ((0, False), (1, True)):
        dirs.append(dict(d=d, rev=rev, tri=_tri_masks(C, rev)[1].astype(BF16),
                         strict=(pk_t < pk_s) if rev else (pk_t > pk_s),
                         incl=(pk_t <= pk_s) if rev else (pk_t >= pk_s)))
    hd = range(R_HEADS)
    sls = [slice(h * R_HEAD, (h + 1) * R_HEAD) for h in hd]

    def prep(n, dr, c):
        d, rev, tri = dr["d"], dr["rev"], dr["tri"]
        rows = pl.ds(pl.multiple_of(c * C, C), C)
        rr = cols(n, rows, 0, R_WIDTH)
        kx = cols(n, rows, R_WIDTH, 2 * R_WIDTH)
        vv = cols(n, rows, 2 * R_WIDTH, 3 * R_WIDTH)
        lhs = lora_lhs(cols(n, rows, 3 * R_WIDTH, 3 * R_WIDTH + LANES))
        logw = -exp_m05 * _sigmoid(w0_ref[d] + _mm(lhs, wd_ref[d]))
        a = _sigmoid(a0_ref[d] + _mm(lhs, wa_ref[d]))
        kkr = kx * k_k
        kk = kkr * lax.rsqrt(_group_sum(kkr * kkr, blk) + 1e-6)
        kd = kx * (1.0 + (a - 1.0) * k_a)
        bvec = kk * a
        gi = _mm_exact_lhs(tri, logw, parts=2)
        gtot = gi[0:1] if rev else gi[C - 1:C]
        en = jnp.exp(-gi)
        ee = jnp.exp(gtot - gi)
        return dict(rows=rows, vv=vv, rt=rr * jnp.exp(gi), at=-kk * jnp.exp(gi - logw), bt=bvec * en, kt=kd * en,
                    bh=bvec * ee, kh=kd * ee, gc=jnp.exp(gtot))

    seqs = range(nrow)
    packs = [(n, d, g) for n in seqs for d in (0, 1) for g in range(R_HEADS // PACK)]

    def step(i, carry):
        ops = {(n, d): prep(n, dirs[d], (nc - 1 - i) if d else i) for n in seqs for d in (0, 1)}
        pcol = lambda name, n, d, g: ops[n, d][name][:, g * PW:(g + 1) * PW]
        bd = lambda x: _block_diag(x, diag)
        ars = {k: jnp.concatenate([pcol("at", *k), pcol("rt", *k)], axis=0).astype(BF16) for k in packs}
        pb = {k: _dg(ars[k], bd(pcol("bt", *k)), 1, 1) for k in packs}
        pk = {k: _dg(ars[k], bd(pcol("kt", *k)), 1, 1) for k in packs}
        sts = {(n, d, g): carry[2 * n + d][:, g * PW:(g + 1) * PW] for n, d, g in packs}
        x0s = {k: _dg(ars[k], bd(sts[k]), 1, 1) for k in packs}
        bdv = {k: bd(pcol("vv", *k)) for k in packs}
        akv = {(n, d, g): _dg(jnp.where(dirs[d]["strict"], pk[n, d, g][:C], 0.0).astype(BF16), bdv[n, d, g], 1, 0)
               for n, d, g in packs}
        tms = dict(zip(packs, _unit_tri_inverse(
            [jnp.where(dirs[d]["strict"], -pb[n, d, g][:C], 0.0) for n, d, g in packs], inv_masks)))
        us = {k: _dg(tms[k].astype(BF16), bd(x0s[k][:C] + akv[k]), 1, 0) for k in packs}
        ys = {(n, d, g): x0s[n, d, g][C:] + _dg(
            jnp.concatenate([jnp.where(dirs[d]["incl"], pb[n, d, g][C:], 0.0),
                             jnp.where(dirs[d]["incl"], pk[n, d, g][C:], 0.0)], axis=1).astype(BF16),
            jnp.concatenate([bd(us[n, d, g]), bdv[n, d, g]], axis=0), 1, 0) for n, d, g in packs}
        snew = {}
        for n, d, g in packs:
            full = _dg(jnp.concatenate([us[n, d, g], pcol("vv", n, d, g)], axis=0).astype(BF16),
                       jnp.concatenate([pcol("bh", n, d, g), pcol("kh", n, d, g)], axis=0).astype(BF16), 0, 0)
            own = jnp.where(diag, full, 0.0)
            upd = own[0:R_HEAD]
            for j in range(1, PACK):
                upd = upd + own[j * R_HEAD:(j + 1) * R_HEAD]
            snew[n, d, g] = sts[n, d, g] * pcol("gc", n, d, g) + upd
        for n, d, g in packs:
            y_scr[n, d, ops[n, d]["rows"], g * PW:(g + 1) * PW] = ys[n, d, g]
        return tuple(jnp.concatenate([snew[n, d, g] for g in range(R_HEADS // PACK)], axis=1)
                     for n in seqs for d in (0, 1))

    def initial(n, d):
        if not has_s0:
            return jnp.zeros((R_HEAD, R_WIDTH), F32)
        return jnp.concatenate([s0_ref[n, d, h] for h in hd], axis=1)

    s_fin = lax.fori_loop(0, nc, step, tuple(initial(n, d) for n in seqs for d in (0, 1)))
    for n in seqs:
        for d in (0, 1):
            for h in hd:
                st_ref[n, d, h] = s_fin[2 * n + d][:, sls[h]]

    g2 = g2_ref[...]
    lng = lng_ref[...]
    lnb = lnb_ref[...]
    inv_n = 1.0 / R_HEAD
    fb = min(seq, 256)

    def finish(j, carry):
        rows = pl.ds(pl.multiple_of(j * fb, fb), fb)
        for n in seqs:
            y = y_scr[n, 0, rows, :] + y_scr[n, 1, rows, :]
            mu = _group_sum(y, blk) * inv_n
            yc = y - mu
            var = _group_sum(yc * yc, blk) * inv_n
            yn = yc * lax.rsqrt(var + LNX_EPS) * lng + lnb
            lhs = lora_lhs(cols(n, rows, 3 * R_WIDTH, 3 * R_WIDTH + LANES))
            a_sum = _sigmoid(a0_ref[0] + _mm(lhs, wa_ref[0])) + _sigmoid(a0_ref[1] + _mm(lhs, wa_ref[1]))
            rr = cols(n, rows, 0, R_WIDTH)
            kx = cols(n, rows, R_WIDTH, 2 * R_WIDTH)
            kd_sum = kx * (2.0 + (a_sum - 2.0) * k_a)
            bonus = _group_sum(rr * kd_sum * r_k, blk) * cols(n, rows, 2 * R_WIDTH, 3 * R_WIDTH)
            gl = cols(n, rows, 3 * R_WIDTH + LANES, 3 * R_WIDTH + 2 * LANES)
            gate = _mm(_sigmoid(gl), g2)
            o_ref[n, rows, :] = (yn + bonus) * gate
        return carry

    lax.fori_loop(0, seq // fb, finish, 0)


def _rwkv_call(proj_r, s0, w, *, batch, seq, nrow, hp):
    p3 = proj_r.reshape(batch, seq, RW)
    kern = functools.partial(_rwkv_kernel, seq=seq, nrow=nrow, hp=hp, has_s0=s0 is not None)
    full = lambda a: pl.BlockSpec(a.shape, lambda b: (0,) * a.ndim)
    ws = [w["w0"], w["wd"], w["a0"], w["wa"], w["g2"], w["k_k"], w["k_a"], w["r_k"], w["lnx_g"], w["lnx_b"]]
    state_spec = pl.BlockSpec((nrow, 2, R_HEADS, R_HEAD, R_HEAD), lambda b: (b, 0, 0, 0, 0))
    states = [] if s0 is None else [s0]
    o, st = pl.pallas_call(
        kern,
        out_shape=(jax.ShapeDtypeStruct((batch, seq, R_WIDTH), F32),
                   jax.ShapeDtypeStruct((batch, 2, R_HEADS, R_HEAD, R_HEAD), F32)),
        grid=(batch // nrow,),
        in_specs=[pl.BlockSpec((nrow, seq, RW), lambda b: (b, 0, 0))] + [state_spec] * len(states)
        + [full(a) for a in ws],
        out_specs=(pl.BlockSpec((nrow, seq, R_WIDTH), lambda b: (b, 0, 0)), state_spec),
        scratch_shapes=[pltpu.VMEM((nrow, 2, seq, R_WIDTH), F32)],
        compiler_params=_cparams(("arbitrary",)),
        name="rwkv_hp" if hp else "rwkv",
    )(p3, *states, *ws)
    return o.reshape(batch * seq, R_WIDTH), st


def _rwkv_weights(lp):
    zeros = jnp.zeros((2, 64, R_WIDTH), F32)
    row = lambda a: jnp.asarray(a, F32).reshape(1, R_WIDTH)
    return {
        "w0": jnp.asarray(lp["rwkv_w0"], F32).reshape(2, 1, R_WIDTH),
        "wd": jnp.concatenate([jnp.asarray(lp["rwkv_w2"], F32), zeros], axis=1),
        "a0": jnp.asarray(lp["rwkv_a0"], F32).reshape(2, 1, R_WIDTH),
        "wa": jnp.concatenate([zeros, jnp.asarray(lp["rwkv_a2"], F32)], axis=1),
        "g2": jnp.asarray(lp["rwkv_g2"], F32),
        "k_k": row(lp["rwkv_k_k"]), "k_a": row(lp["rwkv_k_a"]), "r_k": row(lp["rwkv_r_k"]),
        "lnx_g": row(lp["rwkv_lnx_g"]), "lnx_b": row(lp["rwkv_lnx_b"]),
    }


def _gdn_kernel(p_ref, *refs, seq, nrow, row_len, hp, has_s0):
    s0_ref = refs[0] if has_s0 else None
    (cw_ref, alog_ref, dtb_ref, ng_ref, o_ref, st_ref, q_scr, k_scr, v_scr, o_scr) = refs[1:] if has_s0 else refs
    C = CHUNK
    nc = seq // C
    fb = min(seq, 256)
    W = G_WIDTH
    q_scale = G_HEAD ** -0.5
    seqs = range(nrow)
    hd = range(G_HEADS)
    hss = [slice(h * G_HEAD, (h + 1) * G_HEAD) for h in hd]

    cw0 = cw_ref[0:1, :]
    cw1 = cw_ref[1:2, :]
    cw2 = cw_ref[2:3, :]
    tpos = _iota((fb, 1), 0) % row_len
    is_first = tpos == 0
    is_last = tpos == row_len - 1

    def prep(j, carry):
        rows = pl.ds(pl.multiple_of(j * fb, fb), fb)
        for n in seqs:
            x = p_ref[n, rows, 0:3 * W].astype(F32)
            xm = jnp.where(is_first, 0.0, pltpu.roll(x, 1, 0))
            xp = jnp.where(is_last, 0.0, pltpu.roll(x, fb - 1, 0))
            y = _silu(cw0 * xm + cw1 * x + cw2 * xp)
            for h in hd:
                qh = y[:, hss[h]]
                kh = y[:, W + h * G_HEAD:W + (h + 1) * G_HEAD]
                q_scr[n, rows, hss[h]] = qh * lax.rsqrt(jnp.sum(qh * qh, axis=-1, keepdims=True) + 1e-6) * q_scale
                k_scr[n, rows, hss[h]] = kh * lax.rsqrt(jnp.sum(kh * kh, axis=-1, keepdims=True) + 1e-6)
            v_scr[n, rows, :] = y[:, 2 * W:3 * W]
        return carry

    lax.fori_loop(0, seq // fb, prep, 0)

    assert G_HEADS == PACK
    inv_masks = _level_masks(C)
    alog = alog_ref[...]
    dtb = dtb_ref[...]
    sel_row = _iota((LANES, W), 0)
    sel_head = _iota((LANES, W), 1) // G_HEAD
    pk_row = _iota((LANES, PACK * C), 0)
    pk_head = _iota((LANES, PACK * C), 1) // C
    pk_t = _iota((C, PACK * C), 0)
    pk_s = _iota((C, PACK * C), 1) % C

    dirs = []
    for d, rev in ((0, False), (1, True)):
        dirs.append(dict(
            d=d, rev=rev, tri=_tri_masks(C, rev)[1].astype(BF16),
            strict=(pk_t < pk_s) if rev else (pk_t > pk_s),
            incl=(pk_t <= pk_s) if rev else (pk_t >= pk_s),
            upto=(pk_t >= pk_s) if rev else (pk_t <= pk_s),
            e_beta=(sel_row == 4 * d + sel_head).astype(BF16),
            e_g=(sel_row == 8 + 4 * d + sel_head).astype(BF16),
            e_g_pk=(pk_row == 8 + 4 * d + pk_head).astype(BF16)))
    items = [(n, d, h) for n in seqs for d in (0, 1) for h in hd]
    packs = [(n, d) for n in seqs for d in (0, 1)]

    def prep_dir(n, dr, c):
        rows = pl.ds(pl.multiple_of(c * C, C), C)
        ba = p_ref[n, rows, 4 * W:4 * W + LANES].astype(F32)
        gval = -jnp.exp(alog) * _softplus(ba + dtb)
        beta_b = _mm_exact_rhs(_sigmoid(ba), dr["e_beta"])
        gcum = _mm_exact_lhs(dr["tri"], _mm_exact_rhs(gval, dr["e_g"]))
        g_pk = _mm_exact_rhs(gval, dr["e_g_pk"])
        g_col = _mm_exact_lhs(dr["tri"], g_pk)
        g_row = jnp.sum(jnp.where(dr["upto"], g_pk, 0.0), axis=0, keepdims=True)
        return dict(rows=rows, beta=beta_b, gcum=gcum,
                    decay=jnp.where(dr["incl"], jnp.exp(jnp.where(dr["incl"], g_col - g_row, 0.0)), 0.0),
                    q=q_scr[n, rows, :], k=k_scr[n, rows, :], v=v_scr[n, rows, :])

    def step(i, carry):
        ops = {(n, d): prep_dir(n, dirs[d], (nc - 1 - i) if d else i) for n, d in packs}
        hcol = lambda name, n, d, h: ops[n, d][name][:, hss[h]]
        qhs = {k: hcol("q", *k) for k in items}
        khs = {k: hcol("k", *k) for k in items}
        gcs = {k: hcol("gcum", *k) for k in items}
        bbs = {k: hcol("beta", *k) for k in items}
        kbs = {k: khs[k] * bbs[k] for k in items}
        kks = {k: _mm(jnp.concatenate([kbs[k], qhs[k]], axis=0), khs[k], 1, 1, hp=hp) for k in items}
        kka = {(n, d): jnp.concatenate([kks[n, d, h][:C] for h in hd], axis=1) for n, d in packs}
        tcat = dict(zip(packs, _unit_tri_inverse(
            [jnp.where(dirs[d]["strict"], kka[n, d] * ops[n, d]["decay"], 0.0) for n, d in packs], inv_masks)))
        egs = {k: jnp.exp(gcs[k]) for k in items}
        uws = {(n, d, h): _mm(tcat[n, d][:, h * C:(h + 1) * C],
                              jnp.concatenate([hcol("v", n, d, h) * bbs[n, d, h], kbs[n, d, h] * egs[n, d, h]], axis=1),
                              hp=hp) for n, d, h in items}
        shs = {(n, d, h): carry[2 * n + d][:, hss[h]] for n, d, h in items}
        wss = {k: _mm(jnp.concatenate([uws[k][:, G_HEAD:], qhs[k] * egs[k]], axis=0), shs[k], hp=hp) for k in items}
        vns = {k: uws[k][:, :G_HEAD] - wss[k][:C] for k in items}
        outs = {(n, d, h): wss[n, d, h][C:]
                + _mm(kks[n, d, h][C:] * ops[n, d]["decay"][:, h * C:(h + 1) * C], vns[n, d, h]) for n, d, h in items}
        snew = {}
        for n, d, h in items:
            glast = gcs[n, d, h][0:1] if dirs[d]["rev"] else gcs[n, d, h][C - 1:C]
            k_dec = khs[n, d, h] * jnp.exp(glast - gcs[n, d, h])
            snew[n, d, h] = shs[n, d, h] * jnp.exp(glast) + _mm(k_dec, vns[n, d, h], 0, 0, hp=hp)
        for n, d in packs:
            o_scr[n, d, ops[n, d]["rows"], :] = jnp.concatenate([outs[n, d, h] for h in hd], axis=1)
        return tuple(jnp.concatenate([snew[n, d, h] for h in hd], axis=1) for n, d in packs)

    def initial(n, d):
        if not has_s0:
            return jnp.zeros((G_HEAD, W), F32)
        return jnp.concatenate([s0_ref[n, d, h] for h in hd], axis=1)

    s_fin = lax.fori_loop(0, nc, step, tuple(initial(n, d) for n, d in packs))
    for j, (n, d) in enumerate(packs):
        for h in hd:
            st_ref[n, d, h] = s_fin[j][:, hss[h]]

    ng = ng_ref[...]

    def finish(j, carry):
        rows = pl.ds(pl.multiple_of(j * fb, fb), fb)
        for n in seqs:
            o = o_scr[n, 0, rows, :] + o_scr[n, 1, rows, :]
            z = p_ref[n, rows, 3 * W:4 * W].astype(F32)
            for h in hd:
                oh = o[:, hss[h]]
                ms = jnp.mean(oh * oh, axis=-1, keepdims=True)
                o_ref[n, rows, hss[h]] = oh * lax.rsqrt(ms + NORM_EPS) * ng * _silu(z[:, hss[h]])
        return carry

    lax.fori_loop(0, seq // fb, finish, 0)


def _gdn_call(proj_g, s0, w, *, batch, seq, nrow, row_len, hp):
    p3 = proj_g.reshape(batch, seq, GW_PAD)
    kern = functools.partial(_gdn_kernel, seq=seq, nrow=nrow, row_len=row_len, hp=hp, has_s0=s0 is not None)
    full = lambda a: pl.BlockSpec(a.shape, lambda b: (0,) * a.ndim)
    ws = [w["conv"], w["alog"], w["dtb"], w["norm_g"]]
    state_spec = pl.BlockSpec((nrow, 2, G_HEADS, G_HEAD, G_HEAD), lambda b: (b, 0, 0, 0, 0))
    states = [] if s0 is None else [s0]
    o, st = pl.pallas_call(
        kern,
        out_shape=(jax.ShapeDtypeStruct((batch, seq, G_WIDTH), F32),
                   jax.ShapeDtypeStruct((batch, 2, G_HEADS, G_HEAD, G_HEAD), F32)),
        grid=(batch // nrow,),
        in_specs=[pl.BlockSpec((nrow, seq, GW_PAD), lambda b: (b, 0, 0))] + [state_spec] * len(states)
        + [full(a) for a in ws],
        out_specs=(pl.BlockSpec((nrow, seq, G_WIDTH), lambda b: (b, 0, 0)), state_spec),
        scratch_shapes=[pltpu.VMEM((nrow, seq, G_WIDTH), F32)] * 3 + [pltpu.VMEM((nrow, 2, seq, G_WIDTH), F32)],
        compiler_params=_cparams(("arbitrary",)),
        name="gdn_hp" if hp else "gdn",
    )(p3, *states, *ws)
    return o.reshape(batch * seq, G_WIDTH), st


def _gdn_weights(lp):
    lanes = jnp.zeros((1, LANES), F32)
    return {
        "conv": jnp.asarray(lp["gdn_conv"], F32),
        "alog": lanes.at[0, 8:16].set(jnp.asarray(lp["gdn_a_log"], F32).reshape(8)),
        "dtb": lanes.at[0, 8:16].set(jnp.asarray(lp["gdn_dt_bias"], F32).reshape(8)),
        "norm_g": jnp.asarray(lp["gdn_norm_g"], F32).reshape(1, G_HEAD),
    }


def _outproj_kernel(x_ref, or_ref, og_ref, mod_ref, wo_ref, n2_ref, rw_ref, rb_ref,
                    x1_ref, h2_ref, rid_ref, rgate_ref):
    m = mod_ref[0]
    g1 = m[:, 2 * D_MODEL:3 * D_MODEL]
    sh2 = m[:, 3 * D_MODEL:4 * D_MODEL]
    sc2 = m[:, 4 * D_MODEL:5 * D_MODEL]
    mix = (jnp.dot(or_ref[...].astype(BF16), wo_ref[0:R_WIDTH, :], preferred_element_type=F32)
           + jnp.dot(og_ref[...].astype(BF16), wo_ref[R_WIDTH:, :], preferred_element_type=F32))
    x1 = x_ref[...] + g1 * mix
    x1_ref[...] = x1
    ms = jnp.mean(x1 * x1, axis=-1, keepdims=True)
    h2 = x1 * lax.rsqrt(ms + NORM_EPS) * n2_ref[...] * (1.0 + sc2) + sh2
    h2_ref[...] = h2

    logits = _mm(h2, rw_ref[...], hp=True) + rb_ref[...]
    lane = _iota(logits.shape, 1)
    neg = jnp.float32(-1e30)
    big = jnp.int32(1 << 20)

    def first_argmax(v):
        mx = jnp.max(v, axis=-1, keepdims=True)
        idx = jnp.min(jnp.where(v == mx, lane, big), axis=-1, keepdims=True)
        return mx, idx

    lg = jnp.where(lane < N_GROUPS, logits, neg)
    mg, grp = first_argmax(lg)
    p_grp = 1.0 / jnp.sum(jnp.where(lane < N_GROUPS, jnp.exp(lg - mg), 0.0), axis=-1, keepdims=True)
    in_grp = (lane >= N_GROUPS) & (lane < N_GROUPS + N_EXPERTS) & ((lane - N_GROUPS) // EXPERTS_PER_GROUP == grp)
    le = jnp.where(in_grp, logits, neg)
    m1, i1 = first_argmax(le)
    m2, i2 = first_argmax(jnp.where(lane == i1, neg, le))
    e2 = jnp.exp(m2 - m1)
    w1 = p_grp / (1.0 + e2)
    w2 = p_grp * e2 / (1.0 + e2)
    rid_ref[...] = jnp.where(lane == 0, i1 - N_GROUPS, jnp.where(lane == 1, i2 - N_GROUPS, 0))
    rgate_ref[...] = jnp.where(lane == 0, w1, jnp.where(lane == 1, w2, 0.0))


def _outproj_call(x2, o_r, o_g, mod3, mod_row, wo_bf, n2g, rw, rb):
    nt = x2.shape[0]
    tile = lambda w: pl.BlockSpec((TOK_TILE, w), lambda i: (i, 0))
    const = lambda a: pl.BlockSpec(a.shape, lambda i: (0,) * a.ndim)
    return pl.pallas_call(
        _outproj_kernel,
        out_shape=(jax.ShapeDtypeStruct((nt, D_MODEL), F32), jax.ShapeDtypeStruct((nt, D_MODEL), F32),
                   jax.ShapeDtypeStruct((nt, LANES), jnp.int32), jax.ShapeDtypeStruct((nt, LANES), F32)),
        grid=(nt // TOK_TILE,),
        in_specs=[tile(D_MODEL), tile(R_WIDTH), tile(G_WIDTH),
                  pl.BlockSpec((1, 1, 6 * D_MODEL), lambda i: (mod_row(i), 0, 0)),
                  const(wo_bf), const(n2g), const(rw), const(rb)],
        out_specs=(tile(D_MODEL), tile(D_MODEL), tile(LANES), tile(LANES)),
        compiler_params=_cparams(("arbitrary",)),
        name="outproj",
    )(x2, o_r, o_g, mod3, wo_bf, n2g, rw, rb)


PLAN_TILE = 1024


def _plan_kernel(rid_ref, cnt_in_ref, route_ref, cnt_ref, carry):
    i = pl.program_id(0)

    @pl.when(i == 0)
    def _():
        carry[...] = cnt_in_ref[...].astype(F32)

    rid = rid_ref[...]
    lane = _iota(rid.shape, 1)
    e0 = jnp.sum(jnp.where(lane == 0, rid, 0), axis=-1, keepdims=True)
    e1 = jnp.sum(jnp.where(lane == 1, rid, 0), axis=-1, keepdims=True)
    oh0 = (lane == e0).astype(F32)
    oh1 = (lane == e1).astype(F32)
    oh = oh0 + oh1
    n = rid.shape[0]
    earlier = (_iota((n, n), 0) > _iota((n, n), 1)).astype(BF16)
    before = jnp.dot(earlier, oh.astype(BF16), preferred_element_type=F32) + carry[0:1, :]
    r0 = jnp.sum(oh0 * before, axis=-1, keepdims=True)
    r1 = jnp.sum(oh1 * before, axis=-1, keepdims=True)
    cols = jnp.where(lane == 0, e0.astype(F32), jnp.where(lane == 1, e1.astype(F32),
                                                          jnp.where(lane == 2, r0, jnp.where(lane == 3, r1, 0.0))))
    route_ref[...] = jnp.transpose(cols)[0:8, :].astype(jnp.int32)
    total = carry[0:1, :] + jnp.sum(oh, axis=0, keepdims=True)
    carry[...] = jnp.broadcast_to(total, carry.shape)
    cnt_ref[...] = jnp.broadcast_to(total, cnt_ref.shape).astype(jnp.int32)


def _plan_call(rid, cnt_in):
    nt = rid.shape[0]
    return pl.pallas_call(
        _plan_kernel,
        out_shape=(jax.ShapeDtypeStruct((8, nt), jnp.int32), jax.ShapeDtypeStruct((8, LANES), jnp.int32)),
        grid=(nt // PLAN_TILE,),
        in_specs=[pl.BlockSpec((PLAN_TILE, LANES), lambda i: (i, 0)), pl.BlockSpec((8, LANES), lambda i: (0, 0))],
        out_specs=(pl.BlockSpec((8, PLAN_TILE), lambda i: (0, i)), pl.BlockSpec((8, LANES), lambda i: (0, 0))),
        scratch_shapes=[pltpu.VMEM((8, LANES), F32)],
        compiler_params=_cparams(("arbitrary",)),
        name="plan",
    )(rid, cnt_in)


def _slots_kernel(off_ref, route_ref, slot_ref):
    route = route_ref[...]
    ids = jnp.concatenate([route[0:2], route[0:2], route[0:2], route[0:2]], axis=0)
    rank = jnp.concatenate([route[2:4], route[2:4], route[2:4], route[2:4]], axis=0)
    base = jnp.zeros_like(ids)
    for e in range(N_EXPERTS):
        base = jnp.where(ids == e, off_ref[e], base)
    slot_ref[...] = base + rank


def _slots_call(off, route):
    nt = route.shape[1]
    tile = min(nt, 2048)
    return pl.pallas_call(
        _slots_kernel,
        out_shape=jax.ShapeDtypeStruct((8, nt), jnp.int32),
        grid=(nt // tile,),
        in_specs=[pl.BlockSpec(memory_space=pltpu.SMEM), pl.BlockSpec((8, tile), lambda i: (0, i))],
        out_specs=pl.BlockSpec((8, tile), lambda i: (0, i)),
        compiler_params=_cparams(("arbitrary",)),
        name="slots",
    )(off, route)


def _dispatch_kernel(slot_ref, h2_ref, *rest, fill, n_rows):
    if fill:
        xs_ref, zero_scr, sem = rest
    else:
        _, xs_ref, sem = rest
    i = pl.program_id(0)

    if fill:
        @pl.when(i == 0)
        def _():
            zero_scr[...] = jnp.zeros_like(zero_scr)
            nblk = n_rows // MOE_ROWS

            def fill_copy(b):
                return pltpu.make_async_copy(zero_scr, xs_ref.at[pl.ds(b * MOE_ROWS, MOE_ROWS)], sem)

            def start(b, c):
                fill_copy(b).start()
                return c

            def wait(b, c):
                fill_copy(b).wait()
                return c

            lax.fori_loop(0, nblk, start, 0)
            lax.fori_loop(0, nblk, wait, 0)

    def start(t, c):
        for k in range(2):
            pltpu.make_async_copy(h2_ref.at[pl.ds(t, 1)], xs_ref.at[pl.ds(slot_ref[k, t], 1)], sem).start()
        return c

    lax.fori_loop(0, TOK_TILE, start, 0, unroll=8)
    for _ in range(2):
        pltpu.make_async_copy(h2_ref, xs_ref.at[pl.ds(0, TOK_TILE)], sem).wait()


def _route_spec(shift=0, last=None):
    if shift == 0:
        return pl.BlockSpec((8, TOK_TILE), lambda i: (0, i), memory_space=pltpu.SMEM)
    return pl.BlockSpec((8, TOK_TILE), lambda i: (0, jnp.minimum(i + shift, last)), memory_space=pltpu.SMEM)


def _dispatch_call(slots, h2, xs, n_rows):
    nt = h2.shape[0]
    fill = xs is None
    kern = functools.partial(_dispatch_kernel, fill=fill, n_rows=n_rows)
    in_specs = [_route_spec(), pl.BlockSpec((TOK_TILE, D_MODEL), lambda i: (i, 0))]
    args = [slots, h2]
    scratch = [pltpu.SemaphoreType.DMA(())]
    aliases = {}
    if fill:
        scratch = [pltpu.VMEM((MOE_ROWS, D_MODEL), F32)] + scratch
    else:
        in_specs.append(pl.BlockSpec(memory_space=pl.ANY))
        args.append(xs)
        aliases = {2: 0}
    return pl.pallas_call(
        kern,
        out_shape=jax.ShapeDtypeStruct((n_rows, D_MODEL), F32),
        grid=(nt // TOK_TILE,),
        in_specs=in_specs,
        out_specs=pl.BlockSpec(memory_space=pl.ANY),
        scratch_shapes=scratch,
        input_output_aliases=aliases,
        compiler_params=_cparams(("arbitrary",)),
        name="dispatch_fill" if fill else "dispatch",
    )(*args)


def _experts_kernel(be_ref, nu_ref, xs_ref, wg_ref, wu_ref, wd_ref, ys_ref, wg_bf, wu_bf, wd_bf):
    b = pl.program_id(0)
    used = b < nu_ref[0]

    @pl.when(used)
    def _():
        prev = be_ref[jnp.maximum(b - 1, 0)]

        x = xs_ref[...]
        g = jnp.dot(x, wg_ref[0], preferred_element_type=F32)
        u = jnp.dot(x, wu_ref[0], preferred_element_type=F32)
        h = _silu(g) * u
        ys_ref[...] = jnp.dot(h, wd_ref[0], preferred_element_type=F32)

    @pl.when(jnp.logical_not(used))
    def _():
        ys_ref[...] = jnp.zeros_like(ys_ref)


def _experts_call(blk_e, n_used, xs, wg, wu, wd):
    n_rows = xs.shape[0]
    nb = n_rows // MOE_ROWS
    grid_spec = pltpu.PrefetchScalarGridSpec(
        num_scalar_prefetch=2,
        grid=(nb,),
        in_specs=[pl.BlockSpec((MOE_ROWS, D_MODEL), lambda b, be, nu: (jnp.minimum(b, nu[0] - 1), 0)),
                  pl.BlockSpec((1, D_MODEL, D_EXPERT), lambda b, be, nu: (be[b], 0, 0)),
                  pl.BlockSpec((1, D_MODEL, D_EXPERT), lambda b, be, nu: (be[b], 0, 0)),
                  pl.BlockSpec((1, D_EXPERT, D_MODEL), lambda b, be, nu: (be[b], 0, 0))],
        out_specs=pl.BlockSpec((MOE_ROWS, D_MODEL), lambda b, be, nu: (b, 0)),
        scratch_shapes=[pltpu.VMEM((D_MODEL, D_EXPERT), BF16), pltpu.VMEM((D_MODEL, D_EXPERT), BF16),
                        pltpu.VMEM((D_EXPERT, D_MODEL), BF16)],
    )
    return pl.pallas_call(
        _experts_kernel,
        out_shape=jax.ShapeDtypeStruct((n_rows, D_MODEL), F32),
        grid_spec=grid_spec,
        compiler_params=_cparams(("arbitrary",)),
        name="experts",
    )(blk_e, n_used, xs, wg, wu, wd)


def _combine_kernel(slot_ref, slot_next_ref, x1_ref, gate_ref, mod_ref, fg_ref, ys_ref, y_ref, buf, sems):
    i = pl.program_id(0)
    n = pl.num_programs(0)
    slot = i % 2

    def gather(slots, s):
        def start(t, c):
            for k in range(2):
                pltpu.make_async_copy(ys_ref.at[pl.ds(slots[k, t], 1)], buf.at[s, k, pl.ds(t, 1)], sems.at[s]).start()
            return c

        lax.fori_loop(0, TOK_TILE, start, 0, unroll=8)

    @pl.when(i == 0)
    def _():
        gather(slot_ref, 0)

    @pl.when(i + 1 < n)
    def _():
        gather(slot_next_ref, 1 - slot)

    for k in range(2):
        pltpu.make_async_copy(ys_ref.at[pl.ds(0, TOK_TILE)], buf.at[slot, k], sems.at[slot]).wait()

    gate = gate_ref[...]
    lane = _iota(gate.shape, 1)
    w0 = jnp.sum(jnp.where(lane == 0, gate, 0.0), axis=-1, keepdims=True)
    w1 = jnp.sum(jnp.where(lane == 1, gate, 0.0), axis=-1, keepdims=True)
    g2 = mod_ref[0][:, 5 * D_MODEL:6 * D_MODEL]
    y = x1_ref[...] + g2 * (w0 * buf[slot, 0] + w1 * buf[slot, 1])
    ms = jnp.mean(y * y, axis=-1, keepdims=True)
    y_ref[...] = y * lax.rsqrt(ms + NORM_EPS) * fg_ref[...]


def _combine_call(slots, x1, rgate, mod3, mod_row, fg, ys):
    nt = x1.shape[0]
    n_tiles = nt // TOK_TILE
    return pl.pallas_call(
        _combine_kernel,
        out_shape=jax.ShapeDtypeStruct((nt, D_MODEL), F32),
        grid=(n_tiles,),
        in_specs=[_route_spec(), _route_spec(1, n_tiles - 1),
                  pl.BlockSpec((TOK_TILE, D_MODEL), lambda i: (i, 0)),
                  pl.BlockSpec((TOK_TILE, LANES), lambda i: (i, 0)),
                  pl.BlockSpec((1, 1, 6 * D_MODEL), lambda i: (mod_row(i), 0, 0)),
                  pl.BlockSpec((1, D_MODEL), lambda i: (0, 0)),
                  pl.BlockSpec(memory_space=pl.ANY)],
        out_specs=pl.BlockSpec((TOK_TILE, D_MODEL), lambda i: (i, 0)),
        scratch_shapes=[pltpu.VMEM((2, 2, TOK_TILE, D_MODEL), F32), pltpu.SemaphoreType.DMA((2,))],
        compiler_params=_cparams(("arbitrary",)),
        name="combine",
    )(slots, slots, x1, rgate, mod3, fg, ys)


def kernel(x_prompt, x_sample, state_rwkv, state_gdn, c, c_ctx, ada_w, ada_b, norm1_g, norm2_g, w_in, w_out,
           rwkv_w0, rwkv_w2, rwkv_a0, rwkv_a2, rwkv_g2, rwkv_k_k, rwkv_k_a, rwkv_r_k, rwkv_lnx_g, rwkv_lnx_b,
           gdn_conv, gdn_a_log, gdn_dt_bias, gdn_norm_g, router_group_w, router_group_b, router_expert_w,
           router_expert_b, expert_gate, expert_up, expert_down, final_norm_g):
    assert ada_w.shape[0] == 1, "one layer"
    bp, tp, _ = x_prompt.shape
    bs, ts, _ = x_sample.shape
    lp = dict(rwkv_w0=rwkv_w0[0], rwkv_w2=rwkv_w2[0], rwkv_a0=rwkv_a0[0], rwkv_a2=rwkv_a2[0], rwkv_g2=rwkv_g2[0],
              rwkv_k_k=rwkv_k_k[0], rwkv_k_a=rwkv_k_a[0], rwkv_r_k=rwkv_r_k[0], rwkv_lnx_g=rwkv_lnx_g[0],
              rwkv_lnx_b=rwkv_lnx_b[0], gdn_conv=gdn_conv[0], gdn_a_log=gdn_a_log[0],
              gdn_dt_bias=gdn_dt_bias[0], gdn_norm_g=gdn_norm_g[0])
    rw_w = _rwkv_weights(lp)
    gd_w = _gdn_weights(lp)

    n_cond = 1 + bs
    cond = jnp.concatenate([c_ctx[None, :], c, jnp.zeros((16 - n_cond, D_MODEL), F32)], axis=0)
    mod = _mod_call(cond, ada_w[0], ada_b)
    mod3 = mod.reshape(16, 1, 6 * D_MODEL)

    in_cols = w_in.shape[2]
    w_in_bf = jnp.pad(w_in[0], ((0, 0), (0, RW + GW_PAD - in_cols))).astype(BF16)
    w_out_bf = w_out[0].astype(BF16)
    n1g = norm1_g.reshape(1, D_MODEL)
    n2g = norm2_g.reshape(1, D_MODEL)
    fg = final_norm_g.reshape(1, D_MODEL)
    rw = jnp.zeros((D_MODEL, LANES), F32).at[:, :N_GROUPS].set(router_group_w[0])
    rw = rw.at[:, N_GROUPS:N_GROUPS + N_EXPERTS].set(router_expert_w[0])
    rb = jnp.zeros((1, LANES), F32).at[0, :N_GROUPS].set(router_group_b[0])
    rb = rb.at[0, N_GROUPS:N_GROUPS + N_EXPERTS].set(router_expert_b[0])

    tiles_per_sample = ts // TOK_TILE
    passes = [
        dict(x=x_prompt.reshape(bp * tp, D_MODEL), batch=bp, seq=tp, row_len=tp,
             mod_row=lambda i: 0,
             s_r=None, s_g=None),
        dict(x=x_sample.reshape(bs * ts, D_MODEL), batch=bs, seq=ts, row_len=GRID_W,
             mod_row=lambda i: 1 + i // tiles_per_sample,
             s_r=state_rwkv[:, 0], s_g=state_gdn[:, 0]),
    ]

    for p in passes:
        proj_r, proj_g = _inproj_call(p["x"], mod3, p["mod_row"], n1g, w_in_bf)
        o_r, p["st_r"] = _rwkv_call(proj_r, p["s_r"], rw_w, batch=p["batch"], seq=p["seq"], nrow=SEQS_PER_STEP,
                                    hp=False)
        nrow_g = GDN_SEQS_PER_STEP_SHORT if p["seq"] <= 256 else SEQS_PER_STEP
        o_g, p["st_g"] = _gdn_call(proj_g, p["s_g"], gd_w, batch=p["batch"], seq=p["seq"], nrow=nrow_g,
                                   row_len=p["row_len"], hp=False)
        p["x1"], p["h2"], p["rid"], p["rgate"] = _outproj_call(
            p["x"], o_r, o_g, mod3, p["mod_row"], w_out_bf, n2g, rw, rb)

    cnt = jnp.zeros((8, LANES), jnp.int32)
    for p in passes:
        p["route"], cnt = _plan_call(p["rid"], cnt)
    counts = cnt[0, :N_EXPERTS]
    padded = (counts + MOE_ROWS - 1) // MOE_ROWS * MOE_ROWS
    ends = jnp.cumsum(padded)
    off = (ends - padded).astype(jnp.int32)
    nt_all = sum(p["rid"].shape[0] for p in passes)
    n_blocks = (2 * nt_all) // MOE_ROWS + N_EXPERTS
    n_rows = n_blocks * MOE_ROWS
    n_used = (ends[-1] // MOE_ROWS).astype(jnp.int32)
    blk_start = jnp.minimum(jnp.arange(n_blocks, dtype=jnp.int32), n_used - 1) * MOE_ROWS
    blk_e = jnp.minimum(jnp.sum((blk_start[:, None] >= ends[None, :]).astype(jnp.int32), axis=1), N_EXPERTS - 1)

    xs = None
    for p in passes:
        p["slots"] = _slots_call(off, p["route"])
        xs = _dispatch_call(p["slots"], p["h2"], xs, n_rows)
    ys = _experts_call(blk_e, n_used.reshape(1), xs, expert_gate[0], expert_up[0], expert_down[0])
    outs = [_combine_call(p["slots"], p["x1"], p["rgate"], mod3, p["mod_row"], fg, ys) for p in passes]

    y_prompt = outs[0].reshape(bp, tp, D_MODEL)
    y_sample = outs[1].reshape(bs, ts, D_MODEL)
    new_state_rwkv = passes[0]["st_r"][:, None]
    new_state_gdn = passes[0]["st_g"][:, None]
    return (y_prompt, y_sample, new_state_rwkv, new_state_gdn)
```

```python
import functools
import math

import jax
import jax.numpy as jnp
from jax import lax
from jax.experimental import pallas as pl
from jax.experimental.pallas import tpu as pltpu

F32 = jnp.float32
BF16 = jnp.bfloat16

D_MODEL = 1024
R_HEADS, R_HEAD = 8, 64
R_WIDTH = R_HEADS * R_HEAD
G_HEADS, G_HEAD = 4, 128
G_WIDTH = G_HEADS * G_HEAD
LNX_EPS = 64e-5
NORM_EPS = 1e-6
N_GROUPS, EXPERTS_PER_GROUP = 4, 8
N_EXPERTS = N_GROUPS * EXPERTS_PER_GROUP
D_EXPERT = 512
GRID_W = 64

CHUNK = 64
TOK_TILE = 256
OUT_TILE = 512
MOE_ROWS = 512
SEQS_PER_STEP = 2
GDN_SEQS_PER_STEP_SHORT = 4
LANES = 128
RW = 3 * R_WIDTH + 256
GW_PAD = 4 * G_WIDTH + LANES
VMEM_LIMIT = 56 * 1024 * 1024


def _cparams(sem):
    return pltpu.CompilerParams(dimension_semantics=sem, vmem_limit_bytes=VMEM_LIMIT)


def _sigmoid(x):
    return 1.0 / (1.0 + jnp.exp(-x))


def _silu(x):
    return x * _sigmoid(x)


def _softplus(x):
    return jnp.maximum(x, 0.0) + jnp.log(1.0 + jnp.exp(-jnp.abs(x)))


def _dg(a, b, ca, cb):
    return lax.dot_general(a, b, (((ca,), (cb,)), ((), ())), preferred_element_type=F32)


def _split2(x):
    hi = x.astype(BF16)
    lo = (x - hi.astype(F32)).astype(BF16)
    return hi, lo


def _split3(x):
    h1 = x.astype(BF16)
    r1 = x - h1.astype(F32)
    h2 = r1.astype(BF16)
    h3 = (r1 - h2.astype(F32)).astype(BF16)
    return h1, h2, h3


def _mm(a, b, ca=1, cb=0, hp=False):
    if not hp:
        return _dg(a.astype(BF16), b.astype(BF16), ca, cb)
    ah, al = _split2(a)
    bh, bl = _split2(b)
    return _dg(ah, bh, ca, cb) + (_dg(ah, bl, ca, cb) + _dg(al, bh, ca, cb))


def _mm_exact_lhs(m_bf, x, ca=1, cb=0, parts=3):
    if parts == 2:
        hi, lo = _split2(x)
        return _dg(m_bf, hi, ca, cb) + _dg(m_bf, lo, ca, cb)
    h1, h2, h3 = _split3(x)
    return _dg(m_bf, h1, ca, cb) + (_dg(m_bf, h2, ca, cb) + _dg(m_bf, h3, ca, cb))


def _mm_exact_rhs(x, m_bf):
    h1, h2, h3 = _split3(x)
    return _dg(h1, m_bf, 1, 0) + (_dg(h2, m_bf, 1, 0) + _dg(h3, m_bf, 1, 0))


def _group_sum(x, ones_bf):
    w = ones_bf.shape[0]
    hi, lo = _split2(x)
    return jnp.concatenate([_dg(hi[:, j:j + w], ones_bf, 1, 0) + _dg(lo[:, j:j + w], ones_bf, 1, 0)
                            for j in range(0, x.shape[1], w)], axis=1)


def _iota(shape, axis):
    return lax.broadcasted_iota(jnp.int32, shape, axis)


def _tri_masks(n, rev):
    t = _iota((n, n), 0)
    s = _iota((n, n), 1)
    if rev:
        return t < s, t <= s
    return t > s, t >= s


PACK = 4


def _level_masks(n):
    t = _iota((n, PACK * n), 0)
    u = _iota((n, PACK * n), 1) % n
    levels = []
    s = 1
    while s < n:
        levels.append((t // (2 * s) == u // (2 * s)) & (t // s != u // s))
        s *= 2
    diag = _iota((PACK * n, PACK * n), 0) // n == _iota((PACK * n, PACK * n), 1) // n
    return levels, diag


def _block_diag(x, diag):
    return jnp.where(diag, jnp.concatenate([x] * PACK, axis=0), 0.0).astype(BF16)


def _unit_tri_inverse(ms, masks):
    levels, diag = masks
    n = ms[0].shape[0]
    eye = (_iota((n, PACK * n), 0) == _iota((n, PACK * n), 1) % n).astype(F32)
    xs = [eye - jnp.where(levels[0], m, 0.0) for m in ms]
    for mask in levels[1:]:
        zs = [_dg(jnp.where(mask, m, 0.0).astype(BF16), _block_diag(x, diag), 1, 0) for m, x in zip(ms, xs)]
        xs = [x - _dg(x.astype(BF16), _block_diag(z, diag), 1, 0) for x, z in zip(xs, zs)]
    return xs


def _mod_kernel(c_ref, w_ref, b_ref, o_ref):
    s = _silu(c_ref[...])
    o_ref[...] = _mm(s, w_ref[...], hp=True) + b_ref[...]


def _mod_call(cond, ada_w, ada_b):
    rows = cond.shape[0]
    n = ada_w.shape[1]
    bn = 1024
    return pl.pallas_call(
        _mod_kernel,
        out_shape=jax.ShapeDtypeStruct((rows, n), F32),
        grid=(n // bn,),
        in_specs=[pl.BlockSpec((rows, D_MODEL), lambda j: (0, 0)),
                  pl.BlockSpec((D_MODEL, bn), lambda j: (0, j)),
                  pl.BlockSpec((1, bn), lambda j: (0, j))],
        out_specs=pl.BlockSpec((rows, bn), lambda j: (0, j)),
        compiler_params=_cparams(("arbitrary",)),
        name="mod",
    )(cond, ada_w, ada_b)


def _inproj_kernel(x_ref, mod_ref, g_ref, w_ref, or_ref, og_ref):
    x = x_ref[...]
    ms = jnp.mean(x * x, axis=-1, keepdims=True)
    y = x * lax.rsqrt(ms + NORM_EPS) * g_ref[...]
    m = mod_ref[0]
    h = y * (1.0 + m[:, D_MODEL:2 * D_MODEL]) + m[:, :D_MODEL]
    p = jnp.dot(h.astype(BF16), w_ref[...], preferred_element_type=F32)
    or_ref[...] = p[:, :RW].astype(BF16)
    og_ref[...] = p[:, RW:].astype(BF16)


def _inproj_call(x2, mod3, mod_row, norm_g, w_in_bf):
    nt = x2.shape[0]
    ncol = w_in_bf.shape[1]
    return pl.pallas_call(
        _inproj_kernel,
        out_shape=(jax.ShapeDtypeStruct((nt, RW), BF16),
                   jax.ShapeDtypeStruct((nt, GW_PAD), BF16)),
        grid=(nt // TOK_TILE,),
        in_specs=[pl.BlockSpec((TOK_TILE, D_MODEL), lambda i: (i, 0)),
                  pl.BlockSpec((1, 1, 6 * D_MODEL), lambda i: (mod_row(i * TOK_TILE), 0, 0)),
                  pl.BlockSpec((1, D_MODEL), lambda i: (0, 0)),
                  pl.BlockSpec((D_MODEL, ncol), lambda i: (0, 0))],
        out_specs=(pl.BlockSpec((TOK_TILE, RW), lambda i: (i, 0)),
                   pl.BlockSpec((TOK_TILE, GW_PAD), lambda i: (i, 0))),
        compiler_params=_cparams(("arbitrary",)),
        name="inproj",
    )(x2, mod3, norm_g, w_in_bf)


def _head_blocks(width, head):
    return (_iota((width, width), 0) // head == _iota((width, width), 1) // head).astype(BF16)


def _rwkv_kernel(p_ref, *refs, seq, nrow, hp, has_s0):
    s0_ref = refs[0] if has_s0 else None
    (w0_ref, wd_ref, a0_ref, wa_ref, g2_ref, kk_ref, ka_ref, rk_ref, lng_ref, lnb_ref,
     o_ref, st_ref, y_scr) = refs[1:] if has_s0 else refs
    C = CHUNK
    nc = seq // C
    blk = _head_blocks(2 * LANES, R_HEAD)
    k_k = kk_ref[...]
    k_a = ka_ref[...]
    r_k = rk_ref[...]
    exp_m05 = math.exp(-0.5)
    inv_masks = _level_masks(C)

    def lora_lhs(la):
        return jnp.where(_iota(la.shape, 1) < 64, jnp.tanh(la), la)

    def cols(n, rows, lo, hi):
        return p_ref[n, rows, lo:hi].astype(F32)

    assert R_HEAD == C
    PW = PACK * R_HEAD
    diag = inv_masks[1]
    pk_t = _iota((C, PW), 0)
    pk_s = _iota((C, PW), 1) % C
    dirs = []
    for d, rev in ((0, False), (1, True)):
        dirs.append(dict(d=d, rev=rev, tri=_tri_masks(C, rev)[1].astype(BF16),
                         strict=(pk_t < pk_s) if rev else (pk_t > pk_s),
                         incl=(pk_t <= pk_s) if rev else (pk_t >= pk_s)))
    hd = range(R_HEADS)
    sls = [slice(h * R_HEAD, (h + 1) * R_HEAD) for h in hd]

    def prep(n, dr, c):
        d, rev, tri = dr["d"], dr["rev"], dr["tri"]
        rows = pl.ds(pl.multiple_of(c * C, C), C)
        rr = cols(n, rows, 0, R_WIDTH)
        kx = cols(n, rows, R_WIDTH, 2 * R_WIDTH)
        vv = cols(n, rows, 2 * R_WIDTH, 3 * R_WIDTH)
        lhs = lora_lhs(cols(n, rows, 3 * R_WIDTH, 3 * R_WIDTH + LANES))
        logw = -exp_m05 * _sigmoid(w0_ref[d] + _mm(lhs, wd_ref[d]))
        a = _sigmoid(a0_ref[d] + _mm(lhs, wa_ref[d]))
        kkr = kx * k_k
        kk = kkr * lax.rsqrt(_group_sum(kkr * kkr, blk) + 1e-6)
        kd = kx * (1.0 + (a - 1.0) * k_a)
        bvec = kk * a
        gi = _mm_exact_lhs(tri, logw, parts=2)
        gtot = gi[0:1] if rev else gi[C - 1:C]
        en = jnp.exp(-gi)
        ee = jnp.exp(gtot - gi)
        return dict(rows=rows, vv=vv, rt=rr * jnp.exp(gi), at=-kk * jnp.exp(gi - logw), bt=bvec * en, kt=kd * en,
                    bh=bvec * ee, kh=kd * ee, gc=jnp.exp(gtot))

    seqs = range(nrow)
    packs = [(n, d, g) for n in seqs for d in (0, 1) for g in range(R_HEADS // PACK)]

    def step(i, carry):
        ops = {(n, d): prep(n, dirs[d], (nc - 1 - i) if d else i) for n in seqs for d in (0, 1)}
        pcol = lambda name, n, d, g: ops[n, d][name][:, g * PW:(g + 1) * PW]
        bd = lambda x: _block_diag(x, diag)
        ars = {k: jnp.concatenate([pcol("at", *k), pcol("rt", *k)], axis=0).astype(BF16) for k in packs}
        pb = {k: _dg(ars[k], bd(pcol("bt", *k)), 1, 1) for k in packs}
        pk = {k: _dg(ars[k], bd(pcol("kt", *k)), 1, 1) for k in packs}
        sts = {(n, d, g): carry[2 * n + d][:, g * PW:(g + 1) * PW] for n, d, g in packs}
        x0s = {k: _dg(ars[k], bd(sts[k]), 1, 1) for k in packs}
        bdv = {k: bd(pcol("vv", *k)) for k in packs}
        akv = {(n, d, g): _dg(jnp.where(dirs[d]["strict"], pk[n, d, g][:C], 0.0).astype(BF16), bdv[n, d, g], 1, 0)
               for n, d, g in packs}
        tms = dict(zip(packs, _unit_tri_inverse(
            [jnp.where(dirs[d]["strict"], -pb[n, d, g][:C], 0.0) for n, d, g in packs], inv_masks)))
        us = {k: _dg(tms[k].astype(BF16), bd(x0s[k][:C] + akv[k]), 1, 0) for k in packs}
        ys = {(n, d, g): x0s[n, d, g][C:] + _dg(
            jnp.concatenate([jnp.where(dirs[d]["incl"], pb[n, d, g][C:], 0.0),
                             jnp.where(dirs[d]["incl"], pk[n, d, g][C:], 0.0)], axis=1).astype(BF16),
            jnp.concatenate([bd(us[n, d, g]), bdv[n, d, g]], axis=0), 1, 0) for n, d, g in packs}
        snew = {}
        for n, d, g in packs:
            full = _dg(jnp.concatenate([us[n, d, g], pcol("vv", n, d, g)], axis=0).astype(BF16),
                       jnp.concatenate([pcol("bh", n, d, g), pcol("kh", n, d, g)], axis=0).astype(BF16), 0, 0)
            own = jnp.where(diag, full, 0.0)
            upd = own[0:R_HEAD]
            for j in range(1, PACK):
                upd = upd + own[j * R_HEAD:(j + 1) * R_HEAD]
            snew[n, d, g] = sts[n, d, g] * pcol("gc", n, d, g) + upd
        for n, d, g in packs:
            y_scr[n, d, ops[n, d]["rows"], g * PW:(g + 1) * PW] = ys[n, d, g]
        return tuple(jnp.concatenate([snew[n, d, g] for g in range(R_HEADS // PACK)], axis=1)
                     for n in seqs for d in (0, 1))

    def initial(n, d):
        if not has_s0:
            return jnp.zeros((R_HEAD, R_WIDTH), F32)
        return jnp.concatenate([s0_ref[n, d, h] for h in hd], axis=1)

    s_fin = lax.fori_loop(0, nc, step, tuple(initial(n, d) for n in seqs for d in (0, 1)))
    for n in seqs:
        for d in (0, 1):
            for h in hd:
                st_ref[n, d, h] = s_fin[2 * n + d][:, sls[h]]

    g2 = g2_ref[...]
    lng = lng_ref[...]
    lnb = lnb_ref[...]
    inv_n = 1.0 / R_HEAD
    fb = min(seq, 256)

    def finish(j, carry):
        rows = pl.ds(pl.multiple_of(j * fb, fb), fb)
        for n in seqs:
            y = y_scr[n, 0, rows, :] + y_scr[n, 1, rows, :]
            mu = _group_sum(y, blk) * inv_n
            yc = y - mu
            var = _group_sum(yc * yc, blk) * inv_n
            yn = yc * lax.rsqrt(var + LNX_EPS) * lng + lnb
            lhs = lora_lhs(cols(n, rows, 3 * R_WIDTH, 3 * R_WIDTH + LANES))
            a_sum = _sigmoid(a0_ref[0] + _mm(lhs, wa_ref[0])) + _sigmoid(a0_ref[1] + _mm(lhs, wa_ref[1]))
            rr = cols(n, rows, 0, R_WIDTH)
            kx = cols(n, rows, R_WIDTH, 2 * R_WIDTH)
            kd_sum = kx * (2.0 + (a_sum - 2.0) * k_a)
            bonus = _group_sum(rr * kd_sum * r_k, blk) * cols(n, rows, 2 * R_WIDTH, 3 * R_WIDTH)
            gl = cols(n, rows, 3 * R_WIDTH + LANES, 3 * R_WIDTH + 2 * LANES)
            gate = _mm(_sigmoid(gl), g2)
            o_ref[n, rows, :] = (yn + bonus) * gate
        return carry

    lax.fori_loop(0, seq // fb, finish, 0)


def _rwkv_call(proj_r, s0, w, *, batch, seq, nrow, hp):
    p3 = proj_r.reshape(batch, seq, RW)
    kern = functools.partial(_rwkv_kernel, seq=seq, nrow=nrow, hp=hp, has_s0=s0 is not None)
    full = lambda a: pl.BlockSpec(a.shape, lambda b: (0,) * a.ndim)
    ws = [w["w0"], w["wd"], w["a0"], w["wa"], w["g2"], w["k_k"], w["k_a"], w["r_k"], w["lnx_g"], w["lnx_b"]]
    state_spec = pl.BlockSpec((nrow, 2, R_HEADS, R_HEAD, R_HEAD), lambda b: (b, 0, 0, 0, 0))
    states = [] if s0 is None else [s0]
    o, st = pl.pallas_call(
        kern,
        out_shape=(jax.ShapeDtypeStruct((batch, seq, R_WIDTH), F32),
                   jax.ShapeDtypeStruct((batch, 2, R_HEADS, R_HEAD, R_HEAD), F32)),
        grid=(batch // nrow,),
        in_specs=[pl.BlockSpec((nrow, seq, RW), lambda b: (b, 0, 0))] + [state_spec] * len(states)
        + [full(a) for a in ws],
        out_specs=(pl.BlockSpec((nrow, seq, R_WIDTH), lambda b: (b, 0, 0)), state_spec),
        scratch_shapes=[pltpu.VMEM((nrow, 2, seq, R_WIDTH), F32)],
        compiler_params=_cparams(("arbitrary",)),
        name="rwkv_hp" if hp else "rwkv",
    )(p3, *states, *ws)
    return o.reshape(batch * seq, R_WIDTH), st


def _rwkv_weights(lp):
    zeros = jnp.zeros((2, 64, R_WIDTH), F32)
    row = lambda a: jnp.asarray(a, F32).reshape(1, R_WIDTH)
    return {
        "w0": jnp.asarray(lp["rwkv_w0"], F32).reshape(2, 1, R_WIDTH),
        "wd": jnp.concatenate([jnp.asarray(lp["rwkv_w2"], F32), zeros], axis=1),
        "a0": jnp.asarray(lp["rwkv_a0"], F32).reshape(2, 1, R_WIDTH),
        "wa": jnp.concatenate([zeros, jnp.asarray(lp["rwkv_a2"], F32)], axis=1),
        "g2": jnp.asarray(lp["rwkv_g2"], F32),
        "k_k": row(lp["rwkv_k_k"]), "k_a": row(lp["rwkv_k_a"]), "r_k": row(lp["rwkv_r_k"]),
        "lnx_g": row(lp["rwkv_lnx_g"]), "lnx_b": row(lp["rwkv_lnx_b"]),
    }


def _gdn_kernel(p_ref, *refs, seq, nrow, row_len, hp, has_s0):
    s0_ref = refs[0] if has_s0 else None
    (cw_ref, alog_ref, dtb_ref, ng_ref, o_ref, st_ref, q_scr, k_scr, v_scr, o_scr) = refs[1:] if has_s0 else refs
    C = CHUNK
    nc = seq // C
    fb = min(seq, 256)
    W = G_WIDTH
    q_scale = G_HEAD ** -0.5
    seqs = range(nrow)
    hd = range(G_HEADS)
    hss = [slice(h * G_HEAD, (h + 1) * G_HEAD) for h in hd]

    cw0 = cw_ref[0:1, :]
    cw1 = cw_ref[1:2, :]
    cw2 = cw_ref[2:3, :]
    tpos = _iota((fb, 1), 0) % row_len
    is_first = tpos == 0
    is_last = tpos == row_len - 1

    def prep(j, carry):
        rows = pl.ds(pl.multiple_of(j * fb, fb), fb)
        for n in seqs:
            x = p_ref[n, rows, 0:3 * W].astype(F32)
            xm = jnp.where(is_first, 0.0, pltpu.roll(x, 1, 0))
            xp = jnp.where(is_last, 0.0, pltpu.roll(x, fb - 1, 0))
            y = _silu(cw0 * xm + cw1 * x + cw2 * xp)
            for h in hd:
                qh = y[:, hss[h]]
                kh = y[:, W + h * G_HEAD:W + (h + 1) * G_HEAD]
                q_scr[n, rows, hss[h]] = qh * lax.rsqrt(jnp.sum(qh * qh, axis=-1, keepdims=True) + 1e-6) * q_scale
                k_scr[n, rows, hss[h]] = kh * lax.rsqrt(jnp.sum(kh * kh, axis=-1, keepdims=True) + 1e-6)
            v_scr[n, rows, :] = y[:, 2 * W:3 * W]
        return carry

    lax.fori_loop(0, seq // fb, prep, 0)

    assert G_HEADS == PACK
    inv_masks = _level_masks(C)
    alog = alog_ref[...]
    dtb = dtb_ref[...]
    sel_row = _iota((LANES, W), 0)
    sel_head = _iota((LANES, W), 1) // G_HEAD
    pk_row = _iota((LANES, PACK * C), 0)
    pk_head = _iota((LANES, PACK * C), 1) // C
    pk_t = _iota((C, PACK * C), 0)
    pk_s = _iota((C, PACK * C), 1) % C

    dirs = []
    for d, rev in ((0, False), (1, True)):
        dirs.append(dict(
            d=d, rev=rev, tri=_tri_masks(C, rev)[1].astype(BF16),
            strict=(pk_t < pk_s) if rev else (pk_t > pk_s),
            incl=(pk_t <= pk_s) if rev else (pk_t >= pk_s),
            upto=(pk_t >= pk_s) if rev else (pk_t <= pk_s),
            e_beta=(sel_row == 4 * d + sel_head).astype(BF16),
            e_g=(sel_row == 8 + 4 * d + sel_head).astype(BF16),
            e_g_pk=(pk_row == 8 + 4 * d + pk_head).astype(BF16)))
    items = [(n, d, h) for n in seqs for d in (0, 1) for h in hd]
    packs = [(n, d) for n in seqs for d in (0, 1)]

    def prep_dir(n, dr, c):
        rows = pl.ds(pl.multiple_of(c * C, C), C)
        ba = p_ref[n, rows, 4 * W:4 * W + LANES].astype(F32)
        gval = -jnp.exp(alog) * _softplus(ba + dtb)
        beta_b = _mm_exact_rhs(_sigmoid(ba), dr["e_beta"])
        gcum = _mm_exact_lhs(dr["tri"], _mm_exact_rhs(gval, dr["e_g"]))
        g_pk = _mm_exact_rhs(gval, dr["e_g_pk"])
        g_col = _mm_exact_lhs(dr["tri"], g_pk)
        g_row = jnp.sum(jnp.where(dr["upto"], g_pk, 0.0), axis=0, keepdims=True)
        return dict(rows=rows, beta=beta_b, gcum=gcum,
                    decay=jnp.where(dr["incl"], jnp.exp(jnp.where(dr["incl"], g_col - g_row, 0.0)), 0.0),
                    q=q_scr[n, rows, :], k=k_scr[n, rows, :], v=v_scr[n, rows, :])

    def step(i, carry):
        ops = {(n, d): prep_dir(n, dirs[d], (nc - 1 - i) if d else i) for n, d in packs}
        hcol = lambda name, n, d, h: ops[n, d][name][:, hss[h]]
        qhs = {k: hcol("q", *k) for k in items}
        khs = {k: hcol("k", *k) for k in items}
        gcs = {k: hcol("gcum", *k) for k in items}
        bbs = {k: hcol("beta", *k) for k in items}
        kbs = {k: khs[k] * bbs[k] for k in items}
        kks = {k: _mm(jnp.concatenate([kbs[k], qhs[k]], axis=0), khs[k], 1, 1, hp=hp) for k in items}
        kka = {(n, d): jnp.concatenate([kks[n, d, h][:C] for h in hd], axis=1) for n, d in packs}
        tcat = dict(zip(packs, _unit_tri_inverse(
            [jnp.where(dirs[d]["strict"], kka[n, d] * ops[n, d]["decay"], 0.0) for n, d in packs], inv_masks)))
        egs = {k: jnp.exp(gcs[k]) for k in items}
        uws = {(n, d, h): _mm(tcat[n, d][:, h * C:(h + 1) * C],
                              jnp.concatenate([hcol("v", n, d, h) * bbs[n, d, h], kbs[n, d, h] * egs[n, d, h]], axis=1),
                              hp=hp) for n, d, h in items}
        shs = {(n, d, h): carry[2 * n + d][:, hss[h]] for n, d, h in items}
        wss = {k: _mm(jnp.concatenate([uws[k][:, G_HEAD:], qhs[k] * egs[k]], axis=0), shs[k], hp=hp) for k in items}
        vns = {k: uws[k][:, :G_HEAD] - wss[k][:C] for k in items}
        outs = {(n, d, h): wss[n, d, h][C:]
                + _mm(kks[n, d, h][C:] * ops[n, d]["decay"][:, h * C:(h + 1) * C], vns[n, d, h]) for n, d, h in items}
        snew = {}
        for n, d, h in items:
            glast = gcs[n, d, h][0:1] if dirs[d]["rev"] else gcs[n, d, h][C - 1:C]
            k_dec = khs[n, d, h] * jnp.exp(glast - gcs[n, d, h])
            snew[n, d, h] = shs[n, d, h] * jnp.exp(glast) + _mm(k_dec, vns[n, d, h], 0, 0, hp=hp)
        for n, d in packs:
            o_scr[n, d, ops[n, d]["rows"], :] = jnp.concatenate([outs[n, d, h] for h in hd], axis=1)
        return tuple(jnp.concatenate([snew[n, d, h] for h in hd], axis=1) for n, d in packs)

    def initial(n, d):
        if not has_s0:
            return jnp.zeros((G_HEAD, W), F32)
        return jnp.concatenate([s0_ref[n, d, h] for h in hd], axis=1)

    s_fin = lax.fori_loop(0, nc, step, tuple(initial(n, d) for n, d in packs))
    for j, (n, d) in enumerate(packs):
        for h in hd:
            st_ref[n, d, h] = s_fin[j][:, hss[h]]

    ng = ng_ref[...]

    def finish(j, carry):
        rows = pl.ds(pl.multiple_of(j * fb, fb), fb)
        for n in seqs:
            o = o_scr[n, 0, rows, :] + o_scr[n, 1, rows, :]
            z = p_ref[n, rows, 3 * W:4 * W].astype(F32)
            for h in hd:
                oh = o[:, hss[h]]
                ms = jnp.mean(oh * oh, axis=-1, keepdims=True)
                o_ref[n, rows, hss[h]] = oh * lax.rsqrt(ms + NORM_EPS) * ng * _silu(z[:, hss[h]])
        return carry

    lax.fori_loop(0, seq // fb, finish, 0)


def _gdn_call(proj_g, s0, w, *, batch, seq, nrow, row_len, hp):
    p3 = proj_g.reshape(batch, seq, GW_PAD)
    kern = functools.partial(_gdn_kernel, seq=seq, nrow=nrow, row_len=row_len, hp=hp, has_s0=s0 is not None)
    full = lambda a: pl.BlockSpec(a.shape, lambda b: (0,) * a.ndim)
    ws = [w["conv"], w["alog"], w["dtb"], w["norm_g"]]
    state_spec = pl.BlockSpec((nrow, 2, G_HEADS, G_HEAD, G_HEAD), lambda b: (b, 0, 0, 0, 0))
    states = [] if s0 is None else [s0]
    o, st = pl.pallas_call(
        kern,
        out_shape=(jax.ShapeDtypeStruct((batch, seq, G_WIDTH), F32),
                   jax.ShapeDtypeStruct((batch, 2, G_HEADS, G_HEAD, G_HEAD), F32)),
        grid=(batch // nrow,),
        in_specs=[pl.BlockSpec((nrow, seq, GW_PAD), lambda b: (b, 0, 0))] + [state_spec] * len(states)
        + [full(a) for a in ws],
        out_specs=(pl.BlockSpec((nrow, seq, G_WIDTH), lambda b: (b, 0, 0)), state_spec),
        scratch_shapes=[pltpu.VMEM((nrow, seq, G_WIDTH), F32)] * 3 + [pltpu.VMEM((nrow, 2, seq, G_WIDTH), F32)],
        compiler_params=_cparams(("arbitrary",)),
        name="gdn_hp" if hp else "gdn",
    )(p3, *states, *ws)
    return o.reshape(batch * seq, G_WIDTH), st


def _gdn_weights(lp):
    lanes = jnp.zeros((1, LANES), F32)
    return {
        "conv": jnp.asarray(lp["gdn_conv"], F32),
        "alog": lanes.at[0, 8:16].set(jnp.asarray(lp["gdn_a_log"], F32).reshape(8)),
        "dtb": lanes.at[0, 8:16].set(jnp.asarray(lp["gdn_dt_bias"], F32).reshape(8)),
        "norm_g": jnp.asarray(lp["gdn_norm_g"], F32).reshape(1, G_HEAD),
    }


def _outproj_kernel(x_ref, or_ref, og_ref, mod_ref, wo_ref, n2_ref, rw_ref, rb_ref,
                    x1_ref, h2_ref, rid_ref, rgate_ref):
    m = mod_ref[0]
    g1 = m[:, 2 * D_MODEL:3 * D_MODEL]
    sh2 = m[:, 3 * D_MODEL:4 * D_MODEL]
    sc2 = m[:, 4 * D_MODEL:5 * D_MODEL]
    mix = (jnp.dot(or_ref[...].astype(BF16), wo_ref[0:R_WIDTH, :], preferred_element_type=F32)
           + jnp.dot(og_ref[...].astype(BF16), wo_ref[R_WIDTH:, :], preferred_element_type=F32))
    x1 = x_ref[...] + g1 * mix
    x1_ref[...] = x1
    ms = jnp.mean(x1 * x1, axis=-1, keepdims=True)
    h2 = x1 * lax.rsqrt(ms + NORM_EPS) * n2_ref[...] * (1.0 + sc2) + sh2
    h2_ref[...] = h2

    logits = _mm(h2, rw_ref[...], hp=True) + rb_ref[...]
    lane = _iota(logits.shape, 1)
    neg = jnp.float32(-1e30)
    big = jnp.int32(1 << 20)

    def first_argmax(v):
        mx = jnp.max(v, axis=-1, keepdims=True)
        idx = jnp.min(jnp.where(v == mx, lane, big), axis=-1, keepdims=True)
        return mx, idx

    lg = jnp.where(lane < N_GROUPS, logits, neg)
    mg, grp = first_argmax(lg)
    p_grp = 1.0 / jnp.sum(jnp.where(lane < N_GROUPS, jnp.exp(lg - mg), 0.0), axis=-1, keepdims=True)
    in_grp = (lane >= N_GROUPS) & (lane < N_GROUPS + N_EXPERTS) & ((lane - N_GROUPS) // EXPERTS_PER_GROUP == grp)
    le = jnp.where(in_grp, logits, neg)
    m1, i1 = first_argmax(le)
    m2, i2 = first_argmax(jnp.where(lane == i1, neg, le))
    e2 = jnp.exp(m2 - m1)
    w1 = p_grp / (1.0 + e2)
    w2 = p_grp * e2 / (1.0 + e2)
    rid_ref[...] = jnp.where(lane == 0, i1 - N_GROUPS, jnp.where(lane == 1, i2 - N_GROUPS, 0))
    rgate_ref[...] = jnp.where(lane == 0, w1, jnp.where(lane == 1, w2, 0.0))


def _outproj_call(x2, o_r, o_g, mod3, mod_row, wo_bf, n2g, rw, rb):
    nt = x2.shape[0]
    tile = lambda w: pl.BlockSpec((OUT_TILE, w), lambda i: (i, 0))
    const = lambda a: pl.BlockSpec(a.shape, lambda i: (0,) * a.ndim)
    return pl.pallas_call(
        _outproj_kernel,
        out_shape=(jax.ShapeDtypeStruct((nt, D_MODEL), F32), jax.ShapeDtypeStruct((nt, D_MODEL), F32),
                   jax.ShapeDtypeStruct((nt, LANES), jnp.int32), jax.ShapeDtypeStruct((nt, LANES), F32)),
        grid=(nt // OUT_TILE,),
        in_specs=[tile(D_MODEL), tile(R_WIDTH), tile(G_WIDTH),
                  pl.BlockSpec((1, 1, 6 * D_MODEL), lambda i: (mod_row(i * OUT_TILE), 0, 0)),
                  const(wo_bf), const(n2g), const(rw), const(rb)],
        out_specs=(tile(D_MODEL), tile(D_MODEL), tile(LANES), tile(LANES)),
        compiler_params=_cparams(("arbitrary",)),
        name="outproj",
    )(x2, o_r, o_g, mod3, wo_bf, n2g, rw, rb)


PLAN_TILE = 1024


def _plan_kernel(rid_ref, cnt_in_ref, route_ref, cnt_ref, carry):
    i = pl.program_id(0)

    @pl.when(i == 0)
    def _():
        carry[...] = cnt_in_ref[...].astype(F32)

    rid = rid_ref[...]
    lane = _iota(rid.shape, 1)
    e0 = jnp.sum(jnp.where(lane == 0, rid, 0), axis=-1, keepdims=True)
    e1 = jnp.sum(jnp.where(lane == 1, rid, 0), axis=-1, keepdims=True)
    oh0 = (lane == e0).astype(F32)
    oh1 = (lane == e1).astype(F32)
    oh = oh0 + oh1
    n = rid.shape[0]
    earlier = (_iota((n, n), 0) > _iota((n, n), 1)).astype(BF16)
    before = jnp.dot(earlier, oh.astype(BF16), preferred_element_type=F32) + carry[0:1, :]
    r0 = jnp.sum(oh0 * before, axis=-1, keepdims=True)
    r1 = jnp.sum(oh1 * before, axis=-1, keepdims=True)
    cols = jnp.where(lane == 0, e0.astype(F32), jnp.where(lane == 1, e1.astype(F32),
                                                          jnp.where(lane == 2, r0, jnp.where(lane == 3, r1, 0.0))))
    route_ref[...] = jnp.transpose(cols)[0:8, :].astype(jnp.int32)
    total = carry[0:1, :] + jnp.sum(oh, axis=0, keepdims=True)
    carry[...] = jnp.broadcast_to(total, carry.shape)
    cnt_ref[...] = jnp.broadcast_to(total, cnt_ref.shape).astype(jnp.int32)


def _plan_call(rid, cnt_in):
    nt = rid.shape[0]
    return pl.pallas_call(
        _plan_kernel,
        out_shape=(jax.ShapeDtypeStruct((8, nt), jnp.int32), jax.ShapeDtypeStruct((8, LANES), jnp.int32)),
        grid=(nt // PLAN_TILE,),
        in_specs=[pl.BlockSpec((PLAN_TILE, LANES), lambda i: (i, 0)), pl.BlockSpec((8, LANES), lambda i: (0, 0))],
        out_specs=(pl.BlockSpec((8, PLAN_TILE), lambda i: (0, i)), pl.BlockSpec((8, LANES), lambda i: (0, 0))),
        scratch_shapes=[pltpu.VMEM((8, LANES), F32)],
        compiler_params=_cparams(("arbitrary",)),
        name="plan",
    )(rid, cnt_in)


def _slots_kernel(off_ref, route_ref, slot_ref):
    route = route_ref[...]
    ids = jnp.concatenate([route[0:2], route[0:2], route[0:2], route[0:2]], axis=0)
    rank = jnp.concatenate([route[2:4], route[2:4], route[2:4], route[2:4]], axis=0)
    base = jnp.zeros_like(ids)
    for e in range(N_EXPERTS):
        base = jnp.where(ids == e, off_ref[e], base)
    slot_ref[...] = base + rank


def _slots_call(off, route):
    nt = route.shape[1]
    tile = min(nt, 2048)
    return pl.pallas_call(
        _slots_kernel,
        out_shape=jax.ShapeDtypeStruct((8, nt), jnp.int32),
        grid=(nt // tile,),
        in_specs=[pl.BlockSpec(memory_space=pltpu.SMEM), pl.BlockSpec((8, tile), lambda i: (0, i))],
        out_specs=pl.BlockSpec((8, tile), lambda i: (0, i)),
        compiler_params=_cparams(("arbitrary",)),
        name="slots",
    )(off, route)


def _dispatch_kernel(slot_ref, h2_ref, *rest, fill, n_rows):
    if fill:
        xs_ref, zero_scr, sem = rest
    else:
        _, xs_ref, sem = rest
    i = pl.program_id(0)

    if fill:
        @pl.when(i == 0)
        def _():
            zero_scr[...] = jnp.zeros_like(zero_scr)
            nblk = n_rows // MOE_ROWS

            def fill_copy(b):
                return pltpu.make_async_copy(zero_scr, xs_ref.at[pl.ds(b * MOE_ROWS, MOE_ROWS)], sem)

            def start(b, c):
                fill_copy(b).start()
                return c

            def wait(b, c):
                fill_copy(b).wait()
                return c

            lax.fori_loop(0, nblk, start, 0)
            lax.fori_loop(0, nblk, wait, 0)

    def start(t, c):
        for k in range(2):
            pltpu.make_async_copy(h2_ref.at[pl.ds(t, 1)], xs_ref.at[pl.ds(slot_ref[k, t], 1)], sem).start()
        return c

    lax.fori_loop(0, TOK_TILE, start, 0, unroll=8)
    for _ in range(2):
        pltpu.make_async_copy(h2_ref, xs_ref.at[pl.ds(0, TOK_TILE)], sem).wait()


def _route_spec(shift=0, last=None):
    if shift == 0:
        return pl.BlockSpec((8, TOK_TILE), lambda i: (0, i), memory_space=pltpu.SMEM)
    return pl.BlockSpec((8, TOK_TILE), lambda i: (0, jnp.minimum(i + shift, last)), memory_space=pltpu.SMEM)


def _dispatch_call(slots, h2, xs, n_rows):
    nt = h2.shape[0]
    fill = xs is None
    kern = functools.partial(_dispatch_kernel, fill=fill, n_rows=n_rows)
    in_specs = [_route_spec(), pl.BlockSpec((TOK_TILE, D_MODEL), lambda i: (i, 0))]
    args = [slots, h2]
    scratch = [pltpu.SemaphoreType.DMA(())]
    aliases = {}
    if fill:
        scratch = [pltpu.VMEM((MOE_ROWS, D_MODEL), F32)] + scratch
    else:
        in_specs.append(pl.BlockSpec(memory_space=pl.ANY))
        args.append(xs)
        aliases = {2: 0}
    return pl.pallas_call(
        kern,
        out_shape=jax.ShapeDtypeStruct((n_rows, D_MODEL), F32),
        grid=(nt // TOK_TILE,),
        in_specs=in_specs,
        out_specs=pl.BlockSpec(memory_space=pl.ANY),
        scratch_shapes=scratch,
        input_output_aliases=aliases,
        compiler_params=_cparams(("arbitrary",)),
        name="dispatch_fill" if fill else "dispatch",
    )(*args)


def _experts_kernel(be_ref, nu_ref, xs_ref, wg_ref, wu_ref, wd_ref, ys_ref, wg_bf, wu_bf, wd_bf):
    b = pl.program_id(0)
    used = b < nu_ref[0]

    @pl.when(used)
    def _():
        prev = be_ref[jnp.maximum(b - 1, 0)]

        x = xs_ref[...]
        g = jnp.dot(x, wg_ref[0], preferred_element_type=F32)
        u = jnp.dot(x, wu_ref[0], preferred_element_type=F32)
        h = _silu(g) * u
        ys_ref[...] = jnp.dot(h, wd_ref[0], preferred_element_type=F32)

    @pl.when(jnp.logical_not(used))
    def _():
        ys_ref[...] = jnp.zeros_like(ys_ref)


def _experts_call(blk_e, n_used, xs, wg, wu, wd):
    n_rows = xs.shape[0]
    nb = n_rows // MOE_ROWS
    grid_spec = pltpu.PrefetchScalarGridSpec(
        num_scalar_prefetch=2,
        grid=(nb,),
        in_specs=[pl.BlockSpec((MOE_ROWS, D_MODEL), lambda b, be, nu: (jnp.minimum(b, nu[0] - 1), 0)),
                  pl.BlockSpec((1, D_MODEL, D_EXPERT), lambda b, be, nu: (be[b], 0, 0)),
                  pl.BlockSpec((1, D_MODEL, D_EXPERT), lambda b, be, nu: (be[b], 0, 0)),
                  pl.BlockSpec((1, D_EXPERT, D_MODEL), lambda b, be, nu: (be[b], 0, 0))],
        out_specs=pl.BlockSpec((MOE_ROWS, D_MODEL), lambda b, be, nu: (b, 0)),
        scratch_shapes=[pltpu.VMEM((D_MODEL, D_EXPERT), BF16), pltpu.VMEM((D_MODEL, D_EXPERT), BF16),
                        pltpu.VMEM((D_EXPERT, D_MODEL), BF16)],
    )
    return pl.pallas_call(
        _experts_kernel,
        out_shape=jax.ShapeDtypeStruct((n_rows, D_MODEL), F32),
        grid_spec=grid_spec,
        compiler_params=_cparams(("arbitrary",)),
        name="experts",
    )(blk_e, n_used, xs, wg, wu, wd)


def _combine_kernel(slot_ref, slot_next_ref, x1_ref, gate_ref, mod_ref, fg_ref, ys_ref, y_ref, buf, sems):
    i = pl.program_id(0)
    n = pl.num_programs(0)
    slot = i % 2

    def gather(slots, s):
        def start(t, c):
            for k in range(2):
                pltpu.make_async_copy(ys_ref.at[pl.ds(slots[k, t], 1)], buf.at[s, k, pl.ds(t, 1)], sems.at[s]).start()
            return c

        lax.fori_loop(0, TOK_TILE, start, 0, unroll=8)

    @pl.when(i == 0)
    def _():
        gather(slot_ref, 0)

    @pl.when(i + 1 < n)
    def _():
        gather(slot_next_ref, 1 - slot)

    for k in range(2):
        pltpu.make_async_copy(ys_ref.at[pl.ds(0, TOK_TILE)], buf.at[slot, k], sems.at[slot]).wait()

    gate = gate_ref[...]
    lane = _iota(gate.shape, 1)
    w0 = jnp.sum(jnp.where(lane == 0, gate, 0.0), axis=-1, keepdims=True)
    w1 = jnp.sum(jnp.where(lane == 1, gate, 0.0), axis=-1, keepdims=True)
    g2 = mod_ref[0][:, 5 * D_MODEL:6 * D_MODEL]
    y = x1_ref[...] + g2 * (w0 * buf[slot, 0] + w1 * buf[slot, 1])
    ms = jnp.mean(y * y, axis=-1, keepdims=True)
    y_ref[...] = y * lax.rsqrt(ms + NORM_EPS) * fg_ref[...]


def _combine_call(slots, x1, rgate, mod3, mod_row, fg, ys):
    nt = x1.shape[0]
    n_tiles = nt // TOK_TILE
    return pl.pallas_call(
        _combine_kernel,
        out_shape=jax.ShapeDtypeStruct((nt, D_MODEL), F32),
        grid=(n_tiles,),
        in_specs=[_route_spec(), _route_spec(1, n_tiles - 1),
                  pl.BlockSpec((TOK_TILE, D_MODEL), lambda i: (i, 0)),
                  pl.BlockSpec((TOK_TILE, LANES), lambda i: (i, 0)),
                  pl.BlockSpec((1, 1, 6 * D_MODEL), lambda i: (mod_row(i * TOK_TILE), 0, 0)),
                  pl.BlockSpec((1, D_MODEL), lambda i: (0, 0)),
                  pl.BlockSpec(memory_space=pl.ANY)],
        out_specs=pl.BlockSpec((TOK_TILE, D_MODEL), lambda i: (i, 0)),
        scratch_shapes=[pltpu.VMEM((2, 2, TOK_TILE, D_MODEL), F32), pltpu.SemaphoreType.DMA((2,))],
        compiler_params=_cparams(("arbitrary",)),
        name="combine",
    )(slots, slots, x1, rgate, mod3, fg, ys)


def kernel(x_prompt, x_sample, state_rwkv, state_gdn, c, c_ctx, ada_w, ada_b, norm1_g, norm2_g, w_in, w_out,
           rwkv_w0, rwkv_w2, rwkv_a0, rwkv_a2, rwkv_g2, rwkv_k_k, rwkv_k_a, rwkv_r_k, rwkv_lnx_g, rwkv_lnx_b,
           gdn_conv, gdn_a_log, gdn_dt_bias, gdn_norm_g, router_group_w, router_group_b, router_expert_w,
           router_expert_b, expert_gate, expert_up, expert_down, final_norm_g):
    assert ada_w.shape[0] == 1, "one layer"
    bp, tp, _ = x_prompt.shape
    bs, ts, _ = x_sample.shape
    lp = dict(rwkv_w0=rwkv_w0[0], rwkv_w2=rwkv_w2[0], rwkv_a0=rwkv_a0[0], rwkv_a2=rwkv_a2[0], rwkv_g2=rwkv_g2[0],
              rwkv_k_k=rwkv_k_k[0], rwkv_k_a=rwkv_k_a[0], rwkv_r_k=rwkv_r_k[0], rwkv_lnx_g=rwkv_lnx_g[0],
              rwkv_lnx_b=rwkv_lnx_b[0], gdn_conv=gdn_conv[0], gdn_a_log=gdn_a_log[0],
              gdn_dt_bias=gdn_dt_bias[0], gdn_norm_g=gdn_norm_g[0])
    rw_w = _rwkv_weights(lp)
    gd_w = _gdn_weights(lp)

    n_cond = 1 + bs
    cond = jnp.concatenate([c_ctx[None, :], c, jnp.zeros((16 - n_cond, D_MODEL), F32)], axis=0)
    mod = _mod_call(cond, ada_w[0], ada_b)
    mod3 = mod.reshape(16, 1, 6 * D_MODEL)

    in_cols = w_in.shape[2]
    w_in_bf = jnp.pad(w_in[0], ((0, 0), (0, RW + GW_PAD - in_cols))).astype(BF16)
    w_out_bf = w_out[0].astype(BF16)
    n1g = norm1_g.reshape(1, D_MODEL)
    n2g = norm2_g.reshape(1, D_MODEL)
    fg = final_norm_g.reshape(1, D_MODEL)
    rw = jnp.zeros((D_MODEL, LANES), F32).at[:, :N_GROUPS].set(router_group_w[0])
    rw = rw.at[:, N_GROUPS:N_GROUPS + N_EXPERTS].set(router_expert_w[0])
    rb = jnp.zeros((1, LANES), F32).at[0, :N_GROUPS].set(router_group_b[0])
    rb = rb.at[0, N_GROUPS:N_GROUPS + N_EXPERTS].set(router_expert_b[0])

    passes = [
        dict(x=x_prompt.reshape(bp * tp, D_MODEL), batch=bp, seq=tp, row_len=tp,
             mod_row=lambda tok: 0,
             s_r=None, s_g=None),
        dict(x=x_sample.reshape(bs * ts, D_MODEL), batch=bs, seq=ts, row_len=GRID_W,
             mod_row=lambda tok: 1 + tok // ts,
             s_r=state_rwkv[:, 0], s_g=state_gdn[:, 0]),
    ]

    for p in passes:
        proj_r, proj_g = _inproj_call(p["x"], mod3, p["mod_row"], n1g, w_in_bf)
        o_r, p["st_r"] = _rwkv_call(proj_r, p["s_r"], rw_w, batch=p["batch"], seq=p["seq"], nrow=SEQS_PER_STEP,
                                    hp=False)
        nrow_g = GDN_SEQS_PER_STEP_SHORT if p["seq"] <= 256 else SEQS_PER_STEP
        o_g, p["st_g"] = _gdn_call(proj_g, p["s_g"], gd_w, batch=p["batch"], seq=p["seq"], nrow=nrow_g,
                                   row_len=p["row_len"], hp=False)
        p["x1"], p["h2"], p["rid"], p["rgate"] = _outproj_call(
            p["x"], o_r, o_g, mod3, p["mod_row"], w_out_bf, n2g, rw, rb)

    cnt = jnp.zeros((8, LANES), jnp.int32)
    for p in passes:
        p["route"], cnt = _plan_call(p["rid"], cnt)
    counts = cnt[0, :N_EXPERTS]
    padded = (counts + MOE_ROWS - 1) // MOE_ROWS * MOE_ROWS
    ends = jnp.cumsum(padded)
    off = (ends - padded).astype(jnp.int32)
    nt_all = sum(p["rid"].shape[0] for p in passes)
    n_blocks = (2 * nt_all) // MOE_ROWS + N_EXPERTS
    n_rows = n_blocks * MOE_ROWS
    n_used = (ends[-1] // MOE_ROWS).astype(jnp.int32)
    blk_start = jnp.minimum(jnp.arange(n_blocks, dtype=jnp.int32), n_used - 1) * MOE_ROWS
    blk_e = jnp.minimum(jnp.sum((blk_start[:, None] >= ends[None, :]).astype(jnp.int32), axis=1), N_EXPERTS - 1)

    xs = None
    for p in passes:
        p["slots"] = _slots_call(off, p["route"])
        xs = _dispatch_call(p["slots"], p["h2"], xs, n_rows)
    ys = _experts_call(blk_e, n_used.reshape(1), xs, expert_gate[0], expert_up[0], expert_down[0])
    outs = [_combine_call(p["slots"], p["x1"], p["rgate"], mod3, p["mod_row"], fg, ys) for p in passes]

    y_prompt = outs[0].reshape(bp, tp, D_MODEL)
    y_sample = outs[1].reshape(bs, ts, D_MODEL)
    new_state_rwkv = passes[0]["st_r"][:, None]
    new_state_gdn = passes[0]["st_g"][:, None]
    return (y_prompt, y_sample, new_state_rwkv, new_state_gdn)
```

```python
import functools
import math

import jax
import jax.numpy as jnp
from jax import lax
from jax.experimental import pallas as pl
from jax.experimental.pallas import tpu as pltpu

F32 = jnp.float32
BF16 = jnp.bfloat16

D_MODEL = 1024
R_HEADS, R_HEAD = 8, 64
R_WIDTH = R_HEADS * R_HEAD
G_HEADS, G_HEAD = 4, 128
G_WIDTH = G_HEADS * G_HEAD
LNX_EPS = 64e-5
NORM_EPS = 1e-6
N_GROUPS, EXPERTS_PER_GROUP = 4, 8
N_EXPERTS = N_GROUPS * EXPERTS_PER_GROUP
D_EXPERT = 512
GRID_W = 64

CHUNK = 64
TOK_TILE = 256
OUT_TILE = 512
MOE_ROWS = 512
SEQS_PER_STEP = 2
GDN_SEQS_PER_STEP_SHORT = 4
LANES = 128
RW = 3 * R_WIDTH + 256
GW_PAD = 4 * G_WIDTH + LANES
VMEM_LIMIT = 56 * 1024 * 1024


def _cparams(sem):
    return pltpu.CompilerParams(dimension_semantics=sem, vmem_limit_bytes=VMEM_LIMIT)


def _sigmoid(x):
    return 1.0 / (1.0 + jnp.exp(-x))


def _silu(x):
    return x * _sigmoid(x)


def _softplus(x):
    return jnp.maximum(x, 0.0) + jnp.log(1.0 + jnp.exp(-jnp.abs(x)))


def _dg(a, b, ca, cb):
    return lax.dot_general(a, b, (((ca,), (cb,)), ((), ())), preferred_element_type=F32)


def _split2(x):
    hi = x.astype(BF16)
    lo = (x - hi.astype(F32)).astype(BF16)
    return hi, lo


def _split3(x):
    h1 = x.astype(BF16)
    r1 = x - h1.astype(F32)
    h2 = r1.astype(BF16)
    h3 = (r1 - h2.astype(F32)).astype(BF16)
    return h1, h2, h3


def _mm(a, b, ca=1, cb=0, hp=False):
    if not hp:
        return _dg(a.astype(BF16), b.astype(BF16), ca, cb)
    ah, al = _split2(a)
    bh, bl = _split2(b)
    return _dg(ah, bh, ca, cb) + (_dg(ah, bl, ca, cb) + _dg(al, bh, ca, cb))


def _mm_exact_lhs(m_bf, x, ca=1, cb=0, parts=3):
    if parts == 2:
        hi, lo = _split2(x)
        return _dg(m_bf, hi, ca, cb) + _dg(m_bf, lo, ca, cb)
    h1, h2, h3 = _split3(x)
    return _dg(m_bf, h1, ca, cb) + (_dg(m_bf, h2, ca, cb) + _dg(m_bf, h3, ca, cb))


def _mm_exact_rhs(x, m_bf):
    h1, h2, h3 = _split3(x)
    return _dg(h1, m_bf, 1, 0) + (_dg(h2, m_bf, 1, 0) + _dg(h3, m_bf, 1, 0))


def _group_sum(x, ones_bf):
    w = ones_bf.shape[0]
    hi, lo = _split2(x)
    return jnp.concatenate([_dg(hi[:, j:j + w], ones_bf, 1, 0) + _dg(lo[:, j:j + w], ones_bf, 1, 0)
                            for j in range(0, x.shape[1], w)], axis=1)


def _iota(shape, axis):
    return lax.broadcasted_iota(jnp.int32, shape, axis)


def _tri_masks(n, rev):
    t = _iota((n, n), 0)
    s = _iota((n, n), 1)
    if rev:
        return t < s, t <= s
    return t > s, t >= s


PACK = 4


def _level_masks(n):
    t = _iota((n, PACK * n), 0)
    u = _iota((n, PACK * n), 1) % n
    levels = []
    s = 1
    while s < n:
        levels.append((t // (2 * s) == u // (2 * s)) & (t // s != u // s))
        s *= 2
    diag = _iota((PACK * n, PACK * n), 0) // n == _iota((PACK * n, PACK * n), 1) // n
    return levels, diag


def _block_diag(x, diag):
    return jnp.where(diag, jnp.concatenate([x] * PACK, axis=0), 0.0).astype(BF16)


def _unit_tri_inverse(ms, masks):
    levels, diag = masks
    n = ms[0].shape[0]
    eye = (_iota((n, PACK * n), 0) == _iota((n, PACK * n), 1) % n).astype(F32)
    xs = [eye - jnp.where(levels[0], m, 0.0) for m in ms]
    for mask in levels[1:]:
        zs = [_dg(jnp.where(mask, m, 0.0).astype(BF16), _block_diag(x, diag), 1, 0) for m, x in zip(ms, xs)]
        xs = [x - _dg(x.astype(BF16), _block_diag(z, diag), 1, 0) for x, z in zip(xs, zs)]
    return xs


def _mod_kernel(c_ref, w_ref, b_ref, o_ref):
    s = _silu(c_ref[...])
    o_ref[...] = _mm(s, w_ref[...], hp=True) + b_ref[...]


def _mod_call(cond, ada_w, ada_b):
    rows = cond.shape[0]
    n = ada_w.shape[1]
    bn = 1024
    return pl.pallas_call(
        _mod_kernel,
        out_shape=jax.ShapeDtypeStruct((rows, n), F32),
        grid=(n // bn,),
        in_specs=[pl.BlockSpec((rows, D_MODEL), lambda j: (0, 0)),
                  pl.BlockSpec((D_MODEL, bn), lambda j: (0, j)),
                  pl.BlockSpec((1, bn), lambda j: (0, j))],
        out_specs=pl.BlockSpec((rows, bn), lambda j: (0, j)),
        compiler_params=_cparams(("arbitrary",)),
        name="mod",
    )(cond, ada_w, ada_b)


def _inproj_kernel(x_ref, mod_ref, g_ref, w_ref, or_ref, og_ref):
    x = x_ref[...]
    ms = jnp.mean(x * x, axis=-1, keepdims=True)
    y = x * lax.rsqrt(ms + NORM_EPS) * g_ref[...]
    m = mod_ref[0]
    h = y * (1.0 + m[:, D_MODEL:2 * D_MODEL]) + m[:, :D_MODEL]
    p = jnp.dot(h.astype(BF16), w_ref[...], preferred_element_type=F32)
    or_ref[...] = p[:, :RW].astype(BF16)
    og_ref[...] = p[:, RW:].astype(BF16)


def _inproj_call(x2, mod3, mod_row, norm_g, w_in_bf):
    nt = x2.shape[0]
    ncol = w_in_bf.shape[1]
    return pl.pallas_call(
        _inproj_kernel,
        out_shape=(jax.ShapeDtypeStruct((nt, RW), BF16),
                   jax.ShapeDtypeStruct((nt, GW_PAD), BF16)),
        grid=(nt // TOK_TILE,),
        in_specs=[pl.BlockSpec((TOK_TILE, D_MODEL), lambda i: (i, 0)),
                  pl.BlockSpec((1, 1, 6 * D_MODEL), lambda i: (mod_row(i * TOK_TILE), 0, 0)),
                  pl.BlockSpec((1, D_MODEL), lambda i: (0, 0)),
                  pl.BlockSpec((D_MODEL, ncol), lambda i: (0, 0))],
        out_specs=(pl.BlockSpec((TOK_TILE, RW), lambda i: (i, 0)),
                   pl.BlockSpec((TOK_TILE, GW_PAD), lambda i: (i, 0))),
        compiler_params=_cparams(("arbitrary",)),
        name="inproj",
    )(x2, mod3, norm_g, w_in_bf)


def _head_blocks(width, head):
    return (_iota((width, width), 0) // head == _iota((width, width), 1) // head).astype(BF16)


def _rwkv_kernel(p_ref, *refs, seq, nrow, hp, has_s0):
    s0_ref = refs[0] if has_s0 else None
    (w0_ref, wd_ref, a0_ref, wa_ref, g2_ref, kk_ref, ka_ref, rk_ref, lng_ref, lnb_ref,
     o_ref, st_ref, y_scr) = refs[1:] if has_s0 else refs
    C = CHUNK
    nc = seq // C
    blk = _head_blocks(2 * LANES, R_HEAD)
    k_k = kk_ref[...]
    k_a = ka_ref[...]
    r_k = rk_ref[...]
    exp_m05 = math.exp(-0.5)
    inv_masks = _level_masks(C)

    def lora_lhs(la):
        return jnp.where(_iota(la.shape, 1) < 64, jnp.tanh(la), la)

    def cols(n, rows, lo, hi):
        return p_ref[n, rows, lo:hi].astype(F32)

    assert R_HEAD == C
    PW = PACK * R_HEAD
    diag = inv_masks[1]
    pk_t = _iota((C, PW), 0)
    pk_s = _iota((C, PW), 1) % C
    dirs = []
    for d, rev in ((0, False), (1, True)):
        dirs.append(dict(d=d, rev=rev, tri=_tri_masks(C, rev)[1].astype(BF16),
                         strict=(pk_t < pk_s) if rev else (pk_t > pk_s),
                         incl=(pk_t <= pk_s) if rev else (pk_t >= pk_s)))
    hd = range(R_HEADS)
    sls = [slice(h * R_HEAD, (h + 1) * R_HEAD) for h in hd]

    def prep(n, dr, c):
        d, rev, tri = dr["d"], dr["rev"], dr["tri"]
        rows = pl.ds(pl.multiple_of(c * C, C), C)
        rr = cols(n, rows, 0, R_WIDTH)
        kx = cols(n, rows, R_WIDTH, 2 * R_WIDTH)
        vv = cols(n, rows, 2 * R_WIDTH, 3 * R_WIDTH)
        lhs = lora_lhs(cols(n, rows, 3 * R_WIDTH, 3 * R_WIDTH + LANES))
        logw = -exp_m05 * _sigmoid(w0_ref[d] + _mm(lhs, wd_ref[d]))
        a = _sigmoid(a0_ref[d] + _mm(lhs, wa_ref[d]))
        kkr = kx * k_k
        kk = kkr * lax.rsqrt(_group_sum(kkr * kkr, blk) + 1e-6)
        kd = kx * (1.0 + (a - 1.0) * k_a)
        bvec = kk * a
        gi = _mm_exact_lhs(tri, logw, parts=2)
        gtot = gi[0:1] if rev else gi[C - 1:C]
        en = jnp.exp(-gi)
        ee = jnp.exp(gtot - gi)
        return dict(rows=rows, vv=vv, rt=rr * jnp.exp(gi), at=-kk * jnp.exp(gi - logw), bt=bvec * en, kt=kd * en,
                    bh=bvec * ee, kh=kd * ee, gc=jnp.exp(gtot))

    seqs = range(nrow)
    packs = [(n, d, g) for n in seqs for d in (0, 1) for g in range(R_HEADS // PACK)]

    def step(i, carry):
        ops = {(n, d): prep(n, dirs[d], (nc - 1 - i) if d else i) for n in seqs for d in (0, 1)}
        pcol = lambda name, n, d, g: ops[n, d][name][:, g * PW:(g + 1) * PW]
        bd = lambda x: _block_diag(x, diag)
        ars = {k: jnp.concatenate([pcol("at", *k), pcol("rt", *k)], axis=0).astype(BF16) for k in packs}
        pb = {k: _dg(ars[k], bd(pcol("bt", *k)), 1, 1) for k in packs}
        pk = {k: _dg(ars[k], bd(pcol("kt", *k)), 1, 1) for k in packs}
        sts = {(n, d, g): carry[2 * n + d][:, g * PW:(g + 1) * PW] for n, d, g in packs}
        x0s = {k: _dg(ars[k], bd(sts[k]), 1, 1) for k in packs}
        bdv = {k: bd(pcol("vv", *k)) for k in packs}
        akv = {(n, d, g): _dg(jnp.where(dirs[d]["strict"], pk[n, d, g][:C], 0.0).astype(BF16), bdv[n, d, g], 1, 0)
               for n, d, g in packs}
        tms = dict(zip(packs, _unit_tri_inverse(
            [jnp.where(dirs[d]["strict"], -pb[n, d, g][:C], 0.0) for n, d, g in packs], inv_masks)))
        us = {k: _dg(tms[k].astype(BF16), bd(x0s[k][:C] + akv[k]), 1, 0) for k in packs}
        ys = {(n, d, g): x0s[n, d, g][C:] + _dg(
            jnp.concatenate([jnp.where(dirs[d]["incl"], pb[n, d, g][C:], 0.0),
                             jnp.where(dirs[d]["incl"], pk[n, d, g][C:], 0.0)], axis=1).astype(BF16),
            jnp.concatenate([bd(us[n, d, g]), bdv[n, d, g]], axis=0), 1, 0) for n, d, g in packs}
        snew = {}
        for n, d, g in packs:
            full = _dg(jnp.concatenate([us[n, d, g], pcol("vv", n, d, g)], axis=0).astype(BF16),
                       jnp.concatenate([pcol("bh", n, d, g), pcol("kh", n, d, g)], axis=0).astype(BF16), 0, 0)
            own = jnp.where(diag, full, 0.0)
            upd = own[0:R_HEAD]
            for j in range(1, PACK):
                upd = upd + own[j * R_HEAD:(j + 1) * R_HEAD]
            snew[n, d, g] = sts[n, d, g] * pcol("gc", n, d, g) + upd
        for n, d, g in packs:
            y_scr[n, d, ops[n, d]["rows"], g * PW:(g + 1) * PW] = ys[n, d, g]
        return tuple(jnp.concatenate([snew[n, d, g] for g in range(R_HEADS // PACK)], axis=1)
                     for n in seqs for d in (0, 1))

    def initial(n, d):
        if not has_s0:
            return jnp.zeros((R_HEAD, R_WIDTH), F32)
        return jnp.concatenate([s0_ref[n, d, h] for h in hd], axis=1)

    s_fin = lax.fori_loop(0, nc, step, tuple(initial(n, d) for n in seqs for d in (0, 1)))
    for n in seqs:
        for d in (0, 1):
            for h in hd:
                st_ref[n, d, h] = s_fin[2 * n + d][:, sls[h]]

    g2 = g2_ref[...]
    lng = lng_ref[...]
    lnb = lnb_ref[...]
    inv_n = 1.0 / R_HEAD
    fb = min(seq, 256)

    def finish(j, carry):
        rows = pl.ds(pl.multiple_of(j * fb, fb), fb)
        for n in seqs:
            y = y_scr[n, 0, rows, :] + y_scr[n, 1, rows, :]
            mu = _group_sum(y, blk) * inv_n
            yc = y - mu
            var = _group_sum(yc * yc, blk) * inv_n
            yn = yc * lax.rsqrt(var + LNX_EPS) * lng + lnb
            lhs = lora_lhs(cols(n, rows, 3 * R_WIDTH, 3 * R_WIDTH + LANES))
            a_sum = _sigmoid(a0_ref[0] + _mm(lhs, wa_ref[0])) + _sigmoid(a0_ref[1] + _mm(lhs, wa_ref[1]))
            rr = cols(n, rows, 0, R_WIDTH)
            kx = cols(n, rows, R_WIDTH, 2 * R_WIDTH)
            kd_sum = kx * (2.0 + (a_sum - 2.0) * k_a)
            bonus = _group_sum(rr * kd_sum * r_k, blk) * cols(n, rows, 2 * R_WIDTH, 3 * R_WIDTH)
            gl = cols(n, rows, 3 * R_WIDTH + LANES, 3 * R_WIDTH + 2 * LANES)
            gate = _mm(_sigmoid(gl), g2)
            o_ref[n, rows, :] = (yn + bonus) * gate
        return carry

    lax.fori_loop(0, seq // fb, finish, 0)


def _rwkv_call(proj_r, s0, w, *, batch, seq, nrow, hp):
    p3 = proj_r.reshape(batch, seq, RW)
    kern = functools.partial(_rwkv_kernel, seq=seq, nrow=nrow, hp=hp, has_s0=s0 is not None)
    full = lambda a: pl.BlockSpec(a.shape, lambda b: (0,) * a.ndim)
    ws = [w["w0"], w["wd"], w["a0"], w["wa"], w["g2"], w["k_k"], w["k_a"], w["r_k"], w["lnx_g"], w["lnx_b"]]
    state_spec = pl.BlockSpec((nrow, 2, R_HEADS, R_HEAD, R_HEAD), lambda b: (b, 0, 0, 0, 0))
    states = [] if s0 is None else [s0]
    o, st = pl.pallas_call(
        kern,
        out_shape=(jax.ShapeDtypeStruct((batch, seq, R_WIDTH), F32),
                   jax.ShapeDtypeStruct((batch, 2, R_HEADS, R_HEAD, R_HEAD), F32)),
        grid=(batch // nrow,),
        in_specs=[pl.BlockSpec((nrow, seq, RW), lambda b: (b, 0, 0))] + [state_spec] * len(states)
        + [full(a) for a in ws],
        out_specs=(pl.BlockSpec((nrow, seq, R_WIDTH), lambda b: (b, 0, 0)), state_spec),
        scratch_shapes=[pltpu.VMEM((nrow, 2, seq, R_WIDTH), F32)],
        compiler_params=_cparams(("arbitrary",)),
        name="rwkv_hp" if hp else "rwkv",
    )(p3, *states, *ws)
    return o.reshape(batch * seq, R_WIDTH), st


def _rwkv_weights(lp):
    zeros = jnp.zeros((2, 64, R_WIDTH), F32)
    row = lambda a: jnp.asarray(a, F32).reshape(1, R_WIDTH)
    return {
        "w0": jnp.asarray(lp["rwkv_w0"], F32).reshape(2, 1, R_WIDTH),
        "wd": jnp.concatenate([jnp.asarray(lp["rwkv_w2"], F32), zeros], axis=1),
        "a0": jnp.asarray(lp["rwkv_a0"], F32).reshape(2, 1, R_WIDTH),
        "wa": jnp.concatenate([zeros, jnp.asarray(lp["rwkv_a2"], F32)], axis=1),
        "g2": jnp.asarray(lp["rwkv_g2"], F32),
        "k_k": row(lp["rwkv_k_k"]), "k_a": row(lp["rwkv_k_a"]), "r_k": row(lp["rwkv_r_k"]),
        "lnx_g": row(lp["rwkv_lnx_g"]), "lnx_b": row(lp["rwkv_lnx_b"]),
    }


def _gdn_kernel(p_ref, *refs, seq, nrow, row_len, hp, has_s0):
    s0_ref = refs[0] if has_s0 else None
    (cw_ref, alog_ref, dtb_ref, ng_ref, o_ref, st_ref, q_scr, k_scr, v_scr, o_scr) = refs[1:] if has_s0 else refs
    C = CHUNK
    nc = seq // C
    fb = min(seq, 256)
    W = G_WIDTH
    q_scale = G_HEAD ** -0.5
    seqs = range(nrow)
    hd = range(G_HEADS)
    hss = [slice(h * G_HEAD, (h + 1) * G_HEAD) for h in hd]

    cw0 = cw_ref[0:1, :]
    cw1 = cw_ref[1:2, :]
    cw2 = cw_ref[2:3, :]
    tpos = _iota((fb, 1), 0) % row_len
    is_first = tpos == 0
    is_last = tpos == row_len - 1

    def prep(j, carry):
        rows = pl.ds(pl.multiple_of(j * fb, fb), fb)
        for n in seqs:
            x = p_ref[n, rows, 0:3 * W].astype(F32)
            xm = jnp.where(is_first, 0.0, pltpu.roll(x, 1, 0))
            xp = jnp.where(is_last, 0.0, pltpu.roll(x, fb - 1, 0))
            y = _silu(cw0 * xm + cw1 * x + cw2 * xp)
            for h in hd:
                qh = y[:, hss[h]]
                kh = y[:, W + h * G_HEAD:W + (h + 1) * G_HEAD]
                q_scr[n, rows, hss[h]] = qh * lax.rsqrt(jnp.sum(qh * qh, axis=-1, keepdims=True) + 1e-6) * q_scale
                k_scr[n, rows, hss[h]] = kh * lax.rsqrt(jnp.sum(kh * kh, axis=-1, keepdims=True) + 1e-6)
            v_scr[n, rows, :] = y[:, 2 * W:3 * W]
        return carry

    lax.fori_loop(0, seq // fb, prep, 0)

    assert G_HEADS == PACK
    inv_masks = _level_masks(C)
    alog = alog_ref[...]
    dtb = dtb_ref[...]
    sel_row = _iota((LANES, W), 0)
    sel_head = _iota((LANES, W), 1) // G_HEAD
    pk_row = _iota((LANES, PACK * C), 0)
    pk_head = _iota((LANES, PACK * C), 1) // C
    pk_t = _iota((C, PACK * C), 0)
    pk_s = _iota((C, PACK * C), 1) % C

    dirs = []
    for d, rev in ((0, False), (1, True)):
        dirs.append(dict(
            d=d, rev=rev, tri=_tri_masks(C, rev)[1].astype(BF16),
            strict=(pk_t < pk_s) if rev else (pk_t > pk_s),
            incl=(pk_t <= pk_s) if rev else (pk_t >= pk_s),
            upto=(pk_t >= pk_s) if rev else (pk_t <= pk_s),
            e_beta=(sel_row == 4 * d + sel_head).astype(BF16),
            e_g=(sel_row == 8 + 4 * d + sel_head).astype(BF16),
            e_g_pk=(pk_row == 8 + 4 * d + pk_head).astype(BF16)))
    items = [(n, d, h) for n in seqs for d in (0, 1) for h in hd]
    packs = [(n, d) for n in seqs for d in (0, 1)]

    def prep_dir(n, dr, c):
        rows = pl.ds(pl.multiple_of(c * C, C), C)
        ba = p_ref[n, rows, 4 * W:4 * W + LANES].astype(F32)
        gval = -jnp.exp(alog) * _softplus(ba + dtb)
        beta_b = _mm_exact_rhs(_sigmoid(ba), dr["e_beta"])
        gcum = _mm_exact_lhs(dr["tri"], _mm_exact_rhs(gval, dr["e_g"]))
        g_pk = _mm_exact_rhs(gval, dr["e_g_pk"])
        g_col = _mm_exact_lhs(dr["tri"], g_pk)
        g_row = jnp.sum(jnp.where(dr["upto"], g_pk, 0.0), axis=0, keepdims=True)
        return dict(rows=rows, beta=beta_b, gcum=gcum,
                    decay=jnp.where(dr["incl"], jnp.exp(jnp.where(dr["incl"], g_col - g_row, 0.0)), 0.0),
                    q=q_scr[n, rows, :], k=k_scr[n, rows, :], v=v_scr[n, rows, :])

    def step(i, carry):
        ops = {(n, d): prep_dir(n, dirs[d], (nc - 1 - i) if d else i) for n, d in packs}
        hcol = lambda name, n, d, h: ops[n, d][name][:, hss[h]]
        qhs = {k: hcol("q", *k) for k in items}
        khs = {k: hcol("k", *k) for k in items}
        gcs = {k: hcol("gcum", *k) for k in items}
        bbs = {k: hcol("beta", *k) for k in items}
        kbs = {k: khs[k] * bbs[k] for k in items}
        kks = {k: _mm(jnp.concatenate([kbs[k], qhs[k]], axis=0), khs[k], 1, 1, hp=hp) for k in items}
        kka = {(n, d): jnp.concatenate([kks[n, d, h][:C] for h in hd], axis=1) for n, d in packs}
        tcat = dict(zip(packs, _unit_tri_inverse(
            [jnp.where(dirs[d]["strict"], kka[n, d] * ops[n, d]["decay"], 0.0) for n, d in packs], inv_masks)))
        egs = {k: jnp.exp(gcs[k]) for k in items}
        uws = {(n, d, h): _mm(tcat[n, d][:, h * C:(h + 1) * C],
                              jnp.concatenate([hcol("v", n, d, h) * bbs[n, d, h], kbs[n, d, h] * egs[n, d, h]], axis=1),
                              hp=hp) for n, d, h in items}
        shs = {(n, d, h): carry[2 * n + d][:, hss[h]] for n, d, h in items}
        wss = {k: _mm(jnp.concatenate([uws[k][:, G_HEAD:], qhs[k] * egs[k]], axis=0), shs[k], hp=hp) for k in items}
        vns = {k: uws[k][:, :G_HEAD] - wss[k][:C] for k in items}
        outs = {(n, d, h): wss[n, d, h][C:]
                + _mm(kks[n, d, h][C:] * ops[n, d]["decay"][:, h * C:(h + 1) * C], vns[n, d, h]) for n, d, h in items}
        snew = {}
        for n, d, h in items:
            glast = gcs[n, d, h][0:1] if dirs[d]["rev"] else gcs[n, d, h][C - 1:C]
            k_dec = khs[n, d, h] * jnp.exp(glast - gcs[n, d, h])
            snew[n, d, h] = shs[n, d, h] * jnp.exp(glast) + _mm(k_dec, vns[n, d, h], 0, 0, hp=hp)
        for n, d in packs:
            o_scr[n, d, ops[n, d]["rows"], :] = jnp.concatenate([outs[n, d, h] for h in hd], axis=1)
        return tuple(jnp.concatenate([snew[n, d, h] for h in hd], axis=1) for n, d in packs)

    def initial(n, d):
        if not has_s0:
            return jnp.zeros((G_HEAD, W), F32)
        return jnp.concatenate([s0_ref[n, d, h] for h in hd], axis=1)

    s_fin = lax.fori_loop(0, nc, step, tuple(initial(n, d) for n, d in packs))
    for j, (n, d) in enumerate(packs):
        for h in hd:
            st_ref[n, d, h] = s_fin[j][:, hss[h]]

    ng = ng_ref[...]

    def finish(j, carry):
        rows = pl.ds(pl.multiple_of(j * fb, fb), fb)
        for n in seqs:
            o = o_scr[n, 0, rows, :] + o_scr[n, 1, rows, :]
            z = p_ref[n, rows, 3 * W:4 * W].astype(F32)
            for h in hd:
                oh = o[:, hss[h]]
                ms = jnp.mean(oh * oh, axis=-1, keepdims=True)
                o_ref[n, rows, hss[h]] = oh * lax.rsqrt(ms + NORM_EPS) * ng * _silu(z[:, hss[h]])
        return carry

    lax.fori_loop(0, seq // fb, finish, 0)


def _gdn_call(proj_g, s0, w, *, batch, seq, nrow, row_len, hp):
    p3 = proj_g.reshape(batch, seq, GW_PAD)
    kern = functools.partial(_gdn_kernel, seq=seq, nrow=nrow, row_len=row_len, hp=hp, has_s0=s0 is not None)
    full = lambda a: pl.BlockSpec(a.shape, lambda b: (0,) * a.ndim)
    ws = [w["conv"], w["alog"], w["dtb"], w["norm_g"]]
    state_spec = pl.BlockSpec((nrow, 2, G_HEADS, G_HEAD, G_HEAD), lambda b: (b, 0, 0, 0, 0))
    states = [] if s0 is None else [s0]
    o, st = pl.pallas_call(
        kern,
        out_shape=(jax.ShapeDtypeStruct((batch, seq, G_WIDTH), F32),
                   jax.ShapeDtypeStruct((batch, 2, G_HEADS, G_HEAD, G_HEAD), F32)),
        grid=(batch // nrow,),
        in_specs=[pl.BlockSpec((nrow, seq, GW_PAD), lambda b: (b, 0, 0))] + [state_spec] * len(states)
        + [full(a) for a in ws],
        out_specs=(pl.BlockSpec((nrow, seq, G_WIDTH), lambda b: (b, 0, 0)), state_spec),
        scratch_shapes=[pltpu.VMEM((nrow, seq, G_WIDTH), F32)] * 3 + [pltpu.VMEM((nrow, 2, seq, G_WIDTH), F32)],
        compiler_params=_cparams(("arbitrary",)),
        name="gdn_hp" if hp else "gdn",
    )(p3, *states, *ws)
    return o.reshape(batch * seq, G_WIDTH), st


def _gdn_weights(lp):
    lanes = jnp.zeros((1, LANES), F32)
    return {
        "conv": jnp.asarray(lp["gdn_conv"], F32),
        "alog": lanes.at[0, 8:16].set(jnp.asarray(lp["gdn_a_log"], F32).reshape(8)),
        "dtb": lanes.at[0, 8:16].set(jnp.asarray(lp["gdn_dt_bias"], F32).reshape(8)),
        "norm_g": jnp.asarray(lp["gdn_norm_g"], F32).reshape(1, G_HEAD),
    }


def _outproj_kernel(x_ref, or_ref, og_ref, mod_ref, wo_ref, n2_ref, rw_ref, rb_ref,
                    x1_ref, h2_ref, rid_ref, rgate_ref):
    m = mod_ref[0]
    g1 = m[:, 2 * D_MODEL:3 * D_MODEL]
    sh2 = m[:, 3 * D_MODEL:4 * D_MODEL]
    sc2 = m[:, 4 * D_MODEL:5 * D_MODEL]
    mix = (jnp.dot(or_ref[...].astype(BF16), wo_ref[0:R_WIDTH, :], preferred_element_type=F32)
           + jnp.dot(og_ref[...].astype(BF16), wo_ref[R_WIDTH:, :], preferred_element_type=F32))
    x1 = x_ref[...] + g1 * mix
    x1_ref[...] = x1
    ms = jnp.mean(x1 * x1, axis=-1, keepdims=True)
    h2 = x1 * lax.rsqrt(ms + NORM_EPS) * n2_ref[...] * (1.0 + sc2) + sh2
    h2_ref[...] = h2

    logits = _mm(h2, rw_ref[...], hp=True) + rb_ref[...]
    lane = _iota(logits.shape, 1)
    neg = jnp.float32(-1e30)
    big = jnp.int32(1 << 20)

    def first_argmax(v):
        mx = jnp.max(v, axis=-1, keepdims=True)
        idx = jnp.min(jnp.where(v == mx, lane, big), axis=-1, keepdims=True)
        return mx, idx

    lg = jnp.where(lane < N_GROUPS, logits, neg)
    mg, grp = first_argmax(lg)
    p_grp = 1.0 / jnp.sum(jnp.where(lane < N_GROUPS, jnp.exp(lg - mg), 0.0), axis=-1, keepdims=True)
    in_grp = (lane >= N_GROUPS) & (lane < N_GROUPS + N_EXPERTS) & ((lane - N_GROUPS) // EXPERTS_PER_GROUP == grp)
    le = jnp.where(in_grp, logits, neg)
    m1, i1 = first_argmax(le)
    m2, i2 = first_argmax(jnp.where(lane == i1, neg, le))
    e2 = jnp.exp(m2 - m1)
    w1 = p_grp / (1.0 + e2)
    w2 = p_grp * e2 / (1.0 + e2)
    rid_ref[...] = jnp.where(lane == 0, i1 - N_GROUPS, jnp.where(lane == 1, i2 - N_GROUPS, 0))
    rgate_ref[...] = jnp.where(lane == 0, w1, jnp.where(lane == 1, w2, 0.0))


def _outproj_call(x2, o_r, o_g, mod3, mod_row, wo_bf, n2g, rw, rb):
    nt = x2.shape[0]
    tile = lambda w: pl.BlockSpec((OUT_TILE, w), lambda i: (i, 0))
    const = lambda a: pl.BlockSpec(a.shape, lambda i: (0,) * a.ndim)
    return pl.pallas_call(
        _outproj_kernel,
        out_shape=(jax.ShapeDtypeStruct((nt, D_MODEL), F32), jax.ShapeDtypeStruct((nt, D_MODEL), F32),
                   jax.ShapeDtypeStruct((nt, LANES), jnp.int32), jax.ShapeDtypeStruct((nt, LANES), F32)),
        grid=(nt // OUT_TILE,),
        in_specs=[tile(D_MODEL), tile(R_WIDTH), tile(G_WIDTH),
                  pl.BlockSpec((1, 1, 6 * D_MODEL), lambda i: (mod_row(i * OUT_TILE), 0, 0)),
                  const(wo_bf), const(n2g), const(rw), const(rb)],
        out_specs=(tile(D_MODEL), tile(D_MODEL), tile(LANES), tile(LANES)),
        compiler_params=_cparams(("arbitrary",)),
        name="outproj",
    )(x2, o_r, o_g, mod3, wo_bf, n2g, rw, rb)


PLAN_TILE = 1024


def _plan_kernel(rid_ref, cnt_in_ref, route_ref, cnt_ref, carry):
    i = pl.program_id(0)

    @pl.when(i == 0)
    def _():
        carry[...] = cnt_in_ref[...].astype(F32)

    rid = rid_ref[...]
    lane = _iota(rid.shape, 1)
    e0 = jnp.sum(jnp.where(lane == 0, rid, 0), axis=-1, keepdims=True)
    e1 = jnp.sum(jnp.where(lane == 1, rid, 0), axis=-1, keepdims=True)
    oh0 = (lane == e0).astype(F32)
    oh1 = (lane == e1).astype(F32)
    oh = oh0 + oh1
    n = rid.shape[0]
    earlier = (_iota((n, n), 0) > _iota((n, n), 1)).astype(BF16)
    before = jnp.dot(earlier, oh.astype(BF16), preferred_element_type=F32) + carry[0:1, :]
    r0 = jnp.sum(oh0 * before, axis=-1, keepdims=True)
    r1 = jnp.sum(oh1 * before, axis=-1, keepdims=True)
    cols = jnp.where(lane == 0, e0.astype(F32), jnp.where(lane == 1, e1.astype(F32),
                                                          jnp.where(lane == 2, r0, jnp.where(lane == 3, r1, 0.0))))
    route_ref[...] = jnp.transpose(cols)[0:8, :].astype(jnp.int32)
    total = carry[0:1, :] + jnp.sum(oh, axis=0, keepdims=True)
    carry[...] = jnp.broadcast_to(total, carry.shape)
    cnt_ref[...] = jnp.broadcast_to(total, cnt_ref.shape).astype(jnp.int32)


def _plan_call(rid, cnt_in):
    nt = rid.shape[0]
    return pl.pallas_call(
        _plan_kernel,
        out_shape=(jax.ShapeDtypeStruct((8, nt), jnp.int32), jax.ShapeDtypeStruct((8, LANES), jnp.int32)),
        grid=(nt // PLAN_TILE,),
        in_specs=[pl.BlockSpec((PLAN_TILE, LANES), lambda i: (i, 0)), pl.BlockSpec((8, LANES), lambda i: (0, 0))],
        out_specs=(pl.BlockSpec((8, PLAN_TILE), lambda i: (0, i)), pl.BlockSpec((8, LANES), lambda i: (0, 0))),
        scratch_shapes=[pltpu.VMEM((8, LANES), F32)],
        compiler_params=_cparams(("arbitrary",)),
        name="plan",
    )(rid, cnt_in)


def _slots_kernel(off_ref, route_ref, slot_ref):
    route = route_ref[...]
    ids = jnp.concatenate([route[0:2], route[0:2], route[0:2], route[0:2]], axis=0)
    rank = jnp.concatenate([route[2:4], route[2:4], route[2:4], route[2:4]], axis=0)
    base = jnp.zeros_like(ids)
    for e in range(N_EXPERTS):
        base = jnp.where(ids == e, off_ref[e], base)
    slot_ref[...] = base + rank


def _slots_call(off, route):
    nt = route.shape[1]
    tile = min(nt, 2048)
    return pl.pallas_call(
        _slots_kernel,
        out_shape=jax.ShapeDtypeStruct((8, nt), jnp.int32),
        grid=(nt // tile,),
        in_specs=[pl.BlockSpec(memory_space=pltpu.SMEM), pl.BlockSpec((8, tile), lambda i: (0, i))],
        out_specs=pl.BlockSpec((8, tile), lambda i: (0, i)),
        compiler_params=_cparams(("arbitrary",)),
        name="slots",
    )(off, route)


def _dispatch_kernel(slot_ref, h2a_ref, h2b_ref, pad_ref, xs_ref, zero_scr, sem, *, tiles_a, n_rows):
    i = pl.program_id(0)

    if True:
        @pl.when(i == 0)
        def _():
            zero_scr[...] = jnp.zeros_like(zero_scr)
            nblk = n_rows // MOE_ROWS
            aligned = [s for s in (256, 128, 64, 32, 16, 8) if s < MOE_ROWS]

            def pad_copies(act):
                for e in range(N_EXPERTS):
                    pos = pad_ref[0, e]
                    pad = pad_ref[1, e]
                    for s in aligned:
                        hit = (pad & s) != 0
                        first = pos - s

                        @pl.when(hit)
                        def _(first=first, s=s):
                            act(pltpu.make_async_copy(zero_scr.at[pl.ds(0, s)],
                                                      xs_ref.at[pl.ds(pl.multiple_of(first, 8), s)], sem))

                        pos = jnp.where(hit, first, pos)
                    rem = pad & 7
                    for j in range(7):
                        @pl.when(j < rem)
                        def _(pos=pos, j=j):
                            act(pltpu.make_async_copy(zero_scr.at[pl.ds(0, 1)], xs_ref.at[pl.ds(pos - 1 - j, 1)], sem))

            def block_copy(b):
                return pltpu.make_async_copy(zero_scr, xs_ref.at[pl.ds(pl.multiple_of(b * MOE_ROWS, MOE_ROWS),
                                                                       MOE_ROWS)], sem)

            def start(b, c):
                block_copy(b).start()
                return c

            def wait(b, c):
                block_copy(b).wait()
                return c

            pad_copies(lambda cp: cp.start())
            lax.fori_loop(pad_ref[2, 0], nblk, start, 0)
            pad_copies(lambda cp: cp.wait())
            lax.fori_loop(pad_ref[2, 0], nblk, wait, 0)

    def scatter_tile(h2_ref):
        def start(t, c):
            for k in range(2):
                pltpu.make_async_copy(h2_ref.at[pl.ds(t, 1)], xs_ref.at[pl.ds(slot_ref[k, t], 1)], sem).start()
            return c

        lax.fori_loop(0, TOK_TILE, start, 0, unroll=8)
        for _ in range(2):
            pltpu.make_async_copy(h2_ref, xs_ref.at[pl.ds(0, TOK_TILE)], sem).wait()

    @pl.when(i < tiles_a)
    def _():
        scatter_tile(h2a_ref)

    @pl.when(i >= tiles_a)
    def _():
        scatter_tile(h2b_ref)


def _route_spec(shift=0, last=None):
    if shift == 0:
        return pl.BlockSpec((8, TOK_TILE), lambda i: (0, i), memory_space=pltpu.SMEM)
    return pl.BlockSpec((8, TOK_TILE), lambda i: (0, jnp.minimum(i + shift, last)), memory_space=pltpu.SMEM)


def _dispatch_call(slots, h2a, h2b, n_rows, pads):
    tiles_a = h2a.shape[0] // TOK_TILE
    tiles_b = h2b.shape[0] // TOK_TILE
    kern = functools.partial(_dispatch_kernel, tiles_a=tiles_a, n_rows=n_rows)
    return pl.pallas_call(
        kern,
        out_shape=jax.ShapeDtypeStruct((n_rows, D_MODEL), F32),
        grid=(tiles_a + tiles_b,),
        in_specs=[_route_spec(),
                  pl.BlockSpec((TOK_TILE, D_MODEL), lambda i: (jnp.minimum(i, tiles_a - 1), 0)),
                  pl.BlockSpec((TOK_TILE, D_MODEL), lambda i: (jnp.maximum(i - tiles_a, 0), 0)),
                  pl.BlockSpec(memory_space=pltpu.SMEM)],
        out_specs=pl.BlockSpec(memory_space=pl.ANY),
        scratch_shapes=[pltpu.VMEM((MOE_ROWS, D_MODEL), F32), pltpu.SemaphoreType.DMA(())],
        compiler_params=_cparams(("arbitrary",)),
        name="dispatch",
    )(slots, h2a, h2b, pads)


def _experts_kernel(be_ref, nu_ref, xs_ref, wg_ref, wu_ref, wd_ref, ys_ref, wg_bf, wu_bf, wd_bf):
    b = pl.program_id(0)
    used = b < nu_ref[0]

    @pl.when(used)
    def _():
        prev = be_ref[jnp.maximum(b - 1, 0)]

        x = xs_ref[...]
        g = jnp.dot(x, wg_ref[0], preferred_element_type=F32)
        u = jnp.dot(x, wu_ref[0], preferred_element_type=F32)
        h = _silu(g) * u
        ys_ref[...] = jnp.dot(h, wd_ref[0], preferred_element_type=F32)

    @pl.when(jnp.logical_not(used))
    def _():
        ys_ref[...] = jnp.zeros_like(ys_ref)


def _experts_call(blk_e, n_used, xs, wg, wu, wd):
    n_rows = xs.shape[0]
    nb = n_rows // MOE_ROWS
    grid_spec = pltpu.PrefetchScalarGridSpec(
        num_scalar_prefetch=2,
        grid=(nb,),
        in_specs=[pl.BlockSpec((MOE_ROWS, D_MODEL), lambda b, be, nu: (jnp.minimum(b, nu[0] - 1), 0)),
                  pl.BlockSpec((1, D_MODEL, D_EXPERT), lambda b, be, nu: (be[b], 0, 0)),
                  pl.BlockSpec((1, D_MODEL, D_EXPERT), lambda b, be, nu: (be[b], 0, 0)),
                  pl.BlockSpec((1, D_EXPERT, D_MODEL), lambda b, be, nu: (be[b], 0, 0))],
        out_specs=pl.BlockSpec((MOE_ROWS, D_MODEL), lambda b, be, nu: (b, 0)),
        scratch_shapes=[pltpu.VMEM((D_MODEL, D_EXPERT), BF16), pltpu.VMEM((D_MODEL, D_EXPERT), BF16),
                        pltpu.VMEM((D_EXPERT, D_MODEL), BF16)],
    )
    return pl.pallas_call(
        _experts_kernel,
        out_shape=jax.ShapeDtypeStruct((n_rows, D_MODEL), F32),
        grid_spec=grid_spec,
        compiler_params=_cparams(("arbitrary",)),
        name="experts",
    )(blk_e, n_used, xs, wg, wu, wd)


def _combine_kernel(slot_ref, slot_next_ref, x1_ref, gate_ref, mod_ref, fg_ref, ys_ref, y_ref, buf, sems):
    i = pl.program_id(0)
    n = pl.num_programs(0)
    slot = i % 2

    def gather(slots, s):
        def start(t, c):
            for k in range(2):
                pltpu.make_async_copy(ys_ref.at[pl.ds(slots[k, t], 1)], buf.at[s, k, pl.ds(t, 1)], sems.at[s]).start()
            return c

        lax.fori_loop(0, TOK_TILE, start, 0, unroll=8)

    @pl.when(i == 0)
    def _():
        gather(slot_ref, 0)

    @pl.when(i + 1 < n)
    def _():
        gather(slot_next_ref, 1 - slot)

    for k in range(2):
        pltpu.make_async_copy(ys_ref.at[pl.ds(0, TOK_TILE)], buf.at[slot, k], sems.at[slot]).wait()

    gate = gate_ref[...]
    lane = _iota(gate.shape, 1)
    w0 = jnp.sum(jnp.where(lane == 0, gate, 0.0), axis=-1, keepdims=True)
    w1 = jnp.sum(jnp.where(lane == 1, gate, 0.0), axis=-1, keepdims=True)
    g2 = mod_ref[0][:, 5 * D_MODEL:6 * D_MODEL]
    y = x1_ref[...] + g2 * (w0 * buf[slot, 0] + w1 * buf[slot, 1])
    ms = jnp.mean(y * y, axis=-1, keepdims=True)
    y_ref[...] = y * lax.rsqrt(ms + NORM_EPS) * fg_ref[...]


def _combine_call(slots, x1, rgate, mod3, mod_row, fg, ys):
    nt = x1.shape[0]
    n_tiles = nt // TOK_TILE
    return pl.pallas_call(
        _combine_kernel,
        out_shape=jax.ShapeDtypeStruct((nt, D_MODEL), F32),
        grid=(n_tiles,),
        in_specs=[_route_spec(), _route_spec(1, n_tiles - 1),
                  pl.BlockSpec((TOK_TILE, D_MODEL), lambda i: (i, 0)),
                  pl.BlockSpec((TOK_TILE, LANES), lambda i: (i, 0)),
                  pl.BlockSpec((1, 1, 6 * D_MODEL), lambda i: (mod_row(i * TOK_TILE), 0, 0)),
                  pl.BlockSpec((1, D_MODEL), lambda i: (0, 0)),
                  pl.BlockSpec(memory_space=pl.ANY)],
        out_specs=pl.BlockSpec((TOK_TILE, D_MODEL), lambda i: (i, 0)),
        scratch_shapes=[pltpu.VMEM((2, 2, TOK_TILE, D_MODEL), F32), pltpu.SemaphoreType.DMA((2,))],
        compiler_params=_cparams(("arbitrary",)),
        name="combine",
    )(slots, slots, x1, rgate, mod3, fg, ys)


def kernel(x_prompt, x_sample, state_rwkv, state_gdn, c, c_ctx, ada_w, ada_b, norm1_g, norm2_g, w_in, w_out,
           rwkv_w0, rwkv_w2, rwkv_a0, rwkv_a2, rwkv_g2, rwkv_k_k, rwkv_k_a, rwkv_r_k, rwkv_lnx_g, rwkv_lnx_b,
           gdn_conv, gdn_a_log, gdn_dt_bias, gdn_norm_g, router_group_w, router_group_b, router_expert_w,
           router_expert_b, expert_gate, expert_up, expert_down, final_norm_g):
    assert ada_w.shape[0] == 1, "one layer"
    bp, tp, _ = x_prompt.shape
    bs, ts, _ = x_sample.shape
    lp = dict(rwkv_w0=rwkv_w0[0], rwkv_w2=rwkv_w2[0], rwkv_a0=rwkv_a0[0], rwkv_a2=rwkv_a2[0], rwkv_g2=rwkv_g2[0],
              rwkv_k_k=rwkv_k_k[0], rwkv_k_a=rwkv_k_a[0], rwkv_r_k=rwkv_r_k[0], rwkv_lnx_g=rwkv_lnx_g[0],
              rwkv_lnx_b=rwkv_lnx_b[0], gdn_conv=gdn_conv[0], gdn_a_log=gdn_a_log[0],
              gdn_dt_bias=gdn_dt_bias[0], gdn_norm_g=gdn_norm_g[0])
    rw_w = _rwkv_weights(lp)
    gd_w = _gdn_weights(lp)

    n_cond = 1 + bs
    cond = jnp.concatenate([c_ctx[None, :], c, jnp.zeros((16 - n_cond, D_MODEL), F32)], axis=0)
    mod = _mod_call(cond, ada_w[0], ada_b)
    mod3 = mod.reshape(16, 1, 6 * D_MODEL)

    in_cols = w_in.shape[2]
    w_in_bf = jnp.pad(w_in[0], ((0, 0), (0, RW + GW_PAD - in_cols))).astype(BF16)
    w_out_bf = w_out[0].astype(BF16)
    n1g = norm1_g.reshape(1, D_MODEL)
    n2g = norm2_g.reshape(1, D_MODEL)
    fg = final_norm_g.reshape(1, D_MODEL)
    rw = jnp.zeros((D_MODEL, LANES), F32).at[:, :N_GROUPS].set(router_group_w[0])
    rw = rw.at[:, N_GROUPS:N_GROUPS + N_EXPERTS].set(router_expert_w[0])
    rb = jnp.zeros((1, LANES), F32).at[0, :N_GROUPS].set(router_group_b[0])
    rb = rb.at[0, N_GROUPS:N_GROUPS + N_EXPERTS].set(router_expert_b[0])

    passes = [
        dict(x=x_prompt.reshape(bp * tp, D_MODEL), batch=bp, seq=tp, row_len=tp,
             mod_row=lambda tok: 0,
             s_r=None, s_g=None),
        dict(x=x_sample.reshape(bs * ts, D_MODEL), batch=bs, seq=ts, row_len=GRID_W,
             mod_row=lambda tok: 1 + tok // ts,
             s_r=state_rwkv[:, 0], s_g=state_gdn[:, 0]),
    ]

    for p in passes:
        proj_r, proj_g = _inproj_call(p["x"], mod3, p["mod_row"], n1g, w_in_bf)
        o_r, p["st_r"] = _rwkv_call(proj_r, p["s_r"], rw_w, batch=p["batch"], seq=p["seq"], nrow=SEQS_PER_STEP,
                                    hp=False)
        nrow_g = GDN_SEQS_PER_STEP_SHORT if p["seq"] <= 256 else SEQS_PER_STEP
        o_g, p["st_g"] = _gdn_call(proj_g, p["s_g"], gd_w, batch=p["batch"], seq=p["seq"], nrow=nrow_g,
                                   row_len=p["row_len"], hp=False)
        p["x1"], p["h2"], p["rid"], p["rgate"] = _outproj_call(
            p["x"], o_r, o_g, mod3, p["mod_row"], w_out_bf, n2g, rw, rb)

    cnt = jnp.zeros((8, LANES), jnp.int32)
    for p in passes:
        p["route"], cnt = _plan_call(p["rid"], cnt)
    counts = cnt[0, :N_EXPERTS]
    padded = (counts + MOE_ROWS - 1) // MOE_ROWS * MOE_ROWS
    ends = jnp.cumsum(padded)
    off = (ends - padded).astype(jnp.int32)
    nt_all = sum(p["rid"].shape[0] for p in passes)
    n_blocks = (2 * nt_all) // MOE_ROWS + N_EXPERTS
    n_rows = n_blocks * MOE_ROWS
    n_used = (ends[-1] // MOE_ROWS).astype(jnp.int32)
    blk_start = jnp.minimum(jnp.arange(n_blocks, dtype=jnp.int32), n_used - 1) * MOE_ROWS
    blk_e = jnp.minimum(jnp.sum((blk_start[:, None] >= ends[None, :]).astype(jnp.int32), axis=1), N_EXPERTS - 1)

    pads = jnp.zeros((8, LANES), jnp.int32).at[0, :N_EXPERTS].set(ends.astype(jnp.int32))
    pads = pads.at[1, :N_EXPERTS].set((padded - counts).astype(jnp.int32)).at[2, 0].set(n_used)
    for p in passes:
        p["slots"] = _slots_call(off, p["route"])
    xs = _dispatch_call(jnp.concatenate([p["slots"] for p in passes], axis=1), passes[0]["h2"], passes[1]["h2"],
                        n_rows, pads)
    ys = _experts_call(blk_e, n_used.reshape(1), xs, expert_gate[0], expert_up[0], expert_down[0])
    outs = [_combine_call(p["slots"], p["x1"], p["rgate"], mod3, p["mod_row"], fg, ys) for p in passes]

    y_prompt = outs[0].reshape(bp, tp, D_MODEL)
    y_sample = outs[1].reshape(bs, ts, D_MODEL)
    new_state_rwkv = passes[0]["st_r"][:, None]
    new_state_gdn = passes[0]["st_g"][:, None]
    return (y_prompt, y_sample, new_state_rwkv, new_state_gdn)
```

```python
import functools
import math

import jax
import jax.numpy as jnp
from jax import lax
from jax.experimental import pallas as pl
from jax.experimental.pallas import tpu as pltpu

F32 = jnp.float32
BF16 = jnp.bfloat16

D_MODEL = 1024
R_HEADS, R_HEAD = 8, 64
R_WIDTH = R_HEADS * R_HEAD
G_HEADS, G_HEAD = 4, 128
G_WIDTH = G_HEADS * G_HEAD
LNX_EPS = 64e-5
NORM_EPS = 1e-6
N_GROUPS, EXPERTS_PER_GROUP = 4, 8
N_EXPERTS = N_GROUPS * EXPERTS_PER_GROUP
D_EXPERT = 512
GRID_W = 64

CHUNK = 64
TOK_TILE = 256
OUT_TILE = 512
MOE_ROWS = 512
SEQS_PER_STEP = 2
SEQS_PER_STEP_SHORT = 4
LANES = 128
RW = 3 * R_WIDTH + 256
GW_PAD = 4 * G_WIDTH + LANES
VMEM_LIMIT = 56 * 1024 * 1024


def _cparams(sem):
    return pltpu.CompilerParams(dimension_semantics=sem, vmem_limit_bytes=VMEM_LIMIT)


def _sigmoid(x):
    return 1.0 / (1.0 + jnp.exp(-x))


def _silu(x):
    return x * _sigmoid(x)


def _softplus(x):
    return jnp.maximum(x, 0.0) + jnp.log(1.0 + jnp.exp(-jnp.abs(x)))


def _dg(a, b, ca, cb):
    return lax.dot_general(a, b, (((ca,), (cb,)), ((), ())), preferred_element_type=F32)


def _split2(x):
    hi = x.astype(BF16)
    lo = (x - hi.astype(F32)).astype(BF16)
    return hi, lo


def _split3(x):
    h1 = x.astype(BF16)
    r1 = x - h1.astype(F32)
    h2 = r1.astype(BF16)
    h3 = (r1 - h2.astype(F32)).astype(BF16)
    return h1, h2, h3


def _mm(a, b, ca=1, cb=0, hp=False):
    if not hp:
        return _dg(a.astype(BF16), b.astype(BF16), ca, cb)
    ah, al = _split2(a)
    bh, bl = _split2(b)
    return _dg(ah, bh, ca, cb) + (_dg(ah, bl, ca, cb) + _dg(al, bh, ca, cb))


def _mm_exact_lhs(m_bf, x, ca=1, cb=0, parts=3):
    if parts == 2:
        hi, lo = _split2(x)
        return _dg(m_bf, hi, ca, cb) + _dg(m_bf, lo, ca, cb)
    h1, h2, h3 = _split3(x)
    return _dg(m_bf, h1, ca, cb) + (_dg(m_bf, h2, ca, cb) + _dg(m_bf, h3, ca, cb))


def _mm_exact_rhs(x, m_bf):
    h1, h2, h3 = _split3(x)
    return _dg(h1, m_bf, 1, 0) + (_dg(h2, m_bf, 1, 0) + _dg(h3, m_bf, 1, 0))


def _group_sum(x, ones_bf):
    w = ones_bf.shape[0]
    hi, lo = _split2(x)
    return jnp.concatenate([_dg(hi[:, j:j + w], ones_bf, 1, 0) + _dg(lo[:, j:j + w], ones_bf, 1, 0)
                            for j in range(0, x.shape[1], w)], axis=1)


def _iota(shape, axis):
    return lax.broadcasted_iota(jnp.int32, shape, axis)


def _tri_masks(n, rev):
    t = _iota((n, n), 0)
    s = _iota((n, n), 1)
    if rev:
        return t < s, t <= s
    return t > s, t >= s


PACK = 4


def _level_masks(n):
    t = _iota((n, PACK * n), 0)
    u = _iota((n, PACK * n), 1) % n
    levels = []
    s = 1
    while s < n:
        levels.append((t // (2 * s) == u // (2 * s)) & (t // s != u // s))
        s *= 2
    diag = _iota((PACK * n, PACK * n), 0) // n == _iota((PACK * n, PACK * n), 1) // n
    return levels, diag


def _block_diag(x, diag):
    return jnp.where(diag, jnp.concatenate([x] * PACK, axis=0), 0.0).astype(BF16)


def _unit_tri_inverse(ms, masks):
    levels, diag = masks
    n = ms[0].shape[0]
    eye = (_iota((n, PACK * n), 0) == _iota((n, PACK * n), 1) % n).astype(F32)
    xs = [eye - jnp.where(levels[0], m, 0.0) for m in ms]
    for mask in levels[1:]:
        zs = [_dg(jnp.where(mask, m, 0.0).astype(BF16), _block_diag(x, diag), 1, 0) for m, x in zip(ms, xs)]
        xs = [x - _dg(x.astype(BF16), _block_diag(z, diag), 1, 0) for x, z in zip(xs, zs)]
    return xs


def _mod_kernel(c_ref, w_ref, b_ref, o_ref):
    s = _silu(c_ref[...])
    o_ref[...] = _mm(s, w_ref[...], hp=True) + b_ref[...]


def _mod_call(cond, ada_w, ada_b):
    rows = cond.shape[0]
    n = ada_w.shape[1]
    bn = 1024
    return pl.pallas_call(
        _mod_kernel,
        out_shape=jax.ShapeDtypeStruct((rows, n), F32),
        grid=(n // bn,),
        in_specs=[pl.BlockSpec((rows, D_MODEL), lambda j: (0, 0)),
                  pl.BlockSpec((D_MODEL, bn), lambda j: (0, j)),
                  pl.BlockSpec((1, bn), lambda j: (0, j))],
        out_specs=pl.BlockSpec((rows, bn), lambda j: (0, j)),
        compiler_params=_cparams(("arbitrary",)),
        name="mod",
    )(cond, ada_w, ada_b)


def _inproj_kernel(x_ref, mod_ref, g_ref, w_ref, or_ref, og_ref):
    x = x_ref[...]
    ms = jnp.mean(x * x, axis=-1, keepdims=True)
    y = x * lax.rsqrt(ms + NORM_EPS) * g_ref[...]
    m = mod_ref[0]
    h = y * (1.0 + m[:, D_MODEL:2 * D_MODEL]) + m[:, :D_MODEL]
    p = jnp.dot(h.astype(BF16), w_ref[...], preferred_element_type=F32)
    or_ref[...] = p[:, :RW].astype(BF16)
    og_ref[...] = p[:, RW:].astype(BF16)


def _inproj_call(x2, mod3, mod_row, norm_g, w_in_bf):
    nt = x2.shape[0]
    ncol = w_in_bf.shape[1]
    return pl.pallas_call(
        _inproj_kernel,
        out_shape=(jax.ShapeDtypeStruct((nt, RW), BF16),
                   jax.ShapeDtypeStruct((nt, GW_PAD), BF16)),
        grid=(nt // TOK_TILE,),
        in_specs=[pl.BlockSpec((TOK_TILE, D_MODEL), lambda i: (i, 0)),
                  pl.BlockSpec((1, 1, 6 * D_MODEL), lambda i: (mod_row(i * TOK_TILE), 0, 0)),
                  pl.BlockSpec((1, D_MODEL), lambda i: (0, 0)),
                  pl.BlockSpec((D_MODEL, ncol), lambda i: (0, 0))],
        out_specs=(pl.BlockSpec((TOK_TILE, RW), lambda i: (i, 0)),
                   pl.BlockSpec((TOK_TILE, GW_PAD), lambda i: (i, 0))),
        compiler_params=_cparams(("arbitrary",)),
        name="inproj",
    )(x2, mod3, norm_g, w_in_bf)


def _head_blocks(width, head):
    return (_iota((width, width), 0) // head == _iota((width, width), 1) // head).astype(BF16)


def _rwkv_kernel(p_ref, *refs, seq, nrow, has_s0):
    s0_ref = refs[0] if has_s0 else None
    (w0_ref, wd_ref, a0_ref, wa_ref, g2_ref, kk_ref, ka_ref, rk_ref, lng_ref, lnb_ref,
     o_ref, st_ref, y_scr) = refs[1:] if has_s0 else refs
    C = CHUNK
    nc = seq // C
    blk = _head_blocks(2 * LANES, R_HEAD)
    k_k = kk_ref[...]
    k_a = ka_ref[...]
    r_k = rk_ref[...]
    exp_m05 = math.exp(-0.5)
    inv_masks = _level_masks(C)

    def lora_lhs(la):
        return jnp.where(_iota(la.shape, 1) < 64, jnp.tanh(la), la)

    def cols(n, rows, lo, hi):
        return p_ref[n, rows, lo:hi].astype(F32)

    assert R_HEAD == C
    PW = PACK * R_HEAD
    diag = inv_masks[1]
    pk_t = _iota((C, PW), 0)
    pk_s = _iota((C, PW), 1) % C
    dirs = []
    for d, rev in ((0, False), (1, True)):
        dirs.append(dict(d=d, rev=rev, tri=_tri_masks(C, rev)[1].astype(BF16),
                         strict=(pk_t < pk_s) if rev else (pk_t > pk_s),
                         incl=(pk_t <= pk_s) if rev else (pk_t >= pk_s)))
    hd = range(R_HEADS)
    sls = [slice(h * R_HEAD, (h + 1) * R_HEAD) for h in hd]

    def prep(n, dr, c):
        d, rev, tri = dr["d"], dr["rev"], dr["tri"]
        rows = pl.ds(pl.multiple_of(c * C, C), C)
        rr = cols(n, rows, 0, R_WIDTH)
        kx = cols(n, rows, R_WIDTH, 2 * R_WIDTH)
        vv = cols(n, rows, 2 * R_WIDTH, 3 * R_WIDTH)
        lhs = lora_lhs(cols(n, rows, 3 * R_WIDTH, 3 * R_WIDTH + LANES))
        logw = -exp_m05 * _sigmoid(w0_ref[d] + _mm(lhs, wd_ref[d]))
        a = _sigmoid(a0_ref[d] + _mm(lhs, wa_ref[d]))
        kkr = kx * k_k
        kk = kkr * lax.rsqrt(_group_sum(kkr * kkr, blk) + 1e-6)
        kd = kx * (1.0 + (a - 1.0) * k_a)
        bvec = kk * a
        gi = _mm_exact_lhs(tri, logw, parts=2)
        gtot = gi[0:1] if rev else gi[C - 1:C]
        en = jnp.exp(-gi)
        ee = jnp.exp(gtot - gi)
        return dict(rows=rows, vv=vv, rt=rr * jnp.exp(gi), at=-kk * jnp.exp(gi - logw), bt=bvec * en, kt=kd * en,
                    bh=bvec * ee, kh=kd * ee, gc=jnp.exp(gtot))

    seqs = range(nrow)
    packs = [(n, d, g) for n in seqs for d in (0, 1) for g in range(R_HEADS // PACK)]

    def step(i, carry):
        ops = {(n, d): prep(n, dirs[d], (nc - 1 - i) if d else i) for n in seqs for d in (0, 1)}
        pcol = lambda name, n, d, g: ops[n, d][name][:, g * PW:(g + 1) * PW]
        bd = lambda x: _block_diag(x, diag)
        ars = {k: jnp.concatenate([pcol("at", *k), pcol("rt", *k)], axis=0).astype(BF16) for k in packs}
        pb = {k: _dg(ars[k], bd(pcol("bt", *k)), 1, 1) for k in packs}
        pk = {k: _dg(ars[k], bd(pcol("kt", *k)), 1, 1) for k in packs}
        sts = {(n, d, g): carry[2 * n + d][:, g * PW:(g + 1) * PW] for n, d, g in packs}
        x0s = {k: _dg(ars[k], bd(sts[k]), 1, 1) for k in packs}
        bdv = {k: bd(pcol("vv", *k)) for k in packs}
        akv = {(n, d, g): _dg(jnp.where(dirs[d]["strict"], pk[n, d, g][:C], 0.0).astype(BF16), bdv[n, d, g], 1, 0)
               for n, d, g in packs}
        tms = dict(zip(packs, _unit_tri_inverse(
            [jnp.where(dirs[d]["strict"], -pb[n, d, g][:C], 0.0) for n, d, g in packs], inv_masks)))
        us = {k: _dg(tms[k].astype(BF16), bd(x0s[k][:C] + akv[k]), 1, 0) for k in packs}
        ys = {(n, d, g): x0s[n, d, g][C:] + _dg(
            jnp.concatenate([jnp.where(dirs[d]["incl"], pb[n, d, g][C:], 0.0),
                             jnp.where(dirs[d]["incl"], pk[n, d, g][C:], 0.0)], axis=1).astype(BF16),
            jnp.concatenate([bd(us[n, d, g]), bdv[n, d, g]], axis=0), 1, 0) for n, d, g in packs}
        snew = {}
        for n, d, g in packs:
            full = _dg(jnp.concatenate([us[n, d, g], pcol("vv", n, d, g)], axis=0).astype(BF16),
                       jnp.concatenate([pcol("bh", n, d, g), pcol("kh", n, d, g)], axis=0).astype(BF16), 0, 0)
            own = jnp.where(diag, full, 0.0)
            upd = own[0:R_HEAD]
            for j in range(1, PACK):
                upd = upd + own[j * R_HEAD:(j + 1) * R_HEAD]
            snew[n, d, g] = sts[n, d, g] * pcol("gc", n, d, g) + upd
        for n, d, g in packs:
            y_scr[n, d, ops[n, d]["rows"], g * PW:(g + 1) * PW] = ys[n, d, g]
        return tuple(jnp.concatenate([snew[n, d, g] for g in range(R_HEADS // PACK)], axis=1)
                     for n in seqs for d in (0, 1))

    def initial(n, d):
        if not has_s0:
            return jnp.zeros((R_HEAD, R_WIDTH), F32)
        return jnp.concatenate([s0_ref[n, d, h] for h in hd], axis=1)

    s_fin = lax.fori_loop(0, nc, step, tuple(initial(n, d) for n in seqs for d in (0, 1)))
    for n in seqs:
        for d in (0, 1):
            for h in hd:
                st_ref[n, d, h] = s_fin[2 * n + d][:, sls[h]]

    g2 = g2_ref[...]
    lng = lng_ref[...]
    lnb = lnb_ref[...]
    inv_n = 1.0 / R_HEAD
    fb = min(seq, 256)

    def finish(j, carry):
        rows = pl.ds(pl.multiple_of(j * fb, fb), fb)
        for n in seqs:
            y = y_scr[n, 0, rows, :] + y_scr[n, 1, rows, :]
            mu = _group_sum(y, blk) * inv_n
            yc = y - mu
            var = _group_sum(yc * yc, blk) * inv_n
            yn = yc * lax.rsqrt(var + LNX_EPS) * lng + lnb
            lhs = lora_lhs(cols(n, rows, 3 * R_WIDTH, 3 * R_WIDTH + LANES))
            a_sum = _sigmoid(a0_ref[0] + _mm(lhs, wa_ref[0])) + _sigmoid(a0_ref[1] + _mm(lhs, wa_ref[1]))
            rr = cols(n, rows, 0, R_WIDTH)
            kx = cols(n, rows, R_WIDTH, 2 * R_WIDTH)
            kd_sum = kx * (2.0 + (a_sum - 2.0) * k_a)
            bonus = _group_sum(rr * kd_sum * r_k, blk) * cols(n, rows, 2 * R_WIDTH, 3 * R_WIDTH)
            gl = cols(n, rows, 3 * R_WIDTH + LANES, 3 * R_WIDTH + 2 * LANES)
            gate = _mm(_sigmoid(gl), g2)
            o_ref[n, rows, :] = (yn + bonus) * gate
        return carry

    lax.fori_loop(0, seq // fb, finish, 0)


def _rwkv_call(proj_r, s0, w, *, batch, seq, nrow):
    p3 = proj_r.reshape(batch, seq, RW)
    kern = functools.partial(_rwkv_kernel, seq=seq, nrow=nrow, has_s0=s0 is not None)
    full = lambda a: pl.BlockSpec(a.shape, lambda b: (0,) * a.ndim)
    ws = [w["w0"], w["wd"], w["a0"], w["wa"], w["g2"], w["k_k"], w["k_a"], w["r_k"], w["lnx_g"], w["lnx_b"]]
    state_spec = pl.BlockSpec((nrow, 2, R_HEADS, R_HEAD, R_HEAD), lambda b: (b, 0, 0, 0, 0))
    states = [] if s0 is None else [s0]
    o, st = pl.pallas_call(
        kern,
        out_shape=(jax.ShapeDtypeStruct((batch, seq, R_WIDTH), F32),
                   jax.ShapeDtypeStruct((batch, 2, R_HEADS, R_HEAD, R_HEAD), F32)),
        grid=(batch // nrow,),
        in_specs=[pl.BlockSpec((nrow, seq, RW), lambda b: (b, 0, 0))] + [state_spec] * len(states)
        + [full(a) for a in ws],
        out_specs=(pl.BlockSpec((nrow, seq, R_WIDTH), lambda b: (b, 0, 0)), state_spec),
        scratch_shapes=[pltpu.VMEM((nrow, 2, seq, R_WIDTH), F32)],
        compiler_params=_cparams(("arbitrary",)),
        name="rwkv",
    )(p3, *states, *ws)
    return o.reshape(batch * seq, R_WIDTH), st


def _rwkv_weights(lp):
    zeros = jnp.zeros((2, 64, R_WIDTH), F32)
    row = lambda a: jnp.asarray(a, F32).reshape(1, R_WIDTH)
    return {
        "w0": jnp.asarray(lp["rwkv_w0"], F32).reshape(2, 1, R_WIDTH),
        "wd": jnp.concatenate([jnp.asarray(lp["rwkv_w2"], F32), zeros], axis=1),
        "a0": jnp.asarray(lp["rwkv_a0"], F32).reshape(2, 1, R_WIDTH),
        "wa": jnp.concatenate([zeros, jnp.asarray(lp["rwkv_a2"], F32)], axis=1),
        "g2": jnp.asarray(lp["rwkv_g2"], F32),
        "k_k": row(lp["rwkv_k_k"]), "k_a": row(lp["rwkv_k_a"]), "r_k": row(lp["rwkv_r_k"]),
        "lnx_g": row(lp["rwkv_lnx_g"]), "lnx_b": row(lp["rwkv_lnx_b"]),
    }


def _gdn_kernel(p_ref, *refs, seq, nrow, row_len, has_s0):
    s0_ref = refs[0] if has_s0 else None
    (cw_ref, alog_ref, dtb_ref, ng_ref, o_ref, st_ref, q_scr, k_scr, v_scr, o_scr) = refs[1:] if has_s0 else refs
    C = CHUNK
    nc = seq // C
    fb = min(seq, 256)
    W = G_WIDTH
    q_scale = G_HEAD ** -0.5
    seqs = range(nrow)
    hd = range(G_HEADS)
    hss = [slice(h * G_HEAD, (h + 1) * G_HEAD) for h in hd]

    cw0 = cw_ref[0:1, :]
    cw1 = cw_ref[1:2, :]
    cw2 = cw_ref[2:3, :]
    tpos = _iota((fb, 1), 0) % row_len
    is_first = tpos == 0
    is_last = tpos == row_len - 1

    def prep(j, carry):
        rows = pl.ds(pl.multiple_of(j * fb, fb), fb)
        for n in seqs:
            x = p_ref[n, rows, 0:3 * W].astype(F32)
            xm = jnp.where(is_first, 0.0, pltpu.roll(x, 1, 0))
            xp = jnp.where(is_last, 0.0, pltpu.roll(x, fb - 1, 0))
            y = _silu(cw0 * xm + cw1 * x + cw2 * xp)
            for h in hd:
                qh = y[:, hss[h]]
                kh = y[:, W + h * G_HEAD:W + (h + 1) * G_HEAD]
                q_scr[n, rows, hss[h]] = qh * lax.rsqrt(jnp.sum(qh * qh, axis=-1, keepdims=True) + 1e-6) * q_scale
                k_scr[n, rows, hss[h]] = kh * lax.rsqrt(jnp.sum(kh * kh, axis=-1, keepdims=True) + 1e-6)
            v_scr[n, rows, :] = y[:, 2 * W:3 * W]
        return carry

    lax.fori_loop(0, seq // fb, prep, 0)

    assert G_HEADS == PACK
    inv_masks = _level_masks(C)
    alog = alog_ref[...]
    dtb = dtb_ref[...]
    sel_row = _iota((LANES, W), 0)
    sel_head = _iota((LANES, W), 1) // G_HEAD
    pk_row = _iota((LANES, PACK * C), 0)
    pk_head = _iota((LANES, PACK * C), 1) // C
    pk_t = _iota((C, PACK * C), 0)
    pk_s = _iota((C, PACK * C), 1) % C

    dirs = []
    for d, rev in ((0, False), (1, True)):
        dirs.append(dict(
            d=d, rev=rev, tri=_tri_masks(C, rev)[1].astype(BF16),
            strict=(pk_t < pk_s) if rev else (pk_t > pk_s),
            incl=(pk_t <= pk_s) if rev else (pk_t >= pk_s),
            upto=(pk_t >= pk_s) if rev else (pk_t <= pk_s),
            e_beta=(sel_row == 4 * d + sel_head).astype(BF16),
            e_g=(sel_row == 8 + 4 * d + sel_head).astype(BF16),
            e_g_pk=(pk_row == 8 + 4 * d + pk_head).astype(BF16)))
    items = [(n, d, h) for n in seqs for d in (0, 1) for h in hd]
    packs = [(n, d) for n in seqs for d in (0, 1)]

    def prep_dir(n, dr, c):
        rows = pl.ds(pl.multiple_of(c * C, C), C)
        ba = p_ref[n, rows, 4 * W:4 * W + LANES].astype(F32)
        gval = -jnp.exp(alog) * _softplus(ba + dtb)
        beta_b = _mm_exact_rhs(_sigmoid(ba), dr["e_beta"])
        gcum = _mm_exact_lhs(dr["tri"], _mm_exact_rhs(gval, dr["e_g"]))
        g_pk = _mm_exact_rhs(gval, dr["e_g_pk"])
        g_col = _mm_exact_lhs(dr["tri"], g_pk)
        g_row = jnp.sum(jnp.where(dr["upto"], g_pk, 0.0), axis=0, keepdims=True)
        return dict(rows=rows, beta=beta_b, gcum=gcum,
                    decay=jnp.where(dr["incl"], jnp.exp(jnp.where(dr["incl"], g_col - g_row, 0.0)), 0.0),
                    q=q_scr[n, rows, :], k=k_scr[n, rows, :], v=v_scr[n, rows, :])

    def step(i, carry):
        ops = {(n, d): prep_dir(n, dirs[d], (nc - 1 - i) if d else i) for n, d in packs}
        hcol = lambda name, n, d, h: ops[n, d][name][:, hss[h]]
        qhs = {k: hcol("q", *k) for k in items}
        khs = {k: hcol("k", *k) for k in items}
        gcs = {k: hcol("gcum", *k) for k in items}
        bbs = {k: hcol("beta", *k) for k in items}
        kbs = {k: khs[k] * bbs[k] for k in items}
        kks = {k: _mm(jnp.concatenate([kbs[k], qhs[k]], axis=0), khs[k], 1, 1) for k in items}
        kka = {(n, d): jnp.concatenate([kks[n, d, h][:C] for h in hd], axis=1) for n, d in packs}
        tcat = dict(zip(packs, _unit_tri_inverse(
            [jnp.where(dirs[d]["strict"], kka[n, d] * ops[n, d]["decay"], 0.0) for n, d in packs], inv_masks)))
        egs = {k: jnp.exp(gcs[k]) for k in items}
        uws = {(n, d, h): _mm(tcat[n, d][:, h * C:(h + 1) * C],
                              jnp.concatenate([hcol("v", n, d, h) * bbs[n, d, h], kbs[n, d, h] * egs[n, d, h]], axis=1))
               for n, d, h in items}
        shs = {(n, d, h): carry[2 * n + d][:, hss[h]] for n, d, h in items}
        wss = {k: _mm(jnp.concatenate([uws[k][:, G_HEAD:], qhs[k] * egs[k]], axis=0), shs[k]) for k in items}
        vns = {k: uws[k][:, :G_HEAD] - wss[k][:C] for k in items}
        outs = {(n, d, h): wss[n, d, h][C:]
                + _mm(kks[n, d, h][C:] * ops[n, d]["decay"][:, h * C:(h + 1) * C], vns[n, d, h]) for n, d, h in items}
        snew = {}
        for n, d, h in items:
            glast = gcs[n, d, h][0:1] if dirs[d]["rev"] else gcs[n, d, h][C - 1:C]
            k_dec = khs[n, d, h] * jnp.exp(glast - gcs[n, d, h])
            snew[n, d, h] = shs[n, d, h] * jnp.exp(glast) + _mm(k_dec, vns[n, d, h], 0, 0)
        for n, d in packs:
            o_scr[n, d, ops[n, d]["rows"], :] = jnp.concatenate([outs[n, d, h] for h in hd], axis=1)
        return tuple(jnp.concatenate([snew[n, d, h] for h in hd], axis=1) for n, d in packs)

    def initial(n, d):
        if not has_s0:
            return jnp.zeros((G_HEAD, W), F32)
        return jnp.concatenate([s0_ref[n, d, h] for h in hd], axis=1)

    s_fin = lax.fori_loop(0, nc, step, tuple(initial(n, d) for n, d in packs))
    for j, (n, d) in enumerate(packs):
        for h in hd:
            st_ref[n, d, h] = s_fin[j][:, hss[h]]

    ng = ng_ref[...]

    def finish(j, carry):
        rows = pl.ds(pl.multiple_of(j * fb, fb), fb)
        for n in seqs:
            o = o_scr[n, 0, rows, :] + o_scr[n, 1, rows, :]
            z = p_ref[n, rows, 3 * W:4 * W].astype(F32)
            for h in hd:
                oh = o[:, hss[h]]
                ms = jnp.mean(oh * oh, axis=-1, keepdims=True)
                o_ref[n, rows, hss[h]] = oh * lax.rsqrt(ms + NORM_EPS) * ng * _silu(z[:, hss[h]])
        return carry

    lax.fori_loop(0, seq // fb, finish, 0)


def _gdn_call(proj_g, s0, w, *, batch, seq, nrow, row_len):
    p3 = proj_g.reshape(batch, seq, GW_PAD)
    kern = functools.partial(_gdn_kernel, seq=seq, nrow=nrow, row_len=row_len, has_s0=s0 is not None)
    full = lambda a: pl.BlockSpec(a.shape, lambda b: (0,) * a.ndim)
    ws = [w["conv"], w["alog"], w["dtb"], w["norm_g"]]
    state_spec = pl.BlockSpec((nrow, 2, G_HEADS, G_HEAD, G_HEAD), lambda b: (b, 0, 0, 0, 0))
    states = [] if s0 is None else [s0]
    o, st = pl.pallas_call(
        kern,
        out_shape=(jax.ShapeDtypeStruct((batch, seq, G_WIDTH), F32),
                   jax.ShapeDtypeStruct((batch, 2, G_HEADS, G_HEAD, G_HEAD), F32)),
        grid=(batch // nrow,),
        in_specs=[pl.BlockSpec((nrow, seq, GW_PAD), lambda b: (b, 0, 0))] + [state_spec] * len(states)
        + [full(a) for a in ws],
        out_specs=(pl.BlockSpec((nrow, seq, G_WIDTH), lambda b: (b, 0, 0)), state_spec),
        scratch_shapes=[pltpu.VMEM((nrow, seq, G_WIDTH), F32)] * 3 + [pltpu.VMEM((nrow, 2, seq, G_WIDTH), F32)],
        compiler_params=_cparams(("arbitrary",)),
        name="gdn",
    )(p3, *states, *ws)
    return o.reshape(batch * seq, G_WIDTH), st


def _gdn_weights(lp):
    lanes = jnp.zeros((1, LANES), F32)
    return {
        "conv": jnp.asarray(lp["gdn_conv"], F32),
        "alog": lanes.at[0, 8:16].set(jnp.asarray(lp["gdn_a_log"], F32).reshape(8)),
        "dtb": lanes.at[0, 8:16].set(jnp.asarray(lp["gdn_dt_bias"], F32).reshape(8)),
        "norm_g": jnp.asarray(lp["gdn_norm_g"], F32).reshape(1, G_HEAD),
    }


def _outproj_kernel(x_ref, or_ref, og_ref, mod_ref, wo_ref, n2_ref, rw_ref, rb_ref,
                    x1_ref, h2_ref, rid_ref, rgate_ref):
    m = mod_ref[0]
    g1 = m[:, 2 * D_MODEL:3 * D_MODEL]
    sh2 = m[:, 3 * D_MODEL:4 * D_MODEL]
    sc2 = m[:, 4 * D_MODEL:5 * D_MODEL]
    mix = (jnp.dot(or_ref[...].astype(BF16), wo_ref[0:R_WIDTH, :], preferred_element_type=F32)
           + jnp.dot(og_ref[...].astype(BF16), wo_ref[R_WIDTH:, :], preferred_element_type=F32))
    x1 = x_ref[...] + g1 * mix
    x1_ref[...] = x1
    ms = jnp.mean(x1 * x1, axis=-1, keepdims=True)
    h2 = x1 * lax.rsqrt(ms + NORM_EPS) * n2_ref[...] * (1.0 + sc2) + sh2
    h2_ref[...] = h2

    logits = _mm(h2, rw_ref[...], hp=True) + rb_ref[...]
    lane = _iota(logits.shape, 1)
    neg = jnp.float32(-1e30)
    big = jnp.int32(1 << 20)

    def first_argmax(v):
        mx = jnp.max(v, axis=-1, keepdims=True)
        idx = jnp.min(jnp.where(v == mx, lane, big), axis=-1, keepdims=True)
        return mx, idx

    lg = jnp.where(lane < N_GROUPS, logits, neg)
    mg, grp = first_argmax(lg)
    p_grp = 1.0 / jnp.sum(jnp.where(lane < N_GROUPS, jnp.exp(lg - mg), 0.0), axis=-1, keepdims=True)
    in_grp = (lane >= N_GROUPS) & (lane < N_GROUPS + N_EXPERTS) & ((lane - N_GROUPS) // EXPERTS_PER_GROUP == grp)
    le = jnp.where(in_grp, logits, neg)
    m1, i1 = first_argmax(le)
    m2, i2 = first_argmax(jnp.where(lane == i1, neg, le))
    e2 = jnp.exp(m2 - m1)
    w1 = p_grp / (1.0 + e2)
    w2 = p_grp * e2 / (1.0 + e2)
    rid_ref[...] = jnp.where(lane == 0, i1 - N_GROUPS, jnp.where(lane == 1, i2 - N_GROUPS, 0))
    rgate_ref[...] = jnp.where(lane == 0, w1, jnp.where(lane == 1, w2, 0.0))


def _outproj_call(x2, o_r, o_g, mod3, mod_row, wo_bf, n2g, rw, rb):
    nt = x2.shape[0]
    tile = lambda w: pl.BlockSpec((OUT_TILE, w), lambda i: (i, 0))
    const = lambda a: pl.BlockSpec(a.shape, lambda i: (0,) * a.ndim)
    return pl.pallas_call(
        _outproj_kernel,
        out_shape=(jax.ShapeDtypeStruct((nt, D_MODEL), F32), jax.ShapeDtypeStruct((nt, D_MODEL), F32),
                   jax.ShapeDtypeStruct((nt, LANES), jnp.int32), jax.ShapeDtypeStruct((nt, LANES), F32)),
        grid=(nt // OUT_TILE,),
        in_specs=[tile(D_MODEL), tile(R_WIDTH), tile(G_WIDTH),
                  pl.BlockSpec((1, 1, 6 * D_MODEL), lambda i: (mod_row(i * OUT_TILE), 0, 0)),
                  const(wo_bf), const(n2g), const(rw), const(rb)],
        out_specs=(tile(D_MODEL), tile(D_MODEL), tile(LANES), tile(LANES)),
        compiler_params=_cparams(("arbitrary",)),
        name="outproj",
    )(x2, o_r, o_g, mod3, wo_bf, n2g, rw, rb)


PLAN_TILE = 1024


def _plan_kernel(rid_ref, cnt_in_ref, route_ref, cnt_ref, carry):
    i = pl.program_id(0)

    @pl.when(i == 0)
    def _():
        carry[...] = cnt_in_ref[...].astype(F32)

    rid = rid_ref[...]
    lane = _iota(rid.shape, 1)
    e0 = jnp.sum(jnp.where(lane == 0, rid, 0), axis=-1, keepdims=True)
    e1 = jnp.sum(jnp.where(lane == 1, rid, 0), axis=-1, keepdims=True)
    oh0 = (lane == e0).astype(F32)
    oh1 = (lane == e1).astype(F32)
    oh = oh0 + oh1
    n = rid.shape[0]
    earlier = (_iota((n, n), 0) > _iota((n, n), 1)).astype(BF16)
    before = jnp.dot(earlier, oh.astype(BF16), preferred_element_type=F32) + carry[0:1, :]
    r0 = jnp.sum(oh0 * before, axis=-1, keepdims=True)
    r1 = jnp.sum(oh1 * before, axis=-1, keepdims=True)
    cols = jnp.where(lane == 0, e0.astype(F32), jnp.where(lane == 1, e1.astype(F32),
                                                          jnp.where(lane == 2, r0, jnp.where(lane == 3, r1, 0.0))))
    route_ref[...] = jnp.transpose(cols)[0:8, :].astype(jnp.int32)
    total = carry[0:1, :] + jnp.sum(oh, axis=0, keepdims=True)
    carry[...] = jnp.broadcast_to(total, carry.shape)
    cnt_ref[...] = jnp.broadcast_to(total, cnt_ref.shape).astype(jnp.int32)


def _plan_call(rid, cnt_in):
    nt = rid.shape[0]
    return pl.pallas_call(
        _plan_kernel,
        out_shape=(jax.ShapeDtypeStruct((8, nt), jnp.int32), jax.ShapeDtypeStruct((8, LANES), jnp.int32)),
        grid=(nt // PLAN_TILE,),
        in_specs=[pl.BlockSpec((PLAN_TILE, LANES), lambda i: (i, 0)), pl.BlockSpec((8, LANES), lambda i: (0, 0))],
        out_specs=(pl.BlockSpec((8, PLAN_TILE), lambda i: (0, i)), pl.BlockSpec((8, LANES), lambda i: (0, 0))),
        scratch_shapes=[pltpu.VMEM((8, LANES), F32)],
        compiler_params=_cparams(("arbitrary",)),
        name="plan",
    )(rid, cnt_in)


def _slots_kernel(off_ref, route_ref, slot_ref):
    route = route_ref[...]
    ids = jnp.concatenate([route[0:2], route[0:2], route[0:2], route[0:2]], axis=0)
    rank = jnp.concatenate([route[2:4], route[2:4], route[2:4], route[2:4]], axis=0)
    base = jnp.zeros_like(ids)
    for e in range(N_EXPERTS):
        base = jnp.where(ids == e, off_ref[e], base)
    slot_ref[...] = base + rank


def _slots_call(off, route):
    nt = route.shape[1]
    tile = min(nt, 2048)
    return pl.pallas_call(
        _slots_kernel,
        out_shape=jax.ShapeDtypeStruct((8, nt), jnp.int32),
        grid=(nt // tile,),
        in_specs=[pl.BlockSpec(memory_space=pltpu.SMEM), pl.BlockSpec((8, tile), lambda i: (0, i))],
        out_specs=pl.BlockSpec((8, tile), lambda i: (0, i)),
        compiler_params=_cparams(("arbitrary",)),
        name="slots",
    )(off, route)


def _dispatch_kernel(slot_ref, h2a_ref, h2b_ref, pad_ref, xs_ref, zero_scr, sem, *, tiles_a, n_rows):
    i = pl.program_id(0)

    @pl.when(i == 0)
    def zero_unrouted_slots():
        zero_scr[...] = jnp.zeros_like(zero_scr)
        nblk = n_rows // MOE_ROWS
        aligned = [s for s in (256, 128, 64, 32, 16, 8) if s < MOE_ROWS]

        def pad_copies(act):
            for e in range(N_EXPERTS):
                pos = pad_ref[0, e]
                pad = pad_ref[1, e]
                for s in aligned:
                    hit = (pad & s) != 0
                    first = pos - s

                    @pl.when(hit)
                    def _(first=first, s=s):
                        act(pltpu.make_async_copy(zero_scr.at[pl.ds(0, s)],
                                                  xs_ref.at[pl.ds(pl.multiple_of(first, 8), s)], sem))

                    pos = jnp.where(hit, first, pos)
                rem = pad & 7
                for j in range(7):
                    @pl.when(j < rem)
                    def _(pos=pos, j=j):
                        act(pltpu.make_async_copy(zero_scr.at[pl.ds(0, 1)], xs_ref.at[pl.ds(pos - 1 - j, 1)], sem))

        def block_copy(b):
            return pltpu.make_async_copy(zero_scr, xs_ref.at[pl.ds(pl.multiple_of(b * MOE_ROWS, MOE_ROWS),
                                                                   MOE_ROWS)], sem)

        def start(b, c):
            block_copy(b).start()
            return c

        def wait(b, c):
            block_copy(b).wait()
            return c

        pad_copies(lambda cp: cp.start())
        lax.fori_loop(pad_ref[2, 0], nblk, start, 0)
        pad_copies(lambda cp: cp.wait())
        lax.fori_loop(pad_ref[2, 0], nblk, wait, 0)

    def scatter_tile(h2_ref):
        def start(t, c):
            for k in range(2):
                pltpu.make_async_copy(h2_ref.at[pl.ds(t, 1)], xs_ref.at[pl.ds(slot_ref[k, t], 1)], sem).start()
            return c

        lax.fori_loop(0, TOK_TILE, start, 0, unroll=8)
        for _ in range(2):
            pltpu.make_async_copy(h2_ref, xs_ref.at[pl.ds(0, TOK_TILE)], sem).wait()

    @pl.when(i < tiles_a)
    def _():
        scatter_tile(h2a_ref)

    @pl.when(i >= tiles_a)
    def _():
        scatter_tile(h2b_ref)


def _route_spec(shift=0, last=None):
    if shift == 0:
        return pl.BlockSpec((8, TOK_TILE), lambda i: (0, i), memory_space=pltpu.SMEM)
    return pl.BlockSpec((8, TOK_TILE), lambda i: (0, jnp.minimum(i + shift, last)), memory_space=pltpu.SMEM)


def _dispatch_call(slots, h2a, h2b, n_rows, pads):
    tiles_a = h2a.shape[0] // TOK_TILE
    tiles_b = h2b.shape[0] // TOK_TILE
    kern = functools.partial(_dispatch_kernel, tiles_a=tiles_a, n_rows=n_rows)
    return pl.pallas_call(
        kern,
        out_shape=jax.ShapeDtypeStruct((n_rows, D_MODEL), F32),
        grid=(tiles_a + tiles_b,),
        in_specs=[_route_spec(),
                  pl.BlockSpec((TOK_TILE, D_MODEL), lambda i: (jnp.minimum(i, tiles_a - 1), 0)),
                  pl.BlockSpec((TOK_TILE, D_MODEL), lambda i: (jnp.maximum(i - tiles_a, 0), 0)),
                  pl.BlockSpec(memory_space=pltpu.SMEM)],
        out_specs=pl.BlockSpec(memory_space=pl.ANY),
        scratch_shapes=[pltpu.VMEM((MOE_ROWS, D_MODEL), F32), pltpu.SemaphoreType.DMA(())],
        compiler_params=_cparams(("arbitrary",)),
        name="dispatch",
    )(slots, h2a, h2b, pads)


def _experts_kernel(be_ref, nu_ref, xs_ref, wg_ref, wu_ref, wd_ref, ys_ref, wg_bf, wu_bf, wd_bf):
    b = pl.program_id(0)
    used = b < nu_ref[0]

    @pl.when(used)
    def _():
        prev = be_ref[jnp.maximum(b - 1, 0)]

        x = xs_ref[...]
        g = jnp.dot(x, wg_ref[0], preferred_element_type=F32)
        u = jnp.dot(x, wu_ref[0], preferred_element_type=F32)
        h = _silu(g) * u
        ys_ref[...] = jnp.dot(h, wd_ref[0], preferred_element_type=F32)

    @pl.when(jnp.logical_not(used))
    def _():
        ys_ref[...] = jnp.zeros_like(ys_ref)


def _experts_call(blk_e, n_used, xs, wg, wu, wd):
    n_rows = xs.shape[0]
    nb = n_rows // MOE_ROWS
    grid_spec = pltpu.PrefetchScalarGridSpec(
        num_scalar_prefetch=2,
        grid=(nb,),
        in_specs=[pl.BlockSpec((MOE_ROWS, D_MODEL), lambda b, be, nu: (jnp.minimum(b, nu[0] - 1), 0)),
                  pl.BlockSpec((1, D_MODEL, D_EXPERT), lambda b, be, nu: (be[b], 0, 0)),
                  pl.BlockSpec((1, D_MODEL, D_EXPERT), lambda b, be, nu: (be[b], 0, 0)),
                  pl.BlockSpec((1, D_EXPERT, D_MODEL), lambda b, be, nu: (be[b], 0, 0))],
        out_specs=pl.BlockSpec((MOE_ROWS, D_MODEL), lambda b, be, nu: (b, 0)),
        scratch_shapes=[pltpu.VMEM((D_MODEL, D_EXPERT), BF16), pltpu.VMEM((D_MODEL, D_EXPERT), BF16),
                        pltpu.VMEM((D_EXPERT, D_MODEL), BF16)],
    )
    return pl.pallas_call(
        _experts_kernel,
        out_shape=jax.ShapeDtypeStruct((n_rows, D_MODEL), F32),
        grid_spec=grid_spec,
        compiler_params=_cparams(("arbitrary",)),
        name="experts",
    )(blk_e, n_used, xs, wg, wu, wd)


def _combine_kernel(slot_ref, slot_next_ref, x1_ref, gate_ref, mod_ref, fg_ref, ys_ref, y_ref, buf, sems):
    i = pl.program_id(0)
    n = pl.num_programs(0)
    slot = i % 2

    def gather(slots, s):
        def start(t, c):
            for k in range(2):
                pltpu.make_async_copy(ys_ref.at[pl.ds(slots[k, t], 1)], buf.at[s, k, pl.ds(t, 1)], sems.at[s]).start()
            return c

        lax.fori_loop(0, TOK_TILE, start, 0, unroll=8)

    @pl.when(i == 0)
    def _():
        gather(slot_ref, 0)

    @pl.when(i + 1 < n)
    def _():
        gather(slot_next_ref, 1 - slot)

    for k in range(2):
        pltpu.make_async_copy(ys_ref.at[pl.ds(0, TOK_TILE)], buf.at[slot, k], sems.at[slot]).wait()

    gate = gate_ref[...]
    lane = _iota(gate.shape, 1)
    w0 = jnp.sum(jnp.where(lane == 0, gate, 0.0), axis=-1, keepdims=True)
    w1 = jnp.sum(jnp.where(lane == 1, gate, 0.0), axis=-1, keepdims=True)
    g2 = mod_ref[0][:, 5 * D_MODEL:6 * D_MODEL]
    y = x1_ref[...] + g2 * (w0 * buf[slot, 0] + w1 * buf[slot, 1])
    ms = jnp.mean(y * y, axis=-1, keepdims=True)
    y_ref[...] = y * lax.rsqrt(ms + NORM_EPS) * fg_ref[...]


def _combine_call(slots, x1, rgate, mod3, mod_row, fg, ys):
    nt = x1.shape[0]
    n_tiles = nt // TOK_TILE
    return pl.pallas_call(
        _combine_kernel,
        out_shape=jax.ShapeDtypeStruct((nt, D_MODEL), F32),
        grid=(n_tiles,),
        in_specs=[_route_spec(), _route_spec(1, n_tiles - 1),
                  pl.BlockSpec((TOK_TILE, D_MODEL), lambda i: (i, 0)),
                  pl.BlockSpec((TOK_TILE, LANES), lambda i: (i, 0)),
                  pl.BlockSpec((1, 1, 6 * D_MODEL), lambda i: (mod_row(i * TOK_TILE), 0, 0)),
                  pl.BlockSpec((1, D_MODEL), lambda i: (0, 0)),
                  pl.BlockSpec(memory_space=pl.ANY)],
        out_specs=pl.BlockSpec((TOK_TILE, D_MODEL), lambda i: (i, 0)),
        scratch_shapes=[pltpu.VMEM((2, 2, TOK_TILE, D_MODEL), F32), pltpu.SemaphoreType.DMA((2,))],
        compiler_params=_cparams(("arbitrary",)),
        name="combine",
    )(slots, slots, x1, rgate, mod3, fg, ys)


def kernel(x_prompt, x_sample, state_rwkv, state_gdn, c, c_ctx, ada_w, ada_b, norm1_g, norm2_g, w_in, w_out,
           rwkv_w0, rwkv_w2, rwkv_a0, rwkv_a2, rwkv_g2, rwkv_k_k, rwkv_k_a, rwkv_r_k, rwkv_lnx_g, rwkv_lnx_b,
           gdn_conv, gdn_a_log, gdn_dt_bias, gdn_norm_g, router_group_w, router_group_b, router_expert_w,
           router_expert_b, expert_gate, expert_up, expert_down, final_norm_g):
    assert ada_w.shape[0] == 1, "one layer"
    bp, tp, _ = x_prompt.shape
    bs, ts, _ = x_sample.shape
    lp = dict(rwkv_w0=rwkv_w0[0], rwkv_w2=rwkv_w2[0], rwkv_a0=rwkv_a0[0], rwkv_a2=rwkv_a2[0], rwkv_g2=rwkv_g2[0],
              rwkv_k_k=rwkv_k_k[0], rwkv_k_a=rwkv_k_a[0], rwkv_r_k=rwkv_r_k[0], rwkv_lnx_g=rwkv_lnx_g[0],
              rwkv_lnx_b=rwkv_lnx_b[0], gdn_conv=gdn_conv[0], gdn_a_log=gdn_a_log[0],
              gdn_dt_bias=gdn_dt_bias[0], gdn_norm_g=gdn_norm_g[0])
    rw_w = _rwkv_weights(lp)
    gd_w = _gdn_weights(lp)

    n_cond = 1 + bs
    cond = jnp.concatenate([c_ctx[None, :], c, jnp.zeros((16 - n_cond, D_MODEL), F32)], axis=0)
    mod = _mod_call(cond, ada_w[0], ada_b)
    mod3 = mod.reshape(16, 1, 6 * D_MODEL)

    in_cols = w_in.shape[2]
    w_in_bf = jnp.pad(w_in[0], ((0, 0), (0, RW + GW_PAD - in_cols))).astype(BF16)
    w_out_bf = w_out[0].astype(BF16)
    n1g = norm1_g.reshape(1, D_MODEL)
    n2g = norm2_g.reshape(1, D_MODEL)
    fg = final_norm_g.reshape(1, D_MODEL)
    rw = jnp.zeros((D_MODEL, LANES), F32).at[:, :N_GROUPS].set(router_group_w[0])
    rw = rw.at[:, N_GROUPS:N_GROUPS + N_EXPERTS].set(router_expert_w[0])
    rb = jnp.zeros((1, LANES), F32).at[0, :N_GROUPS].set(router_group_b[0])
    rb = rb.at[0, N_GROUPS:N_GROUPS + N_EXPERTS].set(router_expert_b[0])

    passes = [
        dict(x=x_prompt.reshape(bp * tp, D_MODEL), batch=bp, seq=tp, row_len=tp,
             mod_row=lambda tok: 0,
             s_r=None, s_g=None),
        dict(x=x_sample.reshape(bs * ts, D_MODEL), batch=bs, seq=ts, row_len=GRID_W,
             mod_row=lambda tok: 1 + tok // ts,
             s_r=state_rwkv[:, 0], s_g=state_gdn[:, 0]),
    ]

    for p in passes:
        proj_r, proj_g = _inproj_call(p["x"], mod3, p["mod_row"], n1g, w_in_bf)
        nrow = SEQS_PER_STEP_SHORT if p["seq"] <= 256 else SEQS_PER_STEP
        o_r, p["st_r"] = _rwkv_call(proj_r, p["s_r"], rw_w, batch=p["batch"], seq=p["seq"], nrow=nrow)
        o_g, p["st_g"] = _gdn_call(proj_g, p["s_g"], gd_w, batch=p["batch"], seq=p["seq"], nrow=nrow,
                                   row_len=p["row_len"])
        p["x1"], p["h2"], p["rid"], p["rgate"] = _outproj_call(
            p["x"], o_r, o_g, mod3, p["mod_row"], w_out_bf, n2g, rw, rb)

    cnt = jnp.zeros((8, LANES), jnp.int32)
    for p in passes:
        p["route"], cnt = _plan_call(p["rid"], cnt)
    counts = cnt[0, :N_EXPERTS]
    padded = (counts + MOE_ROWS - 1) // MOE_ROWS * MOE_ROWS
    ends = jnp.cumsum(padded)
    off = (ends - padded).astype(jnp.int32)
    nt_all = sum(p["rid"].shape[0] for p in passes)
    n_blocks = (2 * nt_all) // MOE_ROWS + N_EXPERTS
    n_rows = n_blocks * MOE_ROWS
    n_used = (ends[-1] // MOE_ROWS).astype(jnp.int32)
    blk_start = jnp.minimum(jnp.arange(n_blocks, dtype=jnp.int32), n_used - 1) * MOE_ROWS
    blk_e = jnp.minimum(jnp.sum((blk_start[:, None] >= ends[None, :]).astype(jnp.int32), axis=1), N_EXPERTS - 1)

    pads = jnp.zeros((8, LANES), jnp.int32).at[0, :N_EXPERTS].set(ends.astype(jnp.int32))
    pads = pads.at[1, :N_EXPERTS].set((padded - counts).astype(jnp.int32)).at[2, 0].set(n_used)
    for p in passes:
        p["slots"] = _slots_call(off, p["route"])
    xs = _dispatch_call(jnp.concatenate([p["slots"] for p in passes], axis=1), passes[0]["h2"], passes[1]["h2"],
                        n_rows, pads)
    ys = _experts_call(blk_e, n_used.reshape(1), xs, expert_gate[0], expert_up[0], expert_down[0])
    outs = [_combine_call(p["slots"], p["x1"], p["rgate"], mod3, p["mod_row"], fg, ys) for p in passes]

    y_prompt = outs[0].reshape(bp, tp, D_MODEL)
    y_sample = outs[1].reshape(bs, ts, D_MODEL)
    new_state_rwkv = passes[0]["st_r"][:, None]
    new_state_gdn = passes[0]["st_g"][:, None]
    return (y_prompt, y_sample, new_state_rwkv, new_state_gdn)
```

```python
import functools
import math

import jax
import jax.numpy as jnp
from jax import lax
from jax.experimental import pallas as pl
from jax.experimental.pallas import tpu as pltpu

F32 = jnp.float32
BF16 = jnp.bfloat16

D_MODEL = 1024
R_HEADS, R_HEAD = 8, 64
R_WIDTH = R_HEADS * R_HEAD
G_HEADS, G_HEAD = 4, 128
G_WIDTH = G_HEADS * G_HEAD
LNX_EPS = 64e-5
NORM_EPS = 1e-6
N_GROUPS, EXPERTS_PER_GROUP = 4, 8
N_EXPERTS = N_GROUPS * EXPERTS_PER_GROUP
D_EXPERT = 512
GRID_W = 64

CHUNK = 64
TOK_TILE = 512
OUT_TILE = 512
MOE_ROWS = 512
SEQS_PER_STEP = 2
SEQS_PER_STEP_SHORT = 4
LANES = 128
RW = 3 * R_WIDTH + 256
GW_PAD = 4 * G_WIDTH + LANES
VMEM_LIMIT = 56 * 1024 * 1024


def _cparams(sem):
    return pltpu.CompilerParams(dimension_semantics=sem, vmem_limit_bytes=VMEM_LIMIT)


def _sigmoid(x):
    return 1.0 / (1.0 + jnp.exp(-x))


def _silu(x):
    return x * _sigmoid(x)


def _softplus(x):
    return jnp.maximum(x, 0.0) + jnp.log(1.0 + jnp.exp(-jnp.abs(x)))


def _dg(a, b, ca, cb):
    return lax.dot_general(a, b, (((ca,), (cb,)), ((), ())), preferred_element_type=F32)


def _split2(x):
    hi = x.astype(BF16)
    lo = (x - hi.astype(F32)).astype(BF16)
    return hi, lo


def _split3(x):
    h1 = x.astype(BF16)
    r1 = x - h1.astype(F32)
    h2 = r1.astype(BF16)
    h3 = (r1 - h2.astype(F32)).astype(BF16)
    return h1, h2, h3


def _mm(a, b, ca=1, cb=0, hp=False):
    if not hp:
        return _dg(a.astype(BF16), b.astype(BF16), ca, cb)
    ah, al = _split2(a)
    bh, bl = _split2(b)
    return _dg(ah, bh, ca, cb) + (_dg(ah, bl, ca, cb) + _dg(al, bh, ca, cb))


def _mm_exact_lhs(m_bf, x, ca=1, cb=0, parts=3):
    if parts == 2:
        hi, lo = _split2(x)
        return _dg(m_bf, hi, ca, cb) + _dg(m_bf, lo, ca, cb)
    h1, h2, h3 = _split3(x)
    return _dg(m_bf, h1, ca, cb) + (_dg(m_bf, h2, ca, cb) + _dg(m_bf, h3, ca, cb))


def _mm_exact_rhs(x, m_bf):
    h1, h2, h3 = _split3(x)
    return _dg(h1, m_bf, 1, 0) + (_dg(h2, m_bf, 1, 0) + _dg(h3, m_bf, 1, 0))


def _group_sum(x, ones_bf):
    w = ones_bf.shape[0]
    hi, lo = _split2(x)
    return jnp.concatenate([_dg(hi[:, j:j + w], ones_bf, 1, 0) + _dg(lo[:, j:j + w], ones_bf, 1, 0)
                            for j in range(0, x.shape[1], w)], axis=1)


def _iota(shape, axis):
    return lax.broadcasted_iota(jnp.int32, shape, axis)


def _tri_masks(n, rev):
    t = _iota((n, n), 0)
    s = _iota((n, n), 1)
    if rev:
        return t < s, t <= s
    return t > s, t >= s


PACK = 4


def _level_masks(n):
    t = _iota((n, PACK * n), 0)
    u = _iota((n, PACK * n), 1) % n
    levels = []
    s = 1
    while s < n:
        levels.append((t // (2 * s) == u // (2 * s)) & (t // s != u // s))
        s *= 2
    diag = _iota((PACK * n, PACK * n), 0) // n == _iota((PACK * n, PACK * n), 1) // n
    return levels, diag


def _block_diag(x, diag):
    return jnp.where(diag, jnp.concatenate([x] * PACK, axis=0), 0.0).astype(BF16)


def _unit_tri_inverse(ms, masks):
    levels, diag = masks
    n = ms[0].shape[0]
    eye = (_iota((n, PACK * n), 0) == _iota((n, PACK * n), 1) % n).astype(F32)
    xs = [eye - jnp.where(levels[0], m, 0.0) for m in ms]
    for mask in levels[1:]:
        zs = [_dg(jnp.where(mask, m, 0.0).astype(BF16), _block_diag(x, diag), 1, 0) for m, x in zip(ms, xs)]
        xs = [x - _dg(x.astype(BF16), _block_diag(z, diag), 1, 0) for x, z in zip(xs, zs)]
    return xs


def _mod_kernel(c_ref, w_ref, b_ref, o_ref):
    s = _silu(c_ref[...])
    o_ref[...] = _mm(s, w_ref[...], hp=True) + b_ref[...]


def _mod_call(cond, ada_w, ada_b):
    rows = cond.shape[0]
    n = ada_w.shape[1]
    bn = 1024
    return pl.pallas_call(
        _mod_kernel,
        out_shape=jax.ShapeDtypeStruct((rows, n), F32),
        grid=(n // bn,),
        in_specs=[pl.BlockSpec((rows, D_MODEL), lambda j: (0, 0)),
                  pl.BlockSpec((D_MODEL, bn), lambda j: (0, j)),
                  pl.BlockSpec((1, bn), lambda j: (0, j))],
        out_specs=pl.BlockSpec((rows, bn), lambda j: (0, j)),
        compiler_params=_cparams(("arbitrary",)),
        name="mod",
    )(cond, ada_w, ada_b)


def _inproj_kernel(x_ref, mod_ref, g_ref, w_ref, or_ref, og_ref):
    x = x_ref[...]
    ms = jnp.mean(x * x, axis=-1, keepdims=True)
    y = x * lax.rsqrt(ms + NORM_EPS) * g_ref[...]
    m = mod_ref[0]
    h = y * (1.0 + m[:, D_MODEL:2 * D_MODEL]) + m[:, :D_MODEL]
    p = jnp.dot(h.astype(BF16), w_ref[...], preferred_element_type=F32)
    or_ref[...] = p[:, :RW].astype(BF16)
    og_ref[...] = p[:, RW:].astype(BF16)


def _inproj_call(x2, mod3, mod_row, norm_g, w_in_bf):
    nt = x2.shape[0]
    ncol = w_in_bf.shape[1]
    return pl.pallas_call(
        _inproj_kernel,
        out_shape=(jax.ShapeDtypeStruct((nt, RW), BF16),
                   jax.ShapeDtypeStruct((nt, GW_PAD), BF16)),
        grid=(nt // TOK_TILE,),
        in_specs=[pl.BlockSpec((TOK_TILE, D_MODEL), lambda i: (i, 0)),
                  pl.BlockSpec((1, 1, 6 * D_MODEL), lambda i: (mod_row(i * TOK_TILE), 0, 0)),
                  pl.BlockSpec((1, D_MODEL), lambda i: (0, 0)),
                  pl.BlockSpec((D_MODEL, ncol), lambda i: (0, 0))],
        out_specs=(pl.BlockSpec((TOK_TILE, RW), lambda i: (i, 0)),
                   pl.BlockSpec((TOK_TILE, GW_PAD), lambda i: (i, 0))),
        compiler_params=_cparams(("arbitrary",)),
        name="inproj",
    )(x2, mod3, norm_g, w_in_bf)


def _head_blocks(width, head):
    return (_iota((width, width), 0) // head == _iota((width, width), 1) // head).astype(BF16)


def _rwkv_kernel(p_ref, *refs, seq, nrow, has_s0):
    s0_ref = refs[0] if has_s0 else None
    (w0_ref, wd_ref, a0_ref, wa_ref, g2_ref, kk_ref, ka_ref, rk_ref, lng_ref, lnb_ref,
     o_ref, st_ref, y_scr) = refs[1:] if has_s0 else refs
    C = CHUNK
    nc = seq // C
    blk = _head_blocks(2 * LANES, R_HEAD)
    k_k = kk_ref[...]
    k_a = ka_ref[...]
    r_k = rk_ref[...]
    exp_m05 = math.exp(-0.5)
    inv_masks = _level_masks(C)

    def lora_lhs(la):
        return jnp.where(_iota(la.shape, 1) < 64, jnp.tanh(la), la)

    def cols(n, rows, lo, hi):
        return p_ref[n, rows, lo:hi].astype(F32)

    assert R_HEAD == C
    PW = PACK * R_HEAD
    diag = inv_masks[1]
    pk_t = _iota((C, PW), 0)
    pk_s = _iota((C, PW), 1) % C
    dirs = []
    for d, rev in ((0, False), (1, True)):
        dirs.append(dict(d=d, rev=rev, tri=_tri_masks(C, rev)[1].astype(BF16),
                         strict=(pk_t < pk_s) if rev else (pk_t > pk_s),
                         incl=(pk_t <= pk_s) if rev else (pk_t >= pk_s)))
    hd = range(R_HEADS)
    sls = [slice(h * R_HEAD, (h + 1) * R_HEAD) for h in hd]

    def prep(n, dr, c):
        d, rev, tri = dr["d"], dr["rev"], dr["tri"]
        rows = pl.ds(pl.multiple_of(c * C, C), C)
        rr = cols(n, rows, 0, R_WIDTH)
        kx = cols(n, rows, R_WIDTH, 2 * R_WIDTH)
        vv = cols(n, rows, 2 * R_WIDTH, 3 * R_WIDTH)
        lhs = lora_lhs(cols(n, rows, 3 * R_WIDTH, 3 * R_WIDTH + LANES))
        logw = -exp_m05 * _sigmoid(w0_ref[d] + _mm(lhs, wd_ref[d]))
        a = _sigmoid(a0_ref[d] + _mm(lhs, wa_ref[d]))
        kkr = kx * k_k
        kk = kkr * lax.rsqrt(_group_sum(kkr * kkr, blk) + 1e-6)
        kd = kx * (1.0 + (a - 1.0) * k_a)
        bvec = kk * a
        gi = _mm_exact_lhs(tri, logw, parts=2)
        gtot = gi[0:1] if rev else gi[C - 1:C]
        en = jnp.exp(-gi)
        ee = jnp.exp(gtot - gi)
        return dict(rows=rows, vv=vv, rt=rr * jnp.exp(gi), at=-kk * jnp.exp(gi - logw), bt=bvec * en, kt=kd * en,
                    bh=bvec * ee, kh=kd * ee, gc=jnp.exp(gtot))

    seqs = range(nrow)
    packs = [(n, d, g) for n in seqs for d in (0, 1) for g in range(R_HEADS // PACK)]

    def step(i, carry):
        ops = {(n, d): prep(n, dirs[d], (nc - 1 - i) if d else i) for n in seqs for d in (0, 1)}
        pcol = lambda name, n, d, g: ops[n, d][name][:, g * PW:(g + 1) * PW]
        bd = lambda x: _block_diag(x, diag)
        ars = {k: jnp.concatenate([pcol("at", *k), pcol("rt", *k)], axis=0).astype(BF16) for k in packs}
        pb = {k: _dg(ars[k], bd(pcol("bt", *k)), 1, 1) for k in packs}
        pk = {k: _dg(ars[k], bd(pcol("kt", *k)), 1, 1) for k in packs}
        sts = {(n, d, g): carry[2 * n + d][:, g * PW:(g + 1) * PW] for n, d, g in packs}
        x0s = {k: _dg(ars[k], bd(sts[k]), 1, 1) for k in packs}
        bdv = {k: bd(pcol("vv", *k)) for k in packs}
        akv = {(n, d, g): _dg(jnp.where(dirs[d]["strict"], pk[n, d, g][:C], 0.0).astype(BF16), bdv[n, d, g], 1, 0)
               for n, d, g in packs}
        tms = dict(zip(packs, _unit_tri_inverse(
            [jnp.where(dirs[d]["strict"], -pb[n, d, g][:C], 0.0) for n, d, g in packs], inv_masks)))
        us = {k: _dg(tms[k].astype(BF16), bd(x0s[k][:C] + akv[k]), 1, 0) for k in packs}
        ys = {(n, d, g): x0s[n, d, g][C:] + _dg(
            jnp.concatenate([jnp.where(dirs[d]["incl"], pb[n, d, g][C:], 0.0),
                             jnp.where(dirs[d]["incl"], pk[n, d, g][C:], 0.0)], axis=1).astype(BF16),
            jnp.concatenate([bd(us[n, d, g]), bdv[n, d, g]], axis=0), 1, 0) for n, d, g in packs}
        snew = {}
        for n, d, g in packs:
            full = _dg(jnp.concatenate([us[n, d, g], pcol("vv", n, d, g)], axis=0).astype(BF16),
                       jnp.concatenate([pcol("bh", n, d, g), pcol("kh", n, d, g)], axis=0).astype(BF16), 0, 0)
            own = jnp.where(diag, full, 0.0)
            upd = own[0:R_HEAD]
            for j in range(1, PACK):
                upd = upd + own[j * R_HEAD:(j + 1) * R_HEAD]
            snew[n, d, g] = sts[n, d, g] * pcol("gc", n, d, g) + upd
        for n, d, g in packs:
            y_scr[n, d, ops[n, d]["rows"], g * PW:(g + 1) * PW] = ys[n, d, g]
        return tuple(jnp.concatenate([snew[n, d, g] for g in range(R_HEADS // PACK)], axis=1)
                     for n in seqs for d in (0, 1))

    def initial(n, d):
        if not has_s0:
            return jnp.zeros((R_HEAD, R_WIDTH), F32)
        return jnp.concatenate([s0_ref[n, d, h] for h in hd], axis=1)

    s_fin = lax.fori_loop(0, nc, step, tuple(initial(n, d) for n in seqs for d in (0, 1)))
    for n in seqs:
        for d in (0, 1):
            for h in hd:
                st_ref[n, d, h] = s_fin[2 * n + d][:, sls[h]]

    g2 = g2_ref[...]
    lng = lng_ref[...]
    lnb = lnb_ref[...]
    inv_n = 1.0 / R_HEAD
    fb = min(seq, 256)

    def finish(j, carry):
        rows = pl.ds(pl.multiple_of(j * fb, fb), fb)
        for n in seqs:
            y = y_scr[n, 0, rows, :] + y_scr[n, 1, rows, :]
            mu = _group_sum(y, blk) * inv_n
            yc = y - mu
            var = _group_sum(yc * yc, blk) * inv_n
            yn = yc * lax.rsqrt(var + LNX_EPS) * lng + lnb
            lhs = lora_lhs(cols(n, rows, 3 * R_WIDTH, 3 * R_WIDTH + LANES))
            a_sum = _sigmoid(a0_ref[0] + _mm(lhs, wa_ref[0])) + _sigmoid(a0_ref[1] + _mm(lhs, wa_ref[1]))
            rr = cols(n, rows, 0, R_WIDTH)
            kx = cols(n, rows, R_WIDTH, 2 * R_WIDTH)
            kd_sum = kx * (2.0 + (a_sum - 2.0) * k_a)
            bonus = _group_sum(rr * kd_sum * r_k, blk) * cols(n, rows, 2 * R_WIDTH, 3 * R_WIDTH)
            gl = cols(n, rows, 3 * R_WIDTH + LANES, 3 * R_WIDTH + 2 * LANES)
            gate = _mm(_sigmoid(gl), g2)
            o_ref[n, rows, :] = (yn + bonus) * gate
        return carry

    lax.fori_loop(0, seq // fb, finish, 0)


def _rwkv_call(proj_r, s0, w, *, batch, seq, nrow):
    p3 = proj_r.reshape(batch, seq, RW)
    kern = functools.partial(_rwkv_kernel, seq=seq, nrow=nrow, has_s0=s0 is not None)
    full = lambda a: pl.BlockSpec(a.shape, lambda b: (0,) * a.ndim)
    ws = [w["w0"], w["wd"], w["a0"], w["wa"], w["g2"], w["k_k"], w["k_a"], w["r_k"], w["lnx_g"], w["lnx_b"]]
    state_spec = pl.BlockSpec((nrow, 2, R_HEADS, R_HEAD, R_HEAD), lambda b: (b, 0, 0, 0, 0))
    states = [] if s0 is None else [s0]
    o, st = pl.pallas_call(
        kern,
        out_shape=(jax.ShapeDtypeStruct((batch, seq, R_WIDTH), F32),
                   jax.ShapeDtypeStruct((batch, 2, R_HEADS, R_HEAD, R_HEAD), F32)),
        grid=(batch // nrow,),
        in_specs=[pl.BlockSpec((nrow, seq, RW), lambda b: (b, 0, 0))] + [state_spec] * len(states)
        + [full(a) for a in ws],
        out_specs=(pl.BlockSpec((nrow, seq, R_WIDTH), lambda b: (b, 0, 0)), state_spec),
        scratch_shapes=[pltpu.VMEM((nrow, 2, seq, R_WIDTH), F32)],
        compiler_params=_cparams(("arbitrary",)),
        name="rwkv",
    )(p3, *states, *ws)
    return o.reshape(batch * seq, R_WIDTH), st


def _rwkv_weights(lp):
    zeros = jnp.zeros((2, 64, R_WIDTH), F32)
    row = lambda a: jnp.asarray(a, F32).reshape(1, R_WIDTH)
    return {
        "w0": jnp.asarray(lp["rwkv_w0"], F32).reshape(2, 1, R_WIDTH),
        "wd": jnp.concatenate([jnp.asarray(lp["rwkv_w2"], F32), zeros], axis=1),
        "a0": jnp.asarray(lp["rwkv_a0"], F32).reshape(2, 1, R_WIDTH),
        "wa": jnp.concatenate([zeros, jnp.asarray(lp["rwkv_a2"], F32)], axis=1),
        "g2": jnp.asarray(lp["rwkv_g2"], F32),
        "k_k": row(lp["rwkv_k_k"]), "k_a": row(lp["rwkv_k_a"]), "r_k": row(lp["rwkv_r_k"]),
        "lnx_g": row(lp["rwkv_lnx_g"]), "lnx_b": row(lp["rwkv_lnx_b"]),
    }


def _gdn_kernel(p_ref, *refs, seq, nrow, row_len, has_s0):
    s0_ref = refs[0] if has_s0 else None
    (cw_ref, alog_ref, dtb_ref, ng_ref, o_ref, st_ref, q_scr, k_scr, v_scr, o_scr) = refs[1:] if has_s0 else refs
    C = CHUNK
    nc = seq // C
    fb = min(seq, 256)
    W = G_WIDTH
    q_scale = G_HEAD ** -0.5
    seqs = range(nrow)
    hd = range(G_HEADS)
    hss = [slice(h * G_HEAD, (h + 1) * G_HEAD) for h in hd]

    cw0 = cw_ref[0:1, :]
    cw1 = cw_ref[1:2, :]
    cw2 = cw_ref[2:3, :]
    tpos = _iota((fb, 1), 0) % row_len
    is_first = tpos == 0
    is_last = tpos == row_len - 1

    def prep(j, carry):
        rows = pl.ds(pl.multiple_of(j * fb, fb), fb)
        for n in seqs:
            x = p_ref[n, rows, 0:3 * W].astype(F32)
            xm = jnp.where(is_first, 0.0, pltpu.roll(x, 1, 0))
            xp = jnp.where(is_last, 0.0, pltpu.roll(x, fb - 1, 0))
            y = _silu(cw0 * xm + cw1 * x + cw2 * xp)
            for h in hd:
                qh = y[:, hss[h]]
                kh = y[:, W + h * G_HEAD:W + (h + 1) * G_HEAD]
                q_scr[n, rows, hss[h]] = qh * lax.rsqrt(jnp.sum(qh * qh, axis=-1, keepdims=True) + 1e-6) * q_scale
                k_scr[n, rows, hss[h]] = kh * lax.rsqrt(jnp.sum(kh * kh, axis=-1, keepdims=True) + 1e-6)
            v_scr[n, rows, :] = y[:, 2 * W:3 * W]
        return carry

    lax.fori_loop(0, seq // fb, prep, 0)

    assert G_HEADS == PACK
    inv_masks = _level_masks(C)
    alog = alog_ref[...]
    dtb = dtb_ref[...]
    sel_row = _iota((LANES, W), 0)
    sel_head = _iota((LANES, W), 1) // G_HEAD
    pk_row = _iota((LANES, PACK * C), 0)
    pk_head = _iota((LANES, PACK * C), 1) // C
    pk_t = _iota((C, PACK * C), 0)
    pk_s = _iota((C, PACK * C), 1) % C

    dirs = []
    for d, rev in ((0, False), (1, True)):
        dirs.append(dict(
            d=d, rev=rev, tri=_tri_masks(C, rev)[1].astype(BF16),
            strict=(pk_t < pk_s) if rev else (pk_t > pk_s),
            incl=(pk_t <= pk_s) if rev else (pk_t >= pk_s),
            upto=(pk_t >= pk_s) if rev else (pk_t <= pk_s),
            e_beta=(sel_row == 4 * d + sel_head).astype(BF16),
            e_g=(sel_row == 8 + 4 * d + sel_head).astype(BF16),
            e_g_pk=(pk_row == 8 + 4 * d + pk_head).astype(BF16)))
    items = [(n, d, h) for n in seqs for d in (0, 1) for h in hd]
    packs = [(n, d) for n in seqs for d in (0, 1)]

    def prep_dir(n, dr, c):
        rows = pl.ds(pl.multiple_of(c * C, C), C)
        ba = p_ref[n, rows, 4 * W:4 * W + LANES].astype(F32)
        gval = -jnp.exp(alog) * _softplus(ba + dtb)
        beta_b = _mm_exact_rhs(_sigmoid(ba), dr["e_beta"])
        gcum = _mm_exact_lhs(dr["tri"], _mm_exact_rhs(gval, dr["e_g"]))
        g_pk = _mm_exact_rhs(gval, dr["e_g_pk"])
        g_col = _mm_exact_lhs(dr["tri"], g_pk)
        g_row = jnp.sum(jnp.where(dr["upto"], g_pk, 0.0), axis=0, keepdims=True)
        return dict(rows=rows, beta=beta_b, gcum=gcum,
                    decay=jnp.where(dr["incl"], jnp.exp(jnp.where(dr["incl"], g_col - g_row, 0.0)), 0.0),
                    q=q_scr[n, rows, :], k=k_scr[n, rows, :], v=v_scr[n, rows, :])

    def step(i, carry):
        ops = {(n, d): prep_dir(n, dirs[d], (nc - 1 - i) if d else i) for n, d in packs}
        hcol = lambda name, n, d, h: ops[n, d][name][:, hss[h]]
        qhs = {k: hcol("q", *k) for k in items}
        khs = {k: hcol("k", *k) for k in items}
        gcs = {k: hcol("gcum", *k) for k in items}
        bbs = {k: hcol("beta", *k) for k in items}
        kbs = {k: khs[k] * bbs[k] for k in items}
        kks = {k: _mm(jnp.concatenate([kbs[k], qhs[k]], axis=0), khs[k], 1, 1) for k in items}
        kka = {(n, d): jnp.concatenate([kks[n, d, h][:C] for h in hd], axis=1) for n, d in packs}
        tcat = dict(zip(packs, _unit_tri_inverse(
            [jnp.where(dirs[d]["strict"], kka[n, d] * ops[n, d]["decay"], 0.0) for n, d in packs], inv_masks)))
        egs = {k: jnp.exp(gcs[k]) for k in items}
        uws = {(n, d, h): _mm(tcat[n, d][:, h * C:(h + 1) * C],
                              jnp.concatenate([hcol("v", n, d, h) * bbs[n, d, h], kbs[n, d, h] * egs[n, d, h]], axis=1))
               for n, d, h in items}
        shs = {(n, d, h): carry[2 * n + d][:, hss[h]] for n, d, h in items}
        wss = {k: _mm(jnp.concatenate([uws[k][:, G_HEAD:], qhs[k] * egs[k]], axis=0), shs[k]) for k in items}
        vns = {k: uws[k][:, :G_HEAD] - wss[k][:C] for k in items}
        outs = {(n, d, h): wss[n, d, h][C:]
                + _mm(kks[n, d, h][C:] * ops[n, d]["decay"][:, h * C:(h + 1) * C], vns[n, d, h]) for n, d, h in items}
        snew = {}
        for n, d, h in items:
            glast = gcs[n, d, h][0:1] if dirs[d]["rev"] else gcs[n, d, h][C - 1:C]
            k_dec = khs[n, d, h] * jnp.exp(glast - gcs[n, d, h])
            snew[n, d, h] = shs[n, d, h] * jnp.exp(glast) + _mm(k_dec, vns[n, d, h], 0, 0)
        for n, d in packs:
            o_scr[n, d, ops[n, d]["rows"], :] = jnp.concatenate([outs[n, d, h] for h in hd], axis=1)
        return tuple(jnp.concatenate([snew[n, d, h] for h in hd], axis=1) for n, d in packs)

    def initial(n, d):
        if not has_s0:
            return jnp.zeros((G_HEAD, W), F32)
        return jnp.concatenate([s0_ref[n, d, h] for h in hd], axis=1)

    s_fin = lax.fori_loop(0, nc, step, tuple(initial(n, d) for n, d in packs))
    for j, (n, d) in enumerate(packs):
        for h in hd:
            st_ref[n, d, h] = s_fin[j][:, hss[h]]

    ng = ng_ref[...]

    def finish(j, carry):
        rows = pl.ds(pl.multiple_of(j * fb, fb), fb)
        for n in seqs:
            o = o_scr[n, 0, rows, :] + o_scr[n, 1, rows, :]
            z = p_ref[n, rows, 3 * W:4 * W].astype(F32)
            for h in hd:
                oh = o[:, hss[h]]
                ms = jnp.mean(oh * oh, axis=-1, keepdims=True)
                o_ref[n, rows, hss[h]] = oh * lax.rsqrt(ms + NORM_EPS) * ng * _silu(z[:, hss[h]])
        return carry

    lax.fori_loop(0, seq // fb, finish, 0)


def _gdn_call(proj_g, s0, w, *, batch, seq, nrow, row_len):
    p3 = proj_g.reshape(batch, seq, GW_PAD)
    kern = functools.partial(_gdn_kernel, seq=seq, nrow=nrow, row_len=row_len, has_s0=s0 is not None)
    full = lambda a: pl.BlockSpec(a.shape, lambda b: (0,) * a.ndim)
    ws = [w["conv"], w["alog"], w["dtb"], w["norm_g"]]
    state_spec = pl.BlockSpec((nrow, 2, G_HEADS, G_HEAD, G_HEAD), lambda b: (b, 0, 0, 0, 0))
    states = [] if s0 is None else [s0]
    o, st = pl.pallas_call(
        kern,
        out_shape=(jax.ShapeDtypeStruct((batch, seq, G_WIDTH), F32),
                   jax.ShapeDtypeStruct((batch, 2, G_HEADS, G_HEAD, G_HEAD), F32)),
        grid=(batch // nrow,),
        in_specs=[pl.BlockSpec((nrow, seq, GW_PAD), lambda b: (b, 0, 0))] + [state_spec] * len(states)
        + [full(a) for a in ws],
        out_specs=(pl.BlockSpec((nrow, seq, G_WIDTH), lambda b: (b, 0, 0)), state_spec),
        scratch_shapes=[pltpu.VMEM((nrow, seq, G_WIDTH), F32)] * 3 + [pltpu.VMEM((nrow, 2, seq, G_WIDTH), F32)],
        compiler_params=_cparams(("arbitrary",)),
        name="gdn",
    )(p3, *states, *ws)
    return o.reshape(batch * seq, G_WIDTH), st


def _gdn_weights(lp):
    lanes = jnp.zeros((1, LANES), F32)
    return {
        "conv": jnp.asarray(lp["gdn_conv"], F32),
        "alog": lanes.at[0, 8:16].set(jnp.asarray(lp["gdn_a_log"], F32).reshape(8)),
        "dtb": lanes.at[0, 8:16].set(jnp.asarray(lp["gdn_dt_bias"], F32).reshape(8)),
        "norm_g": jnp.asarray(lp["gdn_norm_g"], F32).reshape(1, G_HEAD),
    }


def _outproj_kernel(x_ref, or_ref, og_ref, mod_ref, wo_ref, n2_ref, rw_ref, rb_ref,
                    x1_ref, h2_ref, rid_ref, rgate_ref):
    m = mod_ref[0]
    g1 = m[:, 2 * D_MODEL:3 * D_MODEL]
    sh2 = m[:, 3 * D_MODEL:4 * D_MODEL]
    sc2 = m[:, 4 * D_MODEL:5 * D_MODEL]
    mix = (jnp.dot(or_ref[...].astype(BF16), wo_ref[0:R_WIDTH, :], preferred_element_type=F32)
           + jnp.dot(og_ref[...].astype(BF16), wo_ref[R_WIDTH:, :], preferred_element_type=F32))
    x1 = x_ref[...] + g1 * mix
    x1_ref[...] = x1
    ms = jnp.mean(x1 * x1, axis=-1, keepdims=True)
    h2 = x1 * lax.rsqrt(ms + NORM_EPS) * n2_ref[...] * (1.0 + sc2) + sh2
    h2_ref[...] = h2

    logits = _mm(h2, rw_ref[...], hp=True) + rb_ref[...]
    lane = _iota(logits.shape, 1)
    neg = jnp.float32(-1e30)
    big = jnp.int32(1 << 20)

    def first_argmax(v):
        mx = jnp.max(v, axis=-1, keepdims=True)
        idx = jnp.min(jnp.where(v == mx, lane, big), axis=-1, keepdims=True)
        return mx, idx

    lg = jnp.where(lane < N_GROUPS, logits, neg)
    mg, grp = first_argmax(lg)
    p_grp = 1.0 / jnp.sum(jnp.where(lane < N_GROUPS, jnp.exp(lg - mg), 0.0), axis=-1, keepdims=True)
    in_grp = (lane >= N_GROUPS) & (lane < N_GROUPS + N_EXPERTS) & ((lane - N_GROUPS) // EXPERTS_PER_GROUP == grp)
    le = jnp.where(in_grp, logits, neg)
    m1, i1 = first_argmax(le)
    m2, i2 = first_argmax(jnp.where(lane == i1, neg, le))
    e2 = jnp.exp(m2 - m1)
    w1 = p_grp / (1.0 + e2)
    w2 = p_grp * e2 / (1.0 + e2)
    rid_ref[...] = jnp.where(lane == 0, i1 - N_GROUPS, jnp.where(lane == 1, i2 - N_GROUPS, 0))
    rgate_ref[...] = jnp.where(lane == 0, w1, jnp.where(lane == 1, w2, 0.0))


def _outproj_call(x2, o_r, o_g, mod3, mod_row, wo_bf, n2g, rw, rb):
    nt = x2.shape[0]
    tile = lambda w: pl.BlockSpec((OUT_TILE, w), lambda i: (i, 0))
    const = lambda a: pl.BlockSpec(a.shape, lambda i: (0,) * a.ndim)
    return pl.pallas_call(
        _outproj_kernel,
        out_shape=(jax.ShapeDtypeStruct((nt, D_MODEL), F32), jax.ShapeDtypeStruct((nt, D_MODEL), F32),
                   jax.ShapeDtypeStruct((nt, LANES), jnp.int32), jax.ShapeDtypeStruct((nt, LANES), F32)),
        grid=(nt // OUT_TILE,),
        in_specs=[tile(D_MODEL), tile(R_WIDTH), tile(G_WIDTH),
                  pl.BlockSpec((1, 1, 6 * D_MODEL), lambda i: (mod_row(i * OUT_TILE), 0, 0)),
                  const(wo_bf), const(n2g), const(rw), const(rb)],
        out_specs=(tile(D_MODEL), tile(D_MODEL), tile(LANES), tile(LANES)),
        compiler_params=_cparams(("arbitrary",)),
        name="outproj",
    )(x2, o_r, o_g, mod3, wo_bf, n2g, rw, rb)


PLAN_TILE = 1024


def _plan_kernel(rid_ref, cnt_in_ref, route_ref, cnt_ref, carry):
    i = pl.program_id(0)

    @pl.when(i == 0)
    def _():
        carry[...] = cnt_in_ref[...].astype(F32)

    rid = rid_ref[...]
    lane = _iota(rid.shape, 1)
    e0 = jnp.sum(jnp.where(lane == 0, rid, 0), axis=-1, keepdims=True)
    e1 = jnp.sum(jnp.where(lane == 1, rid, 0), axis=-1, keepdims=True)
    oh0 = (lane == e0).astype(F32)
    oh1 = (lane == e1).astype(F32)
    oh = oh0 + oh1
    n = rid.shape[0]
    earlier = (_iota((n, n), 0) > _iota((n, n), 1)).astype(BF16)
    before = jnp.dot(earlier, oh.astype(BF16), preferred_element_type=F32) + carry[0:1, :]
    r0 = jnp.sum(oh0 * before, axis=-1, keepdims=True)
    r1 = jnp.sum(oh1 * before, axis=-1, keepdims=True)
    cols = jnp.where(lane == 0, e0.astype(F32), jnp.where(lane == 1, e1.astype(F32),
                                                          jnp.where(lane == 2, r0, jnp.where(lane == 3, r1, 0.0))))
    route_ref[...] = jnp.transpose(cols)[0:8, :].astype(jnp.int32)
    total = carry[0:1, :] + jnp.sum(oh, axis=0, keepdims=True)
    carry[...] = jnp.broadcast_to(total, carry.shape)
    cnt_ref[...] = jnp.broadcast_to(total, cnt_ref.shape).astype(jnp.int32)


def _plan_call(rid, cnt_in):
    nt = rid.shape[0]
    return pl.pallas_call(
        _plan_kernel,
        out_shape=(jax.ShapeDtypeStruct((8, nt), jnp.int32), jax.ShapeDtypeStruct((8, LANES), jnp.int32)),
        grid=(nt // PLAN_TILE,),
        in_specs=[pl.BlockSpec((PLAN_TILE, LANES), lambda i: (i, 0)), pl.BlockSpec((8, LANES), lambda i: (0, 0))],
        out_specs=(pl.BlockSpec((8, PLAN_TILE), lambda i: (0, i)), pl.BlockSpec((8, LANES), lambda i: (0, 0))),
        scratch_shapes=[pltpu.VMEM((8, LANES), F32)],
        compiler_params=_cparams(("arbitrary",)),
        name="plan",
    )(rid, cnt_in)


def _slots_kernel(off_ref, route_ref, slot_ref):
    route = route_ref[...]
    ids = jnp.concatenate([route[0:2], route[0:2], route[0:2], route[0:2]], axis=0)
    rank = jnp.concatenate([route[2:4], route[2:4], route[2:4], route[2:4]], axis=0)
    base = jnp.zeros_like(ids)
    for e in range(N_EXPERTS):
        base = jnp.where(ids == e, off_ref[e], base)
    slot_ref[...] = base + rank


def _slots_call(off, route):
    nt = route.shape[1]
    tile = min(nt, 2048)
    return pl.pallas_call(
        _slots_kernel,
        out_shape=jax.ShapeDtypeStruct((8, nt), jnp.int32),
        grid=(nt // tile,),
        in_specs=[pl.BlockSpec(memory_space=pltpu.SMEM), pl.BlockSpec((8, tile), lambda i: (0, i))],
        out_specs=pl.BlockSpec((8, tile), lambda i: (0, i)),
        compiler_params=_cparams(("arbitrary",)),
        name="slots",
    )(off, route)


def _dispatch_kernel(slot_ref, h2a_ref, h2b_ref, pad_ref, xs_ref, zero_scr, sem, *, tiles_a, n_rows):
    i = pl.program_id(0)

    @pl.when(i == 0)
    def zero_unrouted_slots():
        zero_scr[...] = jnp.zeros_like(zero_scr)
        nblk = n_rows // MOE_ROWS
        aligned = [s for s in (256, 128, 64, 32, 16, 8) if s < MOE_ROWS]

        def pad_copies(act):
            for e in range(N_EXPERTS):
                pos = pad_ref[0, e]
                pad = pad_ref[1, e]
                for s in aligned:
                    hit = (pad & s) != 0
                    first = pos - s

                    @pl.when(hit)
                    def _(first=first, s=s):
                        act(pltpu.make_async_copy(zero_scr.at[pl.ds(0, s)],
                                                  xs_ref.at[pl.ds(pl.multiple_of(first, 8), s)], sem))

                    pos = jnp.where(hit, first, pos)
                rem = pad & 7
                for j in range(7):
                    @pl.when(j < rem)
                    def _(pos=pos, j=j):
                        act(pltpu.make_async_copy(zero_scr.at[pl.ds(0, 1)], xs_ref.at[pl.ds(pos - 1 - j, 1)], sem))

        def block_copy(b):
            return pltpu.make_async_copy(zero_scr, xs_ref.at[pl.ds(pl.multiple_of(b * MOE_ROWS, MOE_ROWS),
                                                                   MOE_ROWS)], sem)

        def start(b, c):
            block_copy(b).start()
            return c

        def wait(b, c):
            block_copy(b).wait()
            return c

        pad_copies(lambda cp: cp.start())
        lax.fori_loop(pad_ref[2, 0], nblk, start, 0)
        pad_copies(lambda cp: cp.wait())
        lax.fori_loop(pad_ref[2, 0], nblk, wait, 0)

    def scatter_tile(h2_ref):
        def start(t, c):
            for k in range(2):
                pltpu.make_async_copy(h2_ref.at[pl.ds(t, 1)], xs_ref.at[pl.ds(slot_ref[k, t], 1)], sem).start()
            return c

        lax.fori_loop(0, TOK_TILE, start, 0, unroll=8)
        for _ in range(2):
            pltpu.make_async_copy(h2_ref, xs_ref.at[pl.ds(0, TOK_TILE)], sem).wait()

    @pl.when(i < tiles_a)
    def _():
        scatter_tile(h2a_ref)

    @pl.when(i >= tiles_a)
    def _():
        scatter_tile(h2b_ref)


def _route_spec(shift=0, last=None):
    if shift == 0:
        return pl.BlockSpec((8, TOK_TILE), lambda i: (0, i), memory_space=pltpu.SMEM)
    return pl.BlockSpec((8, TOK_TILE), lambda i: (0, jnp.minimum(i + shift, last)), memory_space=pltpu.SMEM)


def _dispatch_call(slots, h2a, h2b, n_rows, pads):
    tiles_a = h2a.shape[0] // TOK_TILE
    tiles_b = h2b.shape[0] // TOK_TILE
    kern = functools.partial(_dispatch_kernel, tiles_a=tiles_a, n_rows=n_rows)
    return pl.pallas_call(
        kern,
        out_shape=jax.ShapeDtypeStruct((n_rows, D_MODEL), F32),
        grid=(tiles_a + tiles_b,),
        in_specs=[_route_spec(),
                  pl.BlockSpec((TOK_TILE, D_MODEL), lambda i: (jnp.minimum(i, tiles_a - 1), 0)),
                  pl.BlockSpec((TOK_TILE, D_MODEL), lambda i: (jnp.maximum(i - tiles_a, 0), 0)),
                  pl.BlockSpec(memory_space=pltpu.SMEM)],
        out_specs=pl.BlockSpec(memory_space=pl.ANY),
        scratch_shapes=[pltpu.VMEM((MOE_ROWS, D_MODEL), F32), pltpu.SemaphoreType.DMA(())],
        compiler_params=_cparams(("arbitrary",)),
        name="dispatch",
    )(slots, h2a, h2b, pads)


def _experts_kernel(be_ref, nu_ref, xs_ref, wg_ref, wu_ref, wd_ref, ys_ref, wg_bf, wu_bf, wd_bf):
    b = pl.program_id(0)
    used = b < nu_ref[0]

    @pl.when(used)
    def _():
        prev = be_ref[jnp.maximum(b - 1, 0)]

        x = xs_ref[...]
        g = jnp.dot(x, wg_ref[0], preferred_element_type=F32)
        u = jnp.dot(x, wu_ref[0], preferred_element_type=F32)
        h = _silu(g) * u
        ys_ref[...] = jnp.dot(h, wd_ref[0], preferred_element_type=F32)

    @pl.when(jnp.logical_not(used))
    def _():
        ys_ref[...] = jnp.zeros_like(ys_ref)


def _experts_call(blk_e, n_used, xs, wg, wu, wd):
    n_rows = xs.shape[0]
    nb = n_rows // MOE_ROWS
    grid_spec = pltpu.PrefetchScalarGridSpec(
        num_scalar_prefetch=2,
        grid=(nb,),
        in_specs=[pl.BlockSpec((MOE_ROWS, D_MODEL), lambda b, be, nu: (jnp.minimum(b, nu[0] - 1), 0)),
                  pl.BlockSpec((1, D_MODEL, D_EXPERT), lambda b, be, nu: (be[b], 0, 0)),
                  pl.BlockSpec((1, D_MODEL, D_EXPERT), lambda b, be, nu: (be[b], 0, 0)),
                  pl.BlockSpec((1, D_EXPERT, D_MODEL), lambda b, be, nu: (be[b], 0, 0))],
        out_specs=pl.BlockSpec((MOE_ROWS, D_MODEL), lambda b, be, nu: (b, 0)),
        scratch_shapes=[pltpu.VMEM((D_MODEL, D_EXPERT), BF16), pltpu.VMEM((D_MODEL, D_EXPERT), BF16),
                        pltpu.VMEM((D_EXPERT, D_MODEL), BF16)],
    )
    return pl.pallas_call(
        _experts_kernel,
        out_shape=jax.ShapeDtypeStruct((n_rows, D_MODEL), F32),
        grid_spec=grid_spec,
        compiler_params=_cparams(("arbitrary",)),
        name="experts",
    )(blk_e, n_used, xs, wg, wu, wd)


def _combine_kernel(slot_ref, slot_next_ref, x1_ref, gate_ref, mod_ref, fg_ref, ys_ref, y_ref, buf, sems):
    i = pl.program_id(0)
    n = pl.num_programs(0)
    slot = i % 2

    def gather(slots, s):
        def start(t, c):
            for k in range(2):
                pltpu.make_async_copy(ys_ref.at[pl.ds(slots[k, t], 1)], buf.at[s, k, pl.ds(t, 1)], sems.at[s]).start()
            return c

        lax.fori_loop(0, TOK_TILE, start, 0, unroll=8)

    @pl.when(i == 0)
    def _():
        gather(slot_ref, 0)

    @pl.when(i + 1 < n)
    def _():
        gather(slot_next_ref, 1 - slot)

    for k in range(2):
        pltpu.make_async_copy(ys_ref.at[pl.ds(0, TOK_TILE)], buf.at[slot, k], sems.at[slot]).wait()

    gate = gate_ref[...]
    lane = _iota(gate.shape, 1)
    w0 = jnp.sum(jnp.where(lane == 0, gate, 0.0), axis=-1, keepdims=True)
    w1 = jnp.sum(jnp.where(lane == 1, gate, 0.0), axis=-1, keepdims=True)
    g2 = mod_ref[0][:, 5 * D_MODEL:6 * D_MODEL]
    y = x1_ref[...] + g2 * (w0 * buf[slot, 0] + w1 * buf[slot, 1])
    ms = jnp.mean(y * y, axis=-1, keepdims=True)
    y_ref[...] = y * lax.rsqrt(ms + NORM_EPS) * fg_ref[...]


def _combine_call(slots, x1, rgate, mod3, mod_row, fg, ys):
    nt = x1.shape[0]
    n_tiles = nt // TOK_TILE
    return pl.pallas_call(
        _combine_kernel,
        out_shape=jax.ShapeDtypeStruct((nt, D_MODEL), F32),
        grid=(n_tiles,),
        in_specs=[_route_spec(), _route_spec(1, n_tiles - 1),
                  pl.BlockSpec((TOK_TILE, D_MODEL), lambda i: (i, 0)),
                  pl.BlockSpec((TOK_TILE, LANES), lambda i: (i, 0)),
                  pl.BlockSpec((1, 1, 6 * D_MODEL), lambda i: (mod_row(i * TOK_TILE), 0, 0)),
                  pl.BlockSpec((1, D_MODEL), lambda i: (0, 0)),
                  pl.BlockSpec(memory_space=pl.ANY)],
        out_specs=pl.BlockSpec((TOK_TILE, D_MODEL), lambda i: (i, 0)),
        scratch_shapes=[pltpu.VMEM((2, 2, TOK_TILE, D_MODEL), F32), pltpu.SemaphoreType.DMA((2,))],
        compiler_params=_cparams(("arbitrary",)),
        name="combine",
    )(slots, slots, x1, rgate, mod3, fg, ys)


def kernel(x_prompt, x_sample, state_rwkv, state_gdn, c, c_ctx, ada_w, ada_b, norm1_g, norm2_g, w_in, w_out,
           rwkv_w0, rwkv_w2, rwkv_a0, rwkv_a2, rwkv_g2, rwkv_k_k, rwkv_k_a, rwkv_r_k, rwkv_lnx_g, rwkv_lnx_b,
           gdn_conv, gdn_a_log, gdn_dt_bias, gdn_norm_g, router_group_w, router_group_b, router_expert_w,
           router_expert_b, expert_gate, expert_up, expert_down, final_norm_g):
    assert ada_w.shape[0] == 1, "one layer"
    bp, tp, _ = x_prompt.shape
    bs, ts, _ = x_sample.shape
    lp = dict(rwkv_w0=rwkv_w0[0], rwkv_w2=rwkv_w2[0], rwkv_a0=rwkv_a0[0], rwkv_a2=rwkv_a2[0], rwkv_g2=rwkv_g2[0],
              rwkv_k_k=rwkv_k_k[0], rwkv_k_a=rwkv_k_a[0], rwkv_r_k=rwkv_r_k[0], rwkv_lnx_g=rwkv_lnx_g[0],
              rwkv_lnx_b=rwkv_lnx_b[0], gdn_conv=gdn_conv[0], gdn_a_log=gdn_a_log[0],
              gdn_dt_bias=gdn_dt_bias[0], gdn_norm_g=gdn_norm_g[0])
    rw_w = _rwkv_weights(lp)
    gd_w = _gdn_weights(lp)

    n_cond = 1 + bs
    cond = jnp.concatenate([c_ctx[None, :], c, jnp.zeros((16 - n_cond, D_MODEL), F32)], axis=0)
    mod = _mod_call(cond, ada_w[0], ada_b)
    mod3 = mod.reshape(16, 1, 6 * D_MODEL)

    in_cols = w_in.shape[2]
    w_in_bf = jnp.pad(w_in[0], ((0, 0), (0, RW + GW_PAD - in_cols))).astype(BF16)
    w_out_bf = w_out[0].astype(BF16)
    n1g = norm1_g.reshape(1, D_MODEL)
    n2g = norm2_g.reshape(1, D_MODEL)
    fg = final_norm_g.reshape(1, D_MODEL)
    rw = jnp.zeros((D_MODEL, LANES), F32).at[:, :N_GROUPS].set(router_group_w[0])
    rw = rw.at[:, N_GROUPS:N_GROUPS + N_EXPERTS].set(router_expert_w[0])
    rb = jnp.zeros((1, LANES), F32).at[0, :N_GROUPS].set(router_group_b[0])
    rb = rb.at[0, N_GROUPS:N_GROUPS + N_EXPERTS].set(router_expert_b[0])

    passes = [
        dict(x=x_prompt.reshape(bp * tp, D_MODEL), batch=bp, seq=tp, row_len=tp,
             mod_row=lambda tok: 0,
             s_r=None, s_g=None),
        dict(x=x_sample.reshape(bs * ts, D_MODEL), batch=bs, seq=ts, row_len=GRID_W,
             mod_row=lambda tok: 1 + tok // ts,
             s_r=state_rwkv[:, 0], s_g=state_gdn[:, 0]),
    ]

    for p in passes:
        proj_r, proj_g = _inproj_call(p["x"], mod3, p["mod_row"], n1g, w_in_bf)
        nrow = SEQS_PER_STEP_SHORT if p["seq"] <= 256 else SEQS_PER_STEP
        o_r, p["st_r"] = _rwkv_call(proj_r, p["s_r"], rw_w, batch=p["batch"], seq=p["seq"], nrow=nrow)
        o_g, p["st_g"] = _gdn_call(proj_g, p["s_g"], gd_w, batch=p["batch"], seq=p["seq"], nrow=nrow,
                                   row_len=p["row_len"])
        p["x1"], p["h2"], p["rid"], p["rgate"] = _outproj_call(
            p["x"], o_r, o_g, mod3, p["mod_row"], w_out_bf, n2g, rw, rb)

    cnt = jnp.zeros((8, LANES), jnp.int32)
    for p in passes:
        p["route"], cnt = _plan_call(p["rid"], cnt)
    counts = cnt[0, :N_EXPERTS]
    padded = (counts + MOE_ROWS - 1) // MOE_ROWS * MOE_ROWS
    ends = jnp.cumsum(padded)
    off = (ends - padded).astype(jnp.int32)
    nt_all = sum(p["rid"].shape[0] for p in passes)
    n_blocks = (2 * nt_all) // MOE_ROWS + N_EXPERTS
    n_rows = n_blocks * MOE_ROWS
    n_used = (ends[-1] // MOE_ROWS).astype(jnp.int32)
    blk_start = jnp.minimum(jnp.arange(n_blocks, dtype=jnp.int32), n_used - 1) * MOE_ROWS
    blk_e = jnp.minimum(jnp.sum((blk_start[:, None] >= ends[None, :]).astype(jnp.int32), axis=1), N_EXPERTS - 1)

    pads = jnp.zeros((8, LANES), jnp.int32).at[0, :N_EXPERTS].set(ends.astype(jnp.int32))
    pads = pads.at[1, :N_EXPERTS].set((padded - counts).astype(jnp.int32)).at[2, 0].set(n_used)
    for p in passes:
        p["slots"] = _slots_call(off, p["route"])
    xs = _dispatch_call(jnp.concatenate([p["slots"] for p in passes], axis=1), passes[0]["h2"], passes[1]["h2"],
                        n_rows, pads)
    ys = _experts_call(blk_e, n_used.reshape(1), xs, expert_gate[0], expert_up[0], expert_down[0])
    outs = [_combine_call(p["slots"], p["x1"], p["rgate"], mod3, p["mod_row"], fg, ys) for p in passes]

    y_prompt = outs[0].reshape(bp, tp, D_MODEL)
    y_sample = outs[1].reshape(bs, ts, D_MODEL)
    new_state_rwkv = passes[0]["st_r"][:, None]
    new_state_gdn = passes[0]["st_g"][:, None]
    return (y_prompt, y_sample, new_state_rwkv, new_state_gdn)
```
